```python
import math
import jax, jax.numpy as jnp
from jax import lax
import numpy as np

D_MODEL = 1024
BATCH = 2
SEQ = 16384
DEPTH = 1
DEC_BATCH = 16
DEC_SEQ = 32
PAST_LEN = 1024

CHUNK = 64
HEAD_DIM = 64
N_HEADS_A = 4
N_HEADS_B = 8
WIDTH_A = N_HEADS_A * 2 * HEAD_DIM
WIDTH_B = N_HEADS_B * HEAD_DIM
MIX_WIDTH = WIDTH_A + WIDTH_B
IN_COLS = 3 * WIDTH_A + 3 * WIDTH_B
D_FF = 4 * D_MODEL
D_PLE = 256
N_BUCKETS = 32
MAX_DISTANCE = 128
BAND_CHUNKS = 8
WINDOW_B = BAND_CHUNKS * CHUNK
BAND_LEN = WINDOW_B + CHUNK
REL_CLIP = 128
Q_BLOCK = 128
EPS = 1e-6
SUBLN_EPS = 1e-5
NEG = -1e30

kernel_name = "hymba_diff_band_stream_step"


def rmsnorm(x, g, eps=EPS):
    xf = x.astype(jnp.float32)
    y = xf * lax.rsqrt(jnp.mean(xf * xf, axis=-1, keepdims=True) + eps)
    return (y * g.astype(jnp.float32)).astype(x.dtype)


def t5_bucket(rel):
    half = N_BUCKETS // 2
    n = -rel
    ret = jnp.where(n < 0, half, 0)
    n = jnp.abs(n)
    max_exact = half // 2
    nf = jnp.maximum(n, 1).astype(jnp.float32)
    large = max_exact + (jnp.log(nf / max_exact) / math.log(MAX_DISTANCE / max_exact)
                         * (half - max_exact)).astype(jnp.int32)
    large = jnp.minimum(large, half - 1)
    return ret + jnp.where(n < max_exact, n, large)


def t5_bias(table, rel):
    return jnp.transpose(table[t5_bucket(rel)].astype(jnp.float32), (2, 0, 1))


def clip_bias(table, rel):
    idx = jnp.clip(rel, -REL_CLIP, REL_CLIP) + REL_CLIP
    return jnp.transpose(table[idx].astype(jnp.float32), (2, 0, 1))


def split_proj(a, w_in):
    B, T = a.shape[:2]
    z = a @ w_in
    qa, ka, va, qb, kb, vb = jnp.split(
        z, [WIDTH_A, 2 * WIDTH_A, 3 * WIDTH_A, 3 * WIDTH_A + WIDTH_B, 3 * WIDTH_A + 2 * WIDTH_B], axis=-1)
    qa = qa.reshape(B, T, N_HEADS_A, 2, HEAD_DIM)
    ka = ka.reshape(B, T, N_HEADS_A, 2, HEAD_DIM)
    va = va.reshape(B, T, N_HEADS_A, 2 * HEAD_DIM)
    qb = qb.reshape(B, T, N_HEADS_B, HEAD_DIM)
    kb = kb.reshape(B, T, N_HEADS_B, HEAD_DIM)
    vb = vb.reshape(B, T, N_HEADS_B, HEAD_DIM)
    return qa, ka, va, qb, kb, vb


def diff_core(q, k, v, bias, mask, lam):
    s = jnp.einsum('bqhmd,bkhmd->bhmqk', q, k).astype(jnp.float32) * (HEAD_DIM ** -0.5) + bias[:, None]
    s = jnp.where(mask, s, NEG)
    p = jax.nn.softmax(s, axis=-1)
    w = p[:, :, 0] - lam * p[:, :, 1]
    return jnp.einsum('bhqk,bkhe->bqhe', w.astype(v.dtype), v)


def band_core(q, k, v, bias, mask):
    s = jnp.einsum('bqhd,bkhd->bhqk', q, k).astype(jnp.float32) * (HEAD_DIM ** -0.5) + bias[None]
    s = jnp.where(mask, s, NEG)
    p = jax.nn.softmax(s, axis=-1)
    return jnp.einsum('bhqk,bkhd->bqhd', p.astype(v.dtype), v)


def diff_attn_prompt(q, k, v, table, lam):
    B, S = q.shape[:2]
    kpos = jnp.arange(S)

    def block(bi):
        q0 = bi * Q_BLOCK
        qb = lax.dynamic_slice_in_dim(q, q0, Q_BLOCK, axis=1)
        qpos = q0 + jnp.arange(Q_BLOCK)
        mask = (kpos[None, :] // CHUNK) <= (qpos[:, None] // CHUNK)
        bias = t5_bias(table, kpos[None, :] - qpos[:, None])
        return diff_core(qb, k, v, bias, mask, lam)

    out = lax.map(block, jnp.arange(S // Q_BLOCK))
    return jnp.moveaxis(out, 0, 1).reshape(B, S, N_HEADS_A, 2 * HEAD_DIM)


def band_attn_prompt(q, k, v, table):
    B, S = q.shape[:2]
    pad = ((0, 0), (WINDOW_B, 0), (0, 0), (0, 0))
    kp = jnp.pad(k, pad)
    vp = jnp.pad(v, pad)
    j = jnp.arange(BAND_LEN)
    i = jnp.arange(CHUNK)
    bias = clip_bias(table, (j[None, :] - WINDOW_B) - i[:, None])

    def chunk(c):
        c0 = c * CHUNK
        qc = lax.dynamic_slice_in_dim(q, c0, CHUNK, axis=1)
        kc = lax.dynamic_slice_in_dim(kp, c0, BAND_LEN, axis=1)
        vc = lax.dynamic_slice_in_dim(vp, c0, BAND_LEN, axis=1)
        valid = (c0 - WINDOW_B + j) >= 0
        mask = jnp.broadcast_to(valid[None, :], (CHUNK, BAND_LEN))
        return band_core(qc, kc, vc, bias, mask)

    out = lax.map(chunk, jnp.arange(S // CHUNK))
    return jnp.moveaxis(out, 0, 1).reshape(B, S, N_HEADS_B, HEAD_DIM)


def diff_attn_sample(q, k_all, v_all, table, lam, past):
    Tn = q.shape[1]
    kpos = jnp.arange(k_all.shape[1])
    qpos = past + jnp.arange(Tn)
    mask = (kpos[None, :] // CHUNK) <= (qpos[:, None] // CHUNK)
    bias = t5_bias(table, kpos[None, :] - qpos[:, None])
    return diff_core(q, k_all, v_all, bias, mask, lam)


def band_attn_sample(q, k_all, v_all, table, past, cache_len):
    Tn = q.shape[1]
    kpos = jnp.concatenate([past - cache_len + jnp.arange(cache_len), past + jnp.arange(Tn)])
    qpos = past + jnp.arange(Tn)
    kc = kpos[None, :] // CHUNK
    qc = qpos[:, None] // CHUNK
    mask = (kc <= qc) & (kc >= qc - BAND_CHUNKS) & (kpos[None, :] >= 0)
    bias = clip_bias(table, kpos[None, :] - qpos[:, None])
    return band_core(q, k_all, v_all, bias, mask)


def merge(oa, ob, subln_g, out_norm_b, w_out, lam_init):
    B, T = oa.shape[:2]
    oa = rmsnorm(oa, subln_g, SUBLN_EPS) * (1.0 - lam_init)
    ob = rmsnorm(ob.reshape(B, T, WIDTH_B), out_norm_b)
    return jnp.concatenate([oa.reshape(B, T, WIDTH_A), ob], axis=-1) @ w_out


def ffn_ple(h, p, g_mlp, w_up, w_down, w_ple_gate, w_ple_proj):
    c = rmsnorm(h, g_mlp)
    h = h + jnp.square(jax.nn.relu(c @ w_up)) @ w_down
    return h + jax.nn.sigmoid(h @ w_ple_gate) * (p @ w_ple_proj)


def setup_inputs(seed: int = 0) -> dict:
    key = jax.random.key(seed)
    ks = jax.random.split(key, 32)
    f32 = jnp.float32
    nrm = lambda k, shape, s: jax.random.normal(k, shape, f32) * s
    cache_b_len = min(WINDOW_B, PAST_LEN)
    return {
        "x_prompt": nrm(ks[0], (BATCH, SEQ, D_MODEL), 1.0),
        "x_sample": nrm(ks[1], (DEC_BATCH, DEC_SEQ, D_MODEL), 1.0),
        "cache_a_k": nrm(ks[2], (DEPTH, DEC_BATCH, PAST_LEN, N_HEADS_A, 2, HEAD_DIM), 1.0),
        "cache_a_v": nrm(ks[3], (DEPTH, DEC_BATCH, PAST_LEN, N_HEADS_A, 2 * HEAD_DIM), 1.0),
        "cache_b_k": nrm(ks[4], (DEPTH, DEC_BATCH, cache_b_len, N_HEADS_B, HEAD_DIM), 1.0),
        "cache_b_v": nrm(ks[5], (DEPTH, DEC_BATCH, cache_b_len, N_HEADS_B, HEAD_DIM), 1.0),
        "p_prompt": nrm(ks[6], (DEPTH, BATCH, SEQ, D_PLE), 1.0),
        "p_sample": nrm(ks[7], (DEPTH, DEC_BATCH, DEC_SEQ, D_PLE), 1.0),
        "t5_table": nrm(ks[8], (N_BUCKETS, N_HEADS_A), 0.5),
        "g_attn": 1.0 + nrm(ks[9], (DEPTH, D_MODEL), 0.02),
        "w_in": nrm(ks[10], (DEPTH, D_MODEL, IN_COLS), D_MODEL ** -0.5),
        "lambda_q1": nrm(ks[11], (DEPTH, HEAD_DIM), 0.1),
        "lambda_k1": nrm(ks[12], (DEPTH, HEAD_DIM), 0.1),
        "lambda_q2": nrm(ks[13], (DEPTH, HEAD_DIM), 0.1),
        "lambda_k2": nrm(ks[14], (DEPTH, HEAD_DIM), 0.1),
        "subln_g": 1.0 + nrm(ks[15], (DEPTH, 2 * HEAD_DIM), 0.02),
        "band_table": nrm(ks[16], (DEPTH, 2 * REL_CLIP + 1, N_HEADS_B), 0.5),
        "out_norm_b": 1.0 + nrm(ks[17], (DEPTH, WIDTH_B), 0.02),
        "w_out": nrm(ks[18], (DEPTH, MIX_WIDTH, D_MODEL), MIX_WIDTH ** -0.5),
        "g_mlp": 1.0 + nrm(ks[19], (DEPTH, D_MODEL), 0.02),
        "w_up": nrm(ks[20], (DEPTH, D_MODEL, D_FF), D_MODEL ** -0.5),
        "w_down": nrm(ks[21], (DEPTH, D_FF, D_MODEL), D_FF ** -0.5),
        "w_ple_gate": nrm(ks[22], (DEPTH, D_MODEL, D_MODEL), D_MODEL ** -0.5),
        "w_ple_proj": nrm(ks[23], (DEPTH, D_PLE, D_MODEL), D_PLE ** -0.5),
        "g_final": 1.0 + nrm(ks[24], (D_MODEL,), 0.02),
    }


def reference(x_prompt, x_sample, cache_a_k, cache_a_v, cache_b_k, cache_b_v, p_prompt, p_sample,
              t5_table, g_attn, w_in, lambda_q1, lambda_k1, lambda_q2, lambda_k2, subln_g,
              band_table, out_norm_b, w_out, g_mlp, w_up, w_down, w_ple_gate, w_ple_proj, g_final):
    hp, hs = x_prompt, x_sample
    S = hp.shape[1]
    past = cache_a_k.shape[2]
    cache_b_len = cache_b_k.shape[2]
    keep = min(WINDOW_B, S)
    ak_p, av_p, bk_p, bv_p = [], [], [], []
    ak_s, av_s, bk_s, bv_s = [], [], [], []
    for i in range(DEPTH):
        lam_init = 0.8 - 0.6 * math.exp(-0.3 * i)
        lam = (jnp.exp(jnp.sum(lambda_q1[i].astype(jnp.float32) * lambda_k1[i].astype(jnp.float32)))
               - jnp.exp(jnp.sum(lambda_q2[i].astype(jnp.float32) * lambda_k2[i].astype(jnp.float32)))
               + lam_init)
        qa, ka, va, qb, kb, vb = split_proj(rmsnorm(hp, g_attn[i]), w_in[i])
        oa = diff_attn_prompt(qa, ka, va, t5_table, lam)
        ob = band_attn_prompt(qb, kb, vb, band_table[i])
        hp = hp + merge(oa, ob, subln_g[i], out_norm_b[i], w_out[i], lam_init)
        hp = ffn_ple(hp, p_prompt[i], g_mlp[i], w_up[i], w_down[i], w_ple_gate[i], w_ple_proj[i])
        ak_p.append(ka)
        av_p.append(va)
        bk_p.append(kb[:, S - keep:])
        bv_p.append(vb[:, S - keep:])
        qa, ka, va, qb, kb, vb = split_proj(rmsnorm(hs, g_attn[i]), w_in[i])
        ka_all = jnp.concatenate([cache_a_k[i], ka], axis=1)
        va_all = jnp.concatenate([cache_a_v[i], va], axis=1)
        kb_all = jnp.concatenate([cache_b_k[i], kb], axis=1)
        vb_all = jnp.concatenate([cache_b_v[i], vb], axis=1)
        oa = diff_attn_sample(qa, ka_all, va_all, t5_table, lam, past)
        ob = band_attn_sample(qb, kb_all, vb_all, band_table[i], past, cache_b_len)
        hs = hs + merge(oa, ob, subln_g[i], out_norm_b[i], w_out[i], lam_init)
        hs = ffn_ple(hs, p_sample[i], g_mlp[i], w_up[i], w_down[i], w_ple_gate[i], w_ple_proj[i])
        ak_s.append(ka)
        av_s.append(va)
        bk_s.append(kb)
        bv_s.append(vb)
    y_prompt = rmsnorm(hp, g_final)
    y_sample = rmsnorm(hs, g_final)
    return (y_prompt, y_sample,
            jnp.stack(ak_p), jnp.stack(av_p), jnp.stack(bk_p), jnp.stack(bv_p),
            jnp.stack(ak_s), jnp.stack(av_s), jnp.stack(bk_s), jnp.stack(bv_s))
```

```python
import functools
import math

import jax
import jax.numpy as jnp
from jax import lax
from jax.experimental import pallas as pl
from jax.experimental.pallas import tpu as pltpu

CHUNK = 64
HEAD_DIM = 64
N_HEADS_A = 4
N_HEADS_B = 8
WIDTH_A = N_HEADS_A * 2 * HEAD_DIM
WIDTH_B = N_HEADS_B * HEAD_DIM
N_BUCKETS = 32
MAX_DISTANCE = 128
BAND_CHUNKS = 8
WINDOW_B = BAND_CHUNKS * CHUNK
REL_CLIP = 128
EPS = 1e-6
SUBLN_EPS = 1e-5
MASKED = -1e30

LANES = 128
ATTN_BLOCK = 256
ROW_TILE = 512
VMEM_LIMIT = 56 * 1024 * 1024

F32 = jnp.float32
BF16 = jnp.bfloat16


def _t5_bucket(rel):
    half = N_BUCKETS // 2
    n = -rel
    ret = jnp.where(n < 0, half, 0)
    n = jnp.abs(n)
    max_exact = half // 2
    nf = jnp.maximum(n, 1).astype(F32)
    large = max_exact + (jnp.log(nf / max_exact) / math.log(MAX_DISTANCE / max_exact)
                         * (half - max_exact)).astype(jnp.int32)
    large = jnp.minimum(large, half - 1)
    return ret + jnp.where(n < max_exact, n, large)


def _t5_bias_t(table, kpos, qpos):
    rel = kpos[:, None] - qpos[None, :]
    return jnp.transpose(table[_t5_bucket(rel)].astype(F32), (2, 0, 1))


def _clip_bias_t(table, kpos, qpos):
    rel = kpos[:, None] - qpos[None, :]
    idx = jnp.clip(rel, -REL_CLIP, REL_CLIP) + REL_CLIP
    return jnp.transpose(table[idx].astype(F32), (2, 0, 1))


def _two_streams(bias):
    return jnp.concatenate([bias, bias], axis=-1)


def _head_pairs(bias):
    return jnp.concatenate([bias[0::2], bias[1::2]], axis=-1)


def _diff_prompt_bias(t5_table):
    blk = ATTN_BLOCK
    q = jnp.arange(blk)
    far = _t5_bias_t(t5_table, jnp.array([-2 * MAX_DISTANCE]), jnp.array([0]))
    tiles = []
    for j in range(2):
        k = jnp.arange(blk) - j * blk
        b = _t5_bias_t(t5_table, k, q) - far
        visible = (k[:, None] // CHUNK) <= (q[None, :] // CHUNK)
        tiles.append(jnp.where(visible[None], b, MASKED))
    return _two_streams(jnp.stack(tiles).reshape(2 * N_HEADS_A, blk, blk)).reshape(
        2, N_HEADS_A, blk, 2 * blk)


def _band_prompt_bias(band_table):
    blk = ATTN_BLOCK
    q = jnp.arange(blk)
    qc = q[None, :] // CHUNK
    tiles = []
    for j in range(3):
        k = jnp.arange(blk) - j * blk
        kc = k[:, None] // CHUNK
        visible = (kc <= qc) & (kc >= qc - BAND_CHUNKS)
        b = _clip_bias_t(band_table, k, q)
        tiles.append(_head_pairs(jnp.where(visible[None], b, MASKED)))
    return jnp.stack(tiles)


def _proj_kernel(x_ref, g_ref, w_ref, qa_ref, ka32_ref, ka16_ref, va32_ref, va16_ref,
                 qb_ref, kb16_ref, vb16_ref, kb32_ref, vb32_ref, *, tail_period):
    x = x_ref[...]
    xn = x * lax.rsqrt(jnp.mean(x * x, axis=-1, keepdims=True) + EPS) * g_ref[...]
    xn = xn.astype(BF16)

    def cols(c):
        return jnp.dot(xn, w_ref[:, c * WIDTH_A:(c + 1) * WIDTH_A], preferred_element_type=F32)

    qa_ref[...] = (cols(0) * (HEAD_DIM ** -0.5)).astype(BF16)
    ka = cols(1)
    ka32_ref[...] = ka
    ka16_ref[...] = ka.astype(BF16)
    va = cols(2)
    va32_ref[...] = va
    va16_ref[...] = va.astype(BF16)
    qb_ref[...] = (cols(3) * (HEAD_DIM ** -0.5)).astype(BF16)
    kb = cols(4)
    kb16_ref[...] = kb.astype(BF16)
    vb = cols(5)
    vb16_ref[...] = vb.astype(BF16)

    @pl.when(pl.program_id(0) % tail_period == tail_period - 1)
    def _():
        kb32_ref[...] = kb
        vb32_ref[...] = vb


def _project(x, g, w_bf, tail_period):
    n, d = x.shape
    tm = ROW_TILE
    assert n % (tm * tail_period) == 0
    steps = n // tm
    row = lambda i: (i, 0)
    tail = lambda i: (i // tail_period, 0)
    out_block = lambda imap: pl.BlockSpec((tm, WIDTH_A), imap)
    shapes = [(BF16, row), (F32, row), (BF16, row), (F32, row), (BF16, row),
              (BF16, row), (BF16, row), (BF16, row)]
    out_shape = [jax.ShapeDtypeStruct((n, WIDTH_A), dt) for dt, _ in shapes]
    out_specs = [out_block(m) for _, m in shapes]
    n_tail = n // tail_period
    out_shape += [jax.ShapeDtypeStruct((n_tail, WIDTH_B), F32)] * 2
    out_specs += [out_block(tail)] * 2
    return pl.pallas_call(
        functools.partial(_proj_kernel, tail_period=tail_period),
        grid=(steps,),
        in_specs=[pl.BlockSpec((tm, d), row),
                  pl.BlockSpec((1, d), lambda i: (0, 0)),
                  pl.BlockSpec(w_bf.shape, lambda i: (0, 0), pipeline_mode=pl.Buffered(1))],
        out_specs=out_specs,
        out_shape=out_shape,
        compiler_params=pltpu.CompilerParams(dimension_semantics=("arbitrary",),
                                             vmem_limit_bytes=VMEM_LIMIT),
        name="in_proj",
    )(x, g.reshape(1, d), w_bf)


def _attn_kernel(*refs, tq, tk, n_near, causal_walk, combine, lam_init):
    if combine == "diff":
        lam_ref, q_ref, k_ref, vt_ref, bias_ref, g_ref, o_ref, m_ref, l_ref, acc_ref = refs
    else:
        q_ref, k_ref, vt_ref, bias_ref, o_ref, m_ref, l_ref, acc_ref = refs
    qi = pl.program_id(2)

    q = q_ref[...]
    lane = lax.broadcasted_iota(jnp.int32, q.shape, 1)
    zero = jnp.zeros_like(q)
    qz = jnp.concatenate([jnp.where(lane < HEAD_DIM, q, zero),
                          jnp.where(lane >= HEAD_DIM, q, zero)], axis=0)

    m_ref[...] = jnp.full(m_ref.shape, MASKED, F32)
    l_ref[...] = jnp.zeros(l_ref.shape, F32)
    acc_ref[...] = jnp.zeros(acc_ref.shape, F32)

    def tile(kb, bias):
        start = kb * tk if isinstance(kb, int) else pl.multiple_of(kb * tk, tk)
        k = k_ref[pl.ds(start, tk), :]
        s = lax.dot_general(k, qz, (((1,), (1,)), ((), ())),
                            preferred_element_type=F32)
        if bias is not None:
            s = s + bias
        m_old = m_ref[...]
        m_new = jnp.maximum(m_old, jnp.max(s, axis=0, keepdims=True))
        alpha = jnp.exp(m_old - m_new)
        p = jnp.exp(s - m_new)
        l_ref[...] = alpha * l_ref[...] + jnp.sum(p, axis=0, keepdims=True)
        acc_ref[...] = alpha * acc_ref[...] + jnp.dot(
            vt_ref[kb], p.astype(BF16), preferred_element_type=F32)
        m_ref[...] = m_new

    if causal_walk:
        tile(qi, bias_ref[0])
        for j in range(1, n_near):
            @pl.when(qi >= j)
            def _(j=j):
                tile(qi - j, bias_ref[j])
        if combine == "diff":
            def far(kb, carry):
                tile(kb, None)
                return carry
            lax.fori_loop(0, jnp.maximum(qi - (n_near - 1), 0), far, 0)
    else:
        for j in range(n_near):
            tile(j, bias_ref[j])

    o = acc_ref[...] / l_ref[...]
    if combine == "diff":
        d = o[:, :tq] - lam_ref[0] * o[:, tq:]
        y = d * lax.rsqrt(jnp.mean(d * d, axis=0, keepdims=True) + SUBLN_EPS)
        o_ref[...] = ((y.T * g_ref[...]) * (1.0 - lam_init)).astype(o_ref.dtype)
    else:
        row = lax.broadcasted_iota(jnp.int32, (LANES, tq), 0)
        o_ref[...] = jnp.where(row < HEAD_DIM, o[:, :tq], o[:, tq:]).T.astype(o_ref.dtype)


def _attention(q, k, vt, bias, *, tq, tk, causal_walk, combine, out_dtype,
               lam=None, g=None, lam_init=0.0):
    b, tq_total, width = q.shape
    tk_total = k.shape[1]
    groups = width // LANES
    n_near = bias.shape[0]
    nkb = vt.shape[2]
    assert tq_total % tq == 0 and tk_total == nkb * tk and vt.shape[3:] == (LANES, tk)
    in_specs = [
        pl.BlockSpec((None, tq, LANES), lambda bi, h, qi: (bi, qi, h)),
        pl.BlockSpec((None, tk_total, LANES), lambda bi, h, qi: (bi, 0, h)),
        pl.BlockSpec((None, None, nkb, LANES, tk), lambda bi, h, qi: (bi, h, 0, 0, 0)),
        pl.BlockSpec((n_near, None, tk, 2 * tq), lambda bi, h, qi: (0, h, 0, 0)),
    ]
    args = [q, k, vt, bias]
    if combine == "diff":
        in_specs = [pl.BlockSpec(memory_space=pltpu.SMEM)] + in_specs
        in_specs.append(pl.BlockSpec((1, LANES), lambda bi, h, qi: (0, 0)))
        args = [lam.reshape(1).astype(F32)] + args + [g.reshape(1, LANES).astype(F32)]
    return pl.pallas_call(
        functools.partial(_attn_kernel, tq=tq, tk=tk, n_near=n_near, causal_walk=causal_walk,
                          combine=combine, lam_init=lam_init),
        grid=(b, groups, tq_total // tq),
        in_specs=in_specs,
        out_specs=pl.BlockSpec((None, tq, LANES), lambda bi, h, qi: (bi, qi, h)),
        out_shape=jax.ShapeDtypeStruct((b, tq_total, width), out_dtype),
        scratch_shapes=[pltpu.VMEM((1, 2 * tq), F32), pltpu.VMEM((1, 2 * tq), F32),
                        pltpu.VMEM((LANES, 2 * tq), F32)],
        compiler_params=pltpu.CompilerParams(
            dimension_semantics=("arbitrary", "arbitrary", "arbitrary"),
            vmem_limit_bytes=VMEM_LIMIT),
        name="attn_" + combine + ("_walk" if causal_walk else "_flat"),
    )(*args)


def _transpose_values(v, tk):
    b, t, width = v.shape
    return v.reshape(b, t // tk, tk, width // LANES, LANES).transpose(0, 3, 1, 4, 2)


def _rms(x, g, eps):
    return x * lax.rsqrt(jnp.mean(x * x, axis=-1, keepdims=True) + eps) * g


def _post_kernel(x_ref, oa_ref, ob_ref, p_ref, gb_ref, gm_ref, gf_ref, wout_ref, wup_ref,
                 wdown_ref, wgate_ref, wple_ref, y_ref):
    obn = _rms(ob_ref[...], gb_ref[...], EPS).astype(BF16)
    h = x_ref[...] + (
        jnp.dot(oa_ref[...], wout_ref[:WIDTH_A, :], preferred_element_type=F32)
        + jnp.dot(obn, wout_ref[WIDTH_A:, :], preferred_element_type=F32))
    c = _rms(h, gm_ref[...], EPS).astype(BF16)
    d_ff = wup_ref.shape[1]
    step = wup_ref.shape[0]
    for j in range(d_ff // step):
        u = jnp.dot(c, wup_ref[:, j * step:(j + 1) * step], preferred_element_type=F32)
        u = jnp.square(jnp.maximum(u, 0.0)).astype(BF16)
        h = h + jnp.dot(u, wdown_ref[j * step:(j + 1) * step, :], preferred_element_type=F32)
    gate = jax.nn.sigmoid(jnp.dot(h.astype(BF16), wgate_ref[...], preferred_element_type=F32))
    h = h + gate * jnp.dot(p_ref[...].astype(BF16), wple_ref[...], preferred_element_type=F32)
    y_ref[...] = _rms(h, gf_ref[...], EPS)


def _post(x, oa, ob, p, gb, gm, gf, wout, wup, wdown, wgate, wple):
    n, d = x.shape
    tm = min(ROW_TILE, n)
    assert n % tm == 0
    row = lambda i: (i, 0)
    fixed = lambda a: pl.BlockSpec(a.shape, lambda i: (0,) * a.ndim, pipeline_mode=pl.Buffered(1))
    vec = lambda a: a.reshape(1, -1).astype(F32)
    gb, gm, gf = vec(gb), vec(gm), vec(gf)
    return pl.pallas_call(
        _post_kernel,
        grid=(n // tm,),
        in_specs=[pl.BlockSpec((tm, d), row), pl.BlockSpec((tm, oa.shape[1]), row),
                  pl.BlockSpec((tm, ob.shape[1]), row), pl.BlockSpec((tm, p.shape[1]), row),
                  fixed(gb), fixed(gm), fixed(gf), fixed(wout), fixed(wup), fixed(wdown),
                  fixed(wgate), fixed(wple)],
        out_specs=pl.BlockSpec((tm, d), row),
        out_shape=jax.ShapeDtypeStruct((n, d), F32),
        compiler_params=pltpu.CompilerParams(dimension_semantics=("arbitrary",),
                                             vmem_limit_bytes=VMEM_LIMIT),
        name="post",
    )(x, oa, ob, p, gb, gm, gf, wout, wup, wdown, wgate, wple)


def _pad_rows(a, rows):
    return jnp.pad(a, ((0, 0), (0, rows - a.shape[1]), (0, 0)))


def _sample_bias(bias_fn, pairing, n_keys, n_new, tk, tq, first_kpos, past, band):
    kk = jnp.arange(tk)
    kpos = first_kpos + kk
    qpos = past + jnp.arange(tq)
    kc = kpos[:, None] // CHUNK
    qc = qpos[None, :] // CHUNK
    visible = (kc <= qc) & (kk[:, None] < n_keys)
    if band:
        visible = visible & (kc >= qc - BAND_CHUNKS) & (kpos[:, None] >= 0)
    del n_new
    return pairing(jnp.where(visible[None], bias_fn(kpos, qpos), MASKED))[None]


def kernel(x_prompt, x_sample, cache_a_k, cache_a_v, cache_b_k, cache_b_v, p_prompt, p_sample,
           t5_table, g_attn, w_in, lambda_q1, lambda_k1, lambda_q2, lambda_k2, subln_g,
           band_table, out_norm_b, w_out, g_mlp, w_up, w_down, w_ple_gate, w_ple_proj, g_final):
    depth = w_in.shape[0]
    assert depth == 1
    i = 0
    B, S, D = x_prompt.shape
    Bs, Ts, _ = x_sample.shape
    past = cache_a_k.shape[2]
    cache_b_len = cache_b_k.shape[2]
    keep = min(WINDOW_B, S)
    assert keep == ROW_TILE and S % ATTN_BLOCK == 0 and Bs * Ts == ROW_TILE

    lam_init = 0.8 - 0.6 * math.exp(-0.3 * i)
    lam = (jnp.exp(jnp.sum(lambda_q1[i].astype(F32) * lambda_k1[i].astype(F32)))
           - jnp.exp(jnp.sum(lambda_q2[i].astype(F32) * lambda_k2[i].astype(F32)))
           + lam_init)

    w_in_bf = w_in[i].astype(BF16)
    post_w = (out_norm_b[i], g_mlp[i], g_final, w_out[i].astype(BF16), w_up[i].astype(BF16),
              w_down[i].astype(BF16), w_ple_gate[i].astype(BF16), w_ple_proj[i].astype(BF16))
    blk = ATTN_BLOCK

    xp = x_prompt.reshape(B * S, D)
    (qa, ka32, ka16, va32, va16, qb, kb16, vb16, kb32, vb32) = _project(
        xp, g_attn[i], w_in_bf, tail_period=S // ROW_TILE)
    to3 = lambda a: a.reshape(B, S, a.shape[-1])
    oa = _attention(to3(qa), to3(ka16), _transpose_values(to3(va16), blk),
                    _diff_prompt_bias(t5_table), tq=blk, tk=blk, causal_walk=True,
                    combine="diff", out_dtype=BF16, lam=lam, g=subln_g[i], lam_init=lam_init)
    ob = _attention(to3(qb), to3(kb16), _transpose_values(to3(vb16), blk),
                    _band_prompt_bias(band_table[i]), tq=blk, tk=blk, causal_walk=True,
                    combine="band", out_dtype=F32)
    y_prompt = _post(xp, oa.reshape(B * S, WIDTH_A), ob.reshape(B * S, WIDTH_B),
                     p_prompt[i].reshape(B * S, -1), *post_w).reshape(B, S, D)

    xs = x_sample.reshape(Bs * Ts, D)
    (qa_s, ka32_s, ka16_s, va32_s, va16_s, qb_s, kb16_s, vb16_s, kb32_s, vb32_s) = _project(
        xs, g_attn[i], w_in_bf, tail_period=1)
    to3s = lambda a: a.reshape(Bs, Ts, a.shape[-1])
    tq_s = LANES
    assert Ts <= tq_s

    def sample_keys(cache, new16):
        c = cache.reshape(Bs, cache.shape[1], -1).astype(BF16)
        allk = jnp.concatenate([c, to3s(new16)], axis=1)
        n_keys = allk.shape[1]
        tk = -(-n_keys // LANES) * LANES
        return _pad_rows(allk, tk), n_keys, tk

    ka_all, n_a, tk_a = sample_keys(cache_a_k[i], ka16_s)
    va_all, _, _ = sample_keys(cache_a_v[i], va16_s)
    kb_all, n_b, tk_b = sample_keys(cache_b_k[i], kb16_s)
    vb_all, _, _ = sample_keys(cache_b_v[i], vb16_s)

    bias_a = _sample_bias(
        lambda kp, qp: _t5_bias_t(t5_table, kp, qp), _two_streams,
        n_a, Ts, tk_a, tq_s, 0, past, band=False)
    bias_b = _sample_bias(
        lambda kp, qp: _clip_bias_t(band_table[i], kp, qp), _head_pairs,
        n_b, Ts, tk_b, tq_s, past - cache_b_len, past, band=True)

    oa_s = _attention(_pad_rows(to3s(qa_s), tq_s), ka_all, _transpose_values(va_all, tk_a),
                      bias_a, tq=tq_s, tk=tk_a, causal_walk=False, combine="diff",
                      out_dtype=BF16, lam=lam, g=subln_g[i], lam_init=lam_init)[:, :Ts]
    ob_s = _attention(_pad_rows(to3s(qb_s), tq_s), kb_all, _transpose_values(vb_all, tk_b),
                      bias_b, tq=tq_s, tk=tk_b, causal_walk=False, combine="band",
                      out_dtype=F32)[:, :Ts]
    y_sample = _post(xs, oa_s.reshape(Bs * Ts, WIDTH_A), ob_s.reshape(Bs * Ts, WIDTH_B),
                     p_sample[i].reshape(Bs * Ts, -1), *post_w).reshape(Bs, Ts, D)

    return (y_prompt, y_sample,
            ka32.reshape(1, B, S, N_HEADS_A, 2, HEAD_DIM),
            va32.reshape(1, B, S, N_HEADS_A, 2 * HEAD_DIM),
            kb32.reshape(1, B, keep, N_HEADS_B, HEAD_DIM),
            vb32.reshape(1, B, keep, N_HEADS_B, HEAD_DIM),
            ka32_s.reshape(1, Bs, Ts, N_HEADS_A, 2, HEAD_DIM),
            va32_s.reshape(1, Bs, Ts, N_HEADS_A, 2 * HEAD_DIM),
            kb32_s.reshape(1, Bs, Ts, N_HEADS_B, HEAD_DIM),
            vb32_s.reshape(1, Bs, Ts, N_HEADS_B, HEAD_DIM))
```

```python
import functools
import math

import jax
import jax.numpy as jnp
from jax import lax
from jax.experimental import pallas as pl
from jax.experimental.pallas import tpu as pltpu

CHUNK = 64
HEAD_DIM = 64
N_HEADS_A = 4
N_HEADS_B = 8
WIDTH_A = N_HEADS_A * 2 * HEAD_DIM
WIDTH_B = N_HEADS_B * HEAD_DIM
N_BUCKETS = 32
MAX_DISTANCE = 128
BAND_CHUNKS = 8
WINDOW_B = BAND_CHUNKS * CHUNK
REL_CLIP = 128
EPS = 1e-6
SUBLN_EPS = 1e-5
MASKED = -1e30

LANES = 128
ATTN_BLOCK = 256
FAR_UNROLL = 4
ROW_TILE = 512
VMEM_LIMIT = 56 * 1024 * 1024

F32 = jnp.float32
BF16 = jnp.bfloat16


def _t5_bucket(rel):
    half = N_BUCKETS // 2
    n = -rel
    ret = jnp.where(n < 0, half, 0)
    n = jnp.abs(n)
    max_exact = half // 2
    nf = jnp.maximum(n, 1).astype(F32)
    large = max_exact + (jnp.log(nf / max_exact) / math.log(MAX_DISTANCE / max_exact)
                         * (half - max_exact)).astype(jnp.int32)
    large = jnp.minimum(large, half - 1)
    return ret + jnp.where(n < max_exact, n, large)


def _t5_vec(table):
    return lambda rel: table[_t5_bucket(rel)]


def _clip_vec(table):
    return lambda rel: table[jnp.clip(rel, -REL_CLIP, REL_CLIP) + REL_CLIP]


def _toeplitz(vec_fn, tk, tq, off):
    n_diag = tk + tq - 1
    rel = jnp.arange(n_diag) + (off - (tq - 1))
    u = jnp.transpose(vec_fn(rel).astype(F32))
    heads = u.shape[0]
    flat = jnp.tile(u, (1, tk + 1))[:, :tk * (n_diag + 1)]
    hankel = flat.reshape(heads, tk, n_diag + 1)[:, :, :tq]
    return hankel[:, :, ::-1]


def _two_streams(bias):
    return jnp.concatenate([bias, bias], axis=-1)


def _head_pairs(bias):
    return jnp.concatenate([bias[0::2], bias[1::2]], axis=-1)


def _diff_prompt_bias(t5_table):
    blk = ATTN_BLOCK
    q = jnp.arange(blk)
    far = _t5_vec(t5_table)(jnp.array([-2 * MAX_DISTANCE])).astype(F32).reshape(-1, 1, 1)
    tiles = []
    for j in range(2):
        k = jnp.arange(blk) - j * blk
        b = _toeplitz(_t5_vec(t5_table), blk, blk, -j * blk) - far
        visible = (k[:, None] // CHUNK) <= (q[None, :] // CHUNK)
        tiles.append(_two_streams(jnp.where(visible[None], b, MASKED)))
    return jnp.stack(tiles)


def _band_prompt_bias(band_table):
    blk = ATTN_BLOCK
    q = jnp.arange(blk)
    qc = q[None, :] // CHUNK
    tiles = []
    for j in range(3):
        k = jnp.arange(blk) - j * blk
        kc = k[:, None] // CHUNK
        visible = (kc <= qc) & (kc >= qc - BAND_CHUNKS)
        b = _toeplitz(_clip_vec(band_table), blk, blk, -j * blk)
        tiles.append(_head_pairs(jnp.where(visible[None], b, MASKED)))
    return jnp.stack(tiles)


def _store_transposed(vt_ref, v):
    blk = vt_ref.shape[-1]
    for h in range(vt_ref.shape[0]):
        for c in range(vt_ref.shape[1]):
            vt_ref[h, c] = v[c * blk:(c + 1) * blk, h * LANES:(h + 1) * LANES].T.astype(BF16)


def _proj_kernel(x_ref, g_ref, w_ref, qa_ref, ka32_ref, ka16_ref, va32_ref, vta_ref,
                 qb_ref, kb16_ref, vtb_ref, kb32_ref, vb32_ref, *, tail_period):
    x = x_ref[...]
    xn = x * lax.rsqrt(jnp.mean(x * x, axis=-1, keepdims=True) + EPS) * g_ref[...]
    xn = xn.astype(BF16)

    def cols(c):
        return jnp.dot(xn, w_ref[:, c * WIDTH_A:(c + 1) * WIDTH_A], preferred_element_type=F32)

    qa_ref[...] = (cols(0) * (HEAD_DIM ** -0.5)).astype(BF16)
    ka = cols(1)
    ka32_ref[...] = ka
    ka16_ref[...] = ka.astype(BF16)
    va = cols(2)
    va32_ref[...] = va
    _store_transposed(vta_ref, va)
    qb_ref[...] = (cols(3) * (HEAD_DIM ** -0.5)).astype(BF16)
    kb = cols(4)
    kb16_ref[...] = kb.astype(BF16)
    vb = cols(5)
    _store_transposed(vtb_ref, vb)

    @pl.when(pl.program_id(0) % tail_period == tail_period - 1)
    def _():
        kb32_ref[...] = kb
        vb32_ref[...] = vb


def _project(x, g, w_bf, tail_period):
    n, d = x.shape
    tm = ROW_TILE
    blk = ATTN_BLOCK
    assert n % (tm * tail_period) == 0 and tm % blk == 0
    groups = WIDTH_A // LANES
    row = lambda i: (i, 0)
    tail = lambda i: (i // tail_period, 0)
    rows = lambda dt, imap=row: (jax.ShapeDtypeStruct((n, WIDTH_A), dt),
                                 pl.BlockSpec((tm, WIDTH_A), imap))
    vt = (jax.ShapeDtypeStruct((groups, n // blk, LANES, blk), BF16),
          pl.BlockSpec((groups, tm // blk, LANES, blk), lambda i: (0, i, 0, 0)))
    n_tail = n // tail_period
    tails = (jax.ShapeDtypeStruct((n_tail, WIDTH_B), F32), pl.BlockSpec((tm, WIDTH_B), tail))
    outs = [rows(BF16), rows(F32), rows(BF16), rows(F32), vt,
            rows(BF16), rows(BF16), vt, tails, tails]
    return pl.pallas_call(
        functools.partial(_proj_kernel, tail_period=tail_period),
        grid=(n // tm,),
        in_specs=[pl.BlockSpec((tm, d), row),
                  pl.BlockSpec((1, d), lambda i: (0, 0)),
                  pl.BlockSpec(w_bf.shape, lambda i: (0, 0), pipeline_mode=pl.Buffered(1))],
        out_specs=[s for _, s in outs],
        out_shape=[s for s, _ in outs],
        compiler_params=pltpu.CompilerParams(dimension_semantics=("arbitrary",),
                                             vmem_limit_bytes=VMEM_LIMIT),
        name="in_proj",
    )(x, g.reshape(1, d), w_bf)


def _attn_kernel(*refs, tq, tk, n_near, causal_walk, combine, lam_init):
    far_walk = causal_walk and combine == "diff"
    if combine == "diff":
        lam_ref, q_ref, k_ref, vt_ref, bias_ref, g_ref, o_ref, m_ref, l_ref, acc_ref = refs[:10]
    else:
        q_ref, k_ref, vt_ref, bias_ref, o_ref, m_ref, l_ref, acc_ref = refs[:8]
    qi = pl.program_id(2)

    q = q_ref[...]
    lane = lax.broadcasted_iota(jnp.int32, q.shape, 1)
    zero = jnp.zeros_like(q)
    qz = jnp.concatenate([jnp.where(lane < HEAD_DIM, q, zero),
                          jnp.where(lane >= HEAD_DIM, q, zero)], axis=0)

    def scores(kb, bias):
        start = kb * tk if isinstance(kb, int) else pl.multiple_of(kb * tk, tk)
        k = k_ref[pl.ds(start, tk), :]
        s = lax.dot_general(k, qz, (((1,), (1,)), ((), ())),
                            preferred_element_type=F32)
        return s if bias is None else s + bias_ref[bias]

    def run_tiles(tiles):
        m = jnp.full(m_ref.shape, MASKED, F32)
        l = jnp.zeros(l_ref.shape, F32)
        acc = jnp.zeros(acc_ref.shape, F32)
        s_next = scores(*tiles[0])
        for t, (kb, _) in enumerate(tiles):
            s = s_next
            if t + 1 < len(tiles):
                s_next = scores(*tiles[t + 1])
            m_new = jnp.maximum(m, jnp.max(s, axis=0, keepdims=True))
            alpha = jnp.exp(m - m_new)
            p = jnp.exp(s - m_new)
            l = alpha * l + jnp.sum(p, axis=0, keepdims=True)
            acc = alpha * acc + jnp.dot(vt_ref[kb], p.astype(BF16),
                                        preferred_element_type=F32)
            m = m_new
        m_ref[...] = m
        l_ref[...] = l
        acc_ref[...] = acc

    def near_tiles(c):
        return [(qi - j, j) for j in range(c)]

    if far_walk:
        sbuf_ref = refs[10]
        group = FAR_UNROLL
        n_far = jnp.maximum(qi - (n_near - 1), 0)
        n_groups = n_far // group
        left = n_far - n_groups * group
        for c in range(1, n_near):
            @pl.when(qi == c - 1)
            def _(c=c):
                run_tiles(near_tiles(c))
        for v in range(group):
            @pl.when((qi >= n_near - 1) & (left == v))
            def _(v=v):
                run_tiles(near_tiles(n_near) + [(n_groups * group + j, None) for j in range(v)])

        sbufs = (sbuf_ref, refs[11])

        def group_step(g, cur, nxt):
            rows = lambda t: cur[t * tk:(t + 1) * tk, :]
            m = m_ref[...]
            m_new = m
            for t in range(group):
                m_new = jnp.maximum(m_new, jnp.max(rows(t), axis=0, keepdims=True))
            alpha = jnp.exp(m - m_new)
            l_sum = None
            pv = None
            for t in range(group):
                if nxt is not None:
                    nxt[1][t * tk:(t + 1) * tk, :] = scores(nxt[0] * group + t, None)
                p = jnp.exp(rows(t) - m_new)
                lt = jnp.sum(p, axis=0, keepdims=True)
                d = jnp.dot(vt_ref[g * group + t], p.astype(BF16), preferred_element_type=F32)
                l_sum = lt if l_sum is None else l_sum + lt
                pv = d if pv is None else pv + d
            l_ref[...] = alpha * l_ref[...] + l_sum
            acc_ref[...] = alpha * acc_ref[...] + pv
            m_ref[...] = m_new

        @pl.when(n_groups > 0)
        def _():
            for t in range(group):
                sbufs[0][t * tk:(t + 1) * tk, :] = scores(t, None)
            last = n_groups - 1

            def pair(gp, carry):
                group_step(2 * gp, sbufs[0], (2 * gp + 1, sbufs[1]))
                group_step(2 * gp + 1, sbufs[1], (jnp.minimum(2 * gp + 2, last), sbufs[0]))
                return carry

            lax.fori_loop(0, n_groups // 2, pair, 0)

            @pl.when(n_groups % 2 == 1)
            def _():
                group_step(last, sbufs[0], None)
    elif causal_walk:
        for c in range(1, n_near):
            @pl.when(qi == c - 1)
            def _(c=c):
                run_tiles(near_tiles(c))

        @pl.when(qi >= n_near - 1)
        def _():
            run_tiles(near_tiles(n_near))
    else:
        run_tiles([(j, j) for j in range(n_near)])

    o = acc_ref[...] / l_ref[...]
    if combine == "diff":
        d = o[:, :tq] - lam_ref[0] * o[:, tq:]
        y = d * lax.rsqrt(jnp.mean(d * d, axis=0, keepdims=True) + SUBLN_EPS)
        o_ref[...] = ((y.T * g_ref[...]) * (1.0 - lam_init)).astype(o_ref.dtype)
    else:
        row = lax.broadcasted_iota(jnp.int32, (LANES, tq), 0)
        o_ref[...] = jnp.where(row < HEAD_DIM, o[:, :tq], o[:, tq:]).T.astype(o_ref.dtype)


def _attention(q, k, vt, bias, *, tq, tk, causal_walk, combine, out_dtype,
               lam=None, g=None, lam_init=0.0):
    b, tq_total, width = q.shape
    tk_total = k.shape[1]
    groups = width // LANES
    n_near = bias.shape[0]
    nkb = tk_total // tk
    assert tq_total % tq == 0 and vt.shape == (groups, b * nkb, LANES, tk)
    in_specs = [
        pl.BlockSpec((None, tq, LANES), lambda bi, h, qi: (bi, qi, h)),
        pl.BlockSpec((None, tk_total, LANES), lambda bi, h, qi: (bi, 0, h)),
        pl.BlockSpec((None, nkb, LANES, tk), lambda bi, h, qi: (h, bi, 0, 0)),
        pl.BlockSpec((n_near, None, tk, 2 * tq), lambda bi, h, qi: (0, h, 0, 0)),
    ]
    args = [q, k, vt, bias]
    if combine == "diff":
        in_specs = [pl.BlockSpec(memory_space=pltpu.SMEM)] + in_specs
        in_specs.append(pl.BlockSpec((1, LANES), lambda bi, h, qi: (0, 0)))
        args = [lam.reshape(1).astype(F32)] + args + [g.reshape(1, LANES).astype(F32)]
    scratch = [pltpu.VMEM((1, 2 * tq), F32), pltpu.VMEM((1, 2 * tq), F32),
               pltpu.VMEM((LANES, 2 * tq), F32)]
    if causal_walk and combine == "diff":
        scratch += [pltpu.VMEM((FAR_UNROLL * tk, 2 * tq), F32)] * 2
    return pl.pallas_call(
        functools.partial(_attn_kernel, tq=tq, tk=tk, n_near=n_near, causal_walk=causal_walk,
                          combine=combine, lam_init=lam_init),
        grid=(b, groups, tq_total // tq),
        in_specs=in_specs,
        out_specs=pl.BlockSpec((None, tq, LANES), lambda bi, h, qi: (bi, qi, h)),
        out_shape=jax.ShapeDtypeStruct((b, tq_total, width), out_dtype),
        scratch_shapes=scratch,
        compiler_params=pltpu.CompilerParams(
            dimension_semantics=("arbitrary", "arbitrary", "arbitrary"),
            vmem_limit_bytes=VMEM_LIMIT),
        name="attn_" + combine + ("_walk" if causal_walk else "_flat"),
    )(*args)


def _transpose_values(v, tk):
    b, t, width = v.shape
    return v.reshape(b * (t // tk), tk, width // LANES, LANES).transpose(2, 0, 3, 1)


def _rms(x, g, eps):
    return x * lax.rsqrt(jnp.mean(x * x, axis=-1, keepdims=True) + eps) * g


def _post_kernel(x_ref, oa_ref, ob_ref, p_ref, gb_ref, gm_ref, gf_ref, wout_ref, wup_ref,
                 wdown_ref, wgate_ref, wple_ref, y_ref):
    obn = _rms(ob_ref[...], gb_ref[...], EPS).astype(BF16)
    h = x_ref[...] + (
        jnp.dot(oa_ref[...], wout_ref[:WIDTH_A, :], preferred_element_type=F32)
        + jnp.dot(obn, wout_ref[WIDTH_A:, :], preferred_element_type=F32))
    c = _rms(h, gm_ref[...], EPS).astype(BF16)
    d_ff = wup_ref.shape[1]
    step = wup_ref.shape[0]
    for j in range(d_ff // step):
        u = jnp.dot(c, wup_ref[:, j * step:(j + 1) * step], preferred_element_type=F32)
        u = jnp.square(jnp.maximum(u, 0.0)).astype(BF16)
        h = h + jnp.dot(u, wdown_ref[j * step:(j + 1) * step, :], preferred_element_type=F32)
    gate = jax.nn.sigmoid(jnp.dot(h.astype(BF16), wgate_ref[...], preferred_element_type=F32))
    h = h + gate * jnp.dot(p_ref[...].astype(BF16), wple_ref[...], preferred_element_type=F32)
    y_ref[...] = _rms(h, gf_ref[...], EPS)


def _post(x, oa, ob, p, gb, gm, gf, wout, wup, wdown, wgate, wple):
    n, d = x.shape
    tm = min(ROW_TILE, n)
    assert n % tm == 0
    row = lambda i: (i, 0)
    fixed = lambda a: pl.BlockSpec(a.shape, lambda i: (0,) * a.ndim, pipeline_mode=pl.Buffered(1))
    vec = lambda a: a.reshape(1, -1).astype(F32)
    gb, gm, gf = vec(gb), vec(gm), vec(gf)
    return pl.pallas_call(
        _post_kernel,
        grid=(n // tm,),
        in_specs=[pl.BlockSpec((tm, d), row), pl.BlockSpec((tm, oa.shape[1]), row),
                  pl.BlockSpec((tm, ob.shape[1]), row), pl.BlockSpec((tm, p.shape[1]), row),
                  fixed(gb), fixed(gm), fixed(gf), fixed(wout), fixed(wup), fixed(wdown),
                  fixed(wgate), fixed(wple)],
        out_specs=pl.BlockSpec((tm, d), row),
        out_shape=jax.ShapeDtypeStruct((n, d), F32),
        compiler_params=pltpu.CompilerParams(dimension_semantics=("arbitrary",),
                                             vmem_limit_bytes=VMEM_LIMIT),
        name="post",
    )(x, oa, ob, p, gb, gm, gf, wout, wup, wdown, wgate, wple)


def _pad_rows(a, rows):
    return jnp.pad(a, ((0, 0), (0, rows - a.shape[1]), (0, 0)))


def _sample_bias(vec_fn, pairing, n_keys, tk, tq, first_kpos, past, band):
    kk = jnp.arange(tk)
    kpos = first_kpos + kk
    qpos = past + jnp.arange(tq)
    kc = kpos[:, None] // CHUNK
    qc = qpos[None, :] // CHUNK
    visible = (kc <= qc) & (kk[:, None] < n_keys)
    if band:
        visible = visible & (kc >= qc - BAND_CHUNKS) & (kpos[:, None] >= 0)
    b = _toeplitz(vec_fn, tk, tq, first_kpos - past)
    return pairing(jnp.where(visible[None], b, MASKED))[None]


def kernel(x_prompt, x_sample, cache_a_k, cache_a_v, cache_b_k, cache_b_v, p_prompt, p_sample,
           t5_table, g_attn, w_in, lambda_q1, lambda_k1, lambda_q2, lambda_k2, subln_g,
           band_table, out_norm_b, w_out, g_mlp, w_up, w_down, w_ple_gate, w_ple_proj, g_final):
    depth = w_in.shape[0]
    assert depth == 1
    i = 0
    B, S, D = x_prompt.shape
    Bs, Ts, _ = x_sample.shape
    past = cache_a_k.shape[2]
    cache_b_len = cache_b_k.shape[2]
    keep = min(WINDOW_B, S)
    assert keep == ROW_TILE and S % ATTN_BLOCK == 0 and Bs * Ts == ROW_TILE

    lam_init = 0.8 - 0.6 * math.exp(-0.3 * i)
    lam = (jnp.exp(jnp.sum(lambda_q1[i].astype(F32) * lambda_k1[i].astype(F32)))
           - jnp.exp(jnp.sum(lambda_q2[i].astype(F32) * lambda_k2[i].astype(F32)))
           + lam_init)

    w_in_bf = w_in[i].astype(BF16)
    post_w = (out_norm_b[i], g_mlp[i], g_final, w_out[i].astype(BF16), w_up[i].astype(BF16),
              w_down[i].astype(BF16), w_ple_gate[i].astype(BF16), w_ple_proj[i].astype(BF16))
    blk = ATTN_BLOCK

    xp = x_prompt.reshape(B * S, D)
    (qa, ka32, ka16, va32, vta, qb, kb16, vtb, kb32, vb32) = _project(
        xp, g_attn[i], w_in_bf, tail_period=S // ROW_TILE)
    to3 = lambda a: a.reshape(B, S, a.shape[-1])
    oa = _attention(to3(qa), to3(ka16), vta, _diff_prompt_bias(t5_table), tq=blk, tk=blk,
                    causal_walk=True, combine="diff", out_dtype=BF16, lam=lam, g=subln_g[i],
                    lam_init=lam_init)
    ob = _attention(to3(qb), to3(kb16), vtb, _band_prompt_bias(band_table[i]), tq=blk, tk=blk,
                    causal_walk=True, combine="band", out_dtype=F32)
    y_prompt = _post(xp, oa.reshape(B * S, WIDTH_A), ob.reshape(B * S, WIDTH_B),
                     p_prompt[i].reshape(B * S, -1), *post_w).reshape(B, S, D)

    xs = x_sample.reshape(Bs * Ts, D)
    (qa_s, ka32_s, ka16_s, va32_s, _, qb_s, kb16_s, _, kb32_s, vb32_s) = _project(
        xs, g_attn[i], w_in_bf, tail_period=1)
    to3s = lambda a: a.reshape(Bs, Ts, a.shape[-1])
    tq_s = LANES
    assert Ts <= tq_s

    def sample_keys(cache, new):
        c = cache.reshape(Bs, cache.shape[1], -1).astype(BF16)
        allk = jnp.concatenate([c, to3s(new).astype(BF16)], axis=1)
        n_keys = allk.shape[1]
        tk = -(-n_keys // LANES) * LANES
        return _pad_rows(allk, tk), n_keys, tk

    ka_all, n_a, tk_a = sample_keys(cache_a_k[i], ka16_s)
    va_all, _, _ = sample_keys(cache_a_v[i], va32_s)
    kb_all, n_b, tk_b = sample_keys(cache_b_k[i], kb16_s)
    vb_all, _, _ = sample_keys(cache_b_v[i], vb32_s)

    bias_a = _sample_bias(_t5_vec(t5_table), _two_streams, n_a, tk_a, tq_s, 0, past, band=False)
    bias_b = _sample_bias(_clip_vec(band_table[i]), _head_pairs, n_b, tk_b, tq_s,
                          past - cache_b_len, past, band=True)

    oa_s = _attention(_pad_rows(to3s(qa_s), tq_s), ka_all, _transpose_values(va_all, tk_a),
                      bias_a, tq=tq_s, tk=tk_a, causal_walk=False, combine="diff",
                      out_dtype=BF16, lam=lam, g=subln_g[i], lam_init=lam_init)[:, :Ts]
    ob_s = _attention(_pad_rows(to3s(qb_s), tq_s), kb_all, _transpose_values(vb_all, tk_b),
                      bias_b, tq=tq_s, tk=tk_b, causal_walk=False, combine="band",
                      out_dtype=F32)[:, :Ts]
    y_sample = _post(xs, oa_s.reshape(Bs * Ts, WIDTH_A), ob_s.reshape(Bs * Ts, WIDTH_B),
                     p_sample[i].reshape(Bs * Ts, -1), *post_w).reshape(Bs, Ts, D)

    return (y_prompt, y_sample,
            ka32.reshape(1, B, S, N_HEADS_A, 2, HEAD_DIM),
            va32.reshape(1, B, S, N_HEADS_A, 2 * HEAD_DIM),
            kb32.reshape(1, B, keep, N_HEADS_B, HEAD_DIM),
            vb32.reshape(1, B, keep, N_HEADS_B, HEAD_DIM),
            ka32_s.reshape(1, Bs, Ts, N_HEADS_A, 2, HEAD_DIM),
            va32_s.reshape(1, Bs, Ts, N_HEADS_A, 2 * HEAD_DIM),
            kb32_s.reshape(1, Bs, Ts, N_HEADS_B, HEAD_DIM),
            vb32_s.reshape(1, Bs, Ts, N_HEADS_B, HEAD_DIM))
```

```python
import functools
import math

import jax
import jax.numpy as jnp
from jax import lax
from jax.experimental import pallas as pl
from jax.experimental.pallas import tpu as pltpu

CHUNK = 64
HEAD_DIM = 64
N_HEADS_A = 4
N_HEADS_B = 8
WIDTH_A = N_HEADS_A * 2 * HEAD_DIM
WIDTH_B = N_HEADS_B * HEAD_DIM
N_BUCKETS = 32
MAX_DISTANCE = 128
BAND_CHUNKS = 8
WINDOW_B = BAND_CHUNKS * CHUNK
REL_CLIP = 128
EPS = 1e-6
SUBLN_EPS = 1e-5
MASKED = -1e30
LOG2E = math.log2(math.e)
Q_SCALE = LOG2E * HEAD_DIM ** -0.5

LANES = 128
ATTN_BLOCK = 256
FAR_UNROLL = 4
FAR_HALVES = 2
ROW_TILE = 512
VMEM_LIMIT = 56 * 1024 * 1024

F32 = jnp.float32
BF16 = jnp.bfloat16


def _t5_bucket(rel):
    half = N_BUCKETS // 2
    n = -rel
    ret = jnp.where(n < 0, half, 0)
    n = jnp.abs(n)
    max_exact = half // 2
    nf = jnp.maximum(n, 1).astype(F32)
    large = max_exact + (jnp.log(nf / max_exact) / math.log(MAX_DISTANCE / max_exact)
                         * (half - max_exact)).astype(jnp.int32)
    large = jnp.minimum(large, half - 1)
    return ret + jnp.where(n < max_exact, n, large)


def _bias_kernel(v_ref, o_ref, *, tk, tq, lane_starts, visible):
    a = lax.broadcasted_iota(jnp.int32, (tk, tq), 0)
    b = lax.broadcasted_iota(jnp.int32, (tk, tq), 1)
    for s in range(2):
        x = jnp.broadcast_to(v_ref[s], (tk, v_ref.shape[-1]))
        x = pltpu.roll(x, 0, 1, stride=1, stride_axis=0)
        for t, start in enumerate(lane_starts):
            tile = x[:, start:start + tq]
            vis = visible(t, a, b)
            o_ref[t, :, s * tq:(s + 1) * tq] = tile if vis is None else jnp.where(vis, tile, MASKED)


def _bias_tiles(stream_vec_fn, tk, tq, offsets, visible):
    first_start = -(-tk // LANES) * LANES
    r0 = first_start + max(offsets)
    lane_starts = tuple(r0 - off for off in offsets)
    assert all(s % LANES == 0 for s in lane_starts) and tq % LANES == 0
    n_lanes = max(lane_starts) + tq
    v = stream_vec_fn(r0 - jnp.arange(n_lanes)).astype(F32) * LOG2E
    groups = v.shape[0]
    return pl.pallas_call(
        functools.partial(_bias_kernel, tk=tk, tq=tq, lane_starts=lane_starts, visible=visible),
        grid=(groups,),
        in_specs=[pl.BlockSpec((None, 2, 1, n_lanes), lambda g: (g, 0, 0, 0))],
        out_specs=pl.BlockSpec((len(offsets), None, tk, 2 * tq), lambda g: (0, g, 0, 0)),
        out_shape=jax.ShapeDtypeStruct((len(offsets), groups, tk, 2 * tq), F32),
        compiler_params=pltpu.CompilerParams(dimension_semantics=("arbitrary",),
                                             vmem_limit_bytes=VMEM_LIMIT),
        name="bias_tiles",
    )(v.reshape(groups, 2, 1, n_lanes))


def _diff_streams(table, shift=None):
    def fn(rel):
        v = jnp.transpose(table[_t5_bucket(rel)].astype(F32))
        if shift is not None:
            v = v - shift[:, None]
        return jnp.stack([v, v], axis=1)
    return fn


def _band_streams(table):
    def fn(rel):
        v = jnp.transpose(table[jnp.clip(rel, -REL_CLIP, REL_CLIP) + REL_CLIP].astype(F32))
        return v.reshape(N_HEADS_B // 2, 2, -1)
    return fn


def _diff_prompt_bias(t5_table):
    blk = ATTN_BLOCK
    far = t5_table[_t5_bucket(jnp.array(-2 * MAX_DISTANCE))].astype(F32)
    same_block_causal = lambda t, a, b: (a // CHUNK <= b // CHUNK) if t == 0 else None
    return _bias_tiles(_diff_streams(t5_table, far), blk, blk, (0, -blk), same_block_causal)


def _band_prompt_bias(band_table):
    blk = ATTN_BLOCK

    def visible(t, a, b):
        if t == 0:
            return a // CHUNK <= b // CHUNK
        if t == (BAND_CHUNKS * CHUNK) // blk:
            return a // CHUNK >= b // CHUNK
        return None

    return _bias_tiles(_band_streams(band_table), blk, blk, (0, -blk, -2 * blk), visible)


def _store_transposed(vt_ref, v):
    blk = vt_ref.shape[-1]
    for h in range(vt_ref.shape[0]):
        for c in range(vt_ref.shape[1]):
            vt_ref[h, c] = v[c * blk:(c + 1) * blk, h * LANES:(h + 1) * LANES].T.astype(BF16)


def _proj_kernel(x_ref, g_ref, w_ref, qa_ref, ka32_ref, ka16_ref, va32_ref, vta_ref,
                 qb_ref, kb16_ref, vtb_ref, kb32_ref, vb32_ref, *, tail_period):
    x = x_ref[...]
    xn = x * lax.rsqrt(jnp.mean(x * x, axis=-1, keepdims=True) + EPS) * g_ref[...]
    xn = xn.astype(BF16)

    def cols(c):
        return jnp.dot(xn, w_ref[:, c * WIDTH_A:(c + 1) * WIDTH_A], preferred_element_type=F32)

    qa_ref[...] = (cols(0) * Q_SCALE).astype(BF16)
    ka = cols(1)
    ka32_ref[...] = ka
    ka16_ref[...] = ka.astype(BF16)
    va = cols(2)
    va32_ref[...] = va
    _store_transposed(vta_ref, va)
    qb_ref[...] = (cols(3) * Q_SCALE).astype(BF16)
    kb = cols(4)
    kb16_ref[...] = kb.astype(BF16)
    vb = cols(5)
    _store_transposed(vtb_ref, vb)

    @pl.when(pl.program_id(0) % tail_period == tail_period - 1)
    def _():
        kb32_ref[...] = kb
        vb32_ref[...] = vb


def _project(x, g, w_bf, tail_period):
    n, d = x.shape
    tm = ROW_TILE
    blk = ATTN_BLOCK
    assert n % (tm * tail_period) == 0 and tm % blk == 0
    groups = WIDTH_A // LANES
    row = lambda i: (i, 0)
    tail = lambda i: (i // tail_period, 0)
    rows = lambda dt, imap=row: (jax.ShapeDtypeStruct((n, WIDTH_A), dt),
                                 pl.BlockSpec((tm, WIDTH_A), imap))
    vt = (jax.ShapeDtypeStruct((groups, n // blk, LANES, blk), BF16),
          pl.BlockSpec((groups, tm // blk, LANES, blk), lambda i: (0, i, 0, 0)))
    n_tail = n // tail_period
    tails = (jax.ShapeDtypeStruct((n_tail, WIDTH_B), F32), pl.BlockSpec((tm, WIDTH_B), tail))
    outs = [rows(BF16), rows(F32), rows(BF16), rows(F32), vt,
            rows(BF16), rows(BF16), vt, tails, tails]
    return pl.pallas_call(
        functools.partial(_proj_kernel, tail_period=tail_period),
        grid=(n // tm,),
        in_specs=[pl.BlockSpec((tm, d), row),
                  pl.BlockSpec((1, d), lambda i: (0, 0)),
                  pl.BlockSpec(w_bf.shape, lambda i: (0, 0), pipeline_mode=pl.Buffered(1))],
        out_specs=[s for _, s in outs],
        out_shape=[s for s, _ in outs],
        compiler_params=pltpu.CompilerParams(dimension_semantics=("arbitrary",),
                                             vmem_limit_bytes=VMEM_LIMIT),
        name="in_proj",
    )(x, g.reshape(1, d), w_bf)


def _attn_kernel(*refs, tq, tk, n_near, causal_walk, combine, lam_init):
    far_walk = causal_walk and combine == "diff"
    if combine == "diff":
        lam_ref, q_ref, k_ref, vt_ref, bias_ref, g_ref, o_ref, m_ref, l_ref, acc_ref = refs[:10]
    else:
        q_ref, k_ref, vt_ref, bias_ref, o_ref, m_ref, l_ref, acc_ref = refs[:8]
    qi = pl.program_id(2)

    q = q_ref[...]
    lane = lax.broadcasted_iota(jnp.int32, q.shape, 1)
    zero = jnp.zeros_like(q)
    qz = jnp.concatenate([jnp.where(lane < HEAD_DIM, q, zero),
                          jnp.where(lane >= HEAD_DIM, q, zero)], axis=0)

    def scores(kb, bias):
        start = kb * tk if isinstance(kb, int) else pl.multiple_of(kb * tk, tk)
        k = k_ref[pl.ds(start, tk), :]
        s = lax.dot_general(k, qz, (((1,), (1,)), ((), ())),
                            preferred_element_type=F32)
        return s if bias is None else s + bias_ref[bias]

    def run_tiles(tiles):
        m = jnp.full(m_ref.shape, MASKED, F32)
        l = jnp.zeros(l_ref.shape, F32)
        acc = jnp.zeros(acc_ref.shape, F32)
        s_next = scores(*tiles[0])
        for t, (kb, _) in enumerate(tiles):
            s = s_next
            if t + 1 < len(tiles):
                s_next = scores(*tiles[t + 1])
            m_new = jnp.maximum(m, jnp.max(s, axis=0, keepdims=True))
            alpha = jnp.exp2(m - m_new)
            p = jnp.exp2(s - m_new)
            l = alpha * l + jnp.sum(p, axis=0, keepdims=True)
            acc = alpha * acc + jnp.dot(vt_ref[kb], p.astype(BF16),
                                        preferred_element_type=F32)
            m = m_new
        m_ref[...] = m
        l_ref[...] = l
        acc_ref[...] = acc

    def near_tiles(c):
        return [(qi - j, j) for j in range(c)]

    if far_walk:
        sbuf_ref = refs[10]
        group = FAR_UNROLL
        n_far = jnp.maximum(qi - (n_near - 1), 0)
        n_groups = n_far // group
        left = n_far - n_groups * group
        for c in range(1, n_near):
            @pl.when(qi == c - 1)
            def _(c=c):
                run_tiles(near_tiles(c))
        for v in range(group):
            @pl.when((qi >= n_near - 1) & (left == v))
            def _(v=v):
                run_tiles(near_tiles(n_near) + [(n_groups * group + j, None) for j in range(v)])

        gmax_ref = refs[11]
        half = group // FAR_HALVES

        def store_scores(g, t, hmax):
            s = scores(g * group + t, None)
            sbuf_ref[t * tk:(t + 1) * tk, :] = s
            mt = jnp.max(s, axis=0, keepdims=True)
            return mt if hmax is None else jnp.maximum(hmax, mt)

        def group_step(g, has_next):
            m = m_ref[...]
            l = l_ref[...]
            acc = acc_ref[...]
            for h in range(FAR_HALVES):
                m_new = jnp.maximum(m, gmax_ref[h])
                alpha = jnp.exp2(m - m_new)
                l_sum = None
                pv = None
                hmax = None
                for t in range(h * half, (h + 1) * half):
                    s = sbuf_ref[t * tk:(t + 1) * tk, :]
                    if has_next:
                        hmax = store_scores(g + 1, t, hmax)
                    p = jnp.exp2(s - m_new)
                    lt = jnp.sum(p, axis=0, keepdims=True)
                    d = jnp.dot(vt_ref[g * group + t], p.astype(BF16),
                                preferred_element_type=F32)
                    l_sum = lt if l_sum is None else l_sum + lt
                    pv = d if pv is None else pv + d
                if has_next:
                    gmax_ref[h] = hmax
                l = alpha * l + l_sum
                acc = alpha * acc + pv
                m = m_new
            l_ref[...] = l
            acc_ref[...] = acc
            m_ref[...] = m

        @pl.when(n_groups > 0)
        def _():
            for h in range(FAR_HALVES):
                hmax = None
                for t in range(h * half, (h + 1) * half):
                    hmax = store_scores(0, t, hmax)
                gmax_ref[h] = hmax

            def pair(i, carry):
                group_step(2 * i, True)
                group_step(2 * i + 1, True)
                return carry

            n_pairs = (n_groups - 1) // 2
            lax.fori_loop(0, n_pairs, pair, 0)

            @pl.when(n_groups - 1 > 2 * n_pairs)
            def _():
                group_step(n_groups - 2, True)

            group_step(n_groups - 1, False)
    elif causal_walk:
        for c in range(1, n_near):
            @pl.when(qi == c - 1)
            def _(c=c):
                run_tiles(near_tiles(c))

        @pl.when(qi >= n_near - 1)
        def _():
            run_tiles(near_tiles(n_near))
    else:
        run_tiles([(j, j) for j in range(n_near)])

    o = acc_ref[...] / l_ref[...]
    if combine == "diff":
        d = o[:, :tq] - lam_ref[0] * o[:, tq:]
        y = d * lax.rsqrt(jnp.mean(d * d, axis=0, keepdims=True) + SUBLN_EPS)
        o_ref[...] = ((y.T * g_ref[...]) * (1.0 - lam_init)).astype(o_ref.dtype)
    else:
        row = lax.broadcasted_iota(jnp.int32, (LANES, tq), 0)
        o_ref[...] = jnp.where(row < HEAD_DIM, o[:, :tq], o[:, tq:]).T.astype(o_ref.dtype)


def _attention(q, k, vt, bias, *, tq, tk, causal_walk, combine, out_dtype,
               lam=None, g=None, lam_init=0.0):
    b, tq_total, width = q.shape
    tk_total = k.shape[1]
    groups = width // LANES
    n_near = bias.shape[0]
    nkb = tk_total // tk
    assert tq_total % tq == 0 and vt.shape == (groups, b * nkb, LANES, tk)
    in_specs = [
        pl.BlockSpec((None, tq, LANES), lambda bi, h, qi: (bi, qi, h)),
        pl.BlockSpec((None, tk_total, LANES), lambda bi, h, qi: (bi, 0, h)),
        pl.BlockSpec((None, nkb, LANES, tk), lambda bi, h, qi: (h, bi, 0, 0)),
        pl.BlockSpec((n_near, None, tk, 2 * tq), lambda bi, h, qi: (0, h, 0, 0)),
    ]
    args = [q, k, vt, bias]
    if combine == "diff":
        in_specs = [pl.BlockSpec(memory_space=pltpu.SMEM)] + in_specs
        in_specs.append(pl.BlockSpec((1, LANES), lambda bi, h, qi: (0, 0)))
        args = [lam.reshape(1).astype(F32)] + args + [g.reshape(1, LANES).astype(F32)]
    scratch = [pltpu.VMEM((1, 2 * tq), F32), pltpu.VMEM((1, 2 * tq), F32),
               pltpu.VMEM((LANES, 2 * tq), F32)]
    if causal_walk and combine == "diff":
        scratch.append(pltpu.VMEM((FAR_UNROLL * tk, 2 * tq), F32))
        scratch.append(pltpu.VMEM((FAR_HALVES, 1, 2 * tq), F32))
    return pl.pallas_call(
        functools.partial(_attn_kernel, tq=tq, tk=tk, n_near=n_near, causal_walk=causal_walk,
                          combine=combine, lam_init=lam_init),
        grid=(b, groups, tq_total // tq),
        in_specs=in_specs,
        out_specs=pl.BlockSpec((None, tq, LANES), lambda bi, h, qi: (bi, qi, h)),
        out_shape=jax.ShapeDtypeStruct((b, tq_total, width), out_dtype),
        scratch_shapes=scratch,
        compiler_params=pltpu.CompilerParams(
            dimension_semantics=("arbitrary", "arbitrary", "arbitrary"),
            vmem_limit_bytes=VMEM_LIMIT),
        name="attn_" + combine + ("_walk" if causal_walk else "_flat"),
    )(*args)


def _transpose_values(v, tk):
    b, t, width = v.shape
    return v.reshape(b * (t // tk), tk, width // LANES, LANES).transpose(2, 0, 3, 1)


def _rms(x, g, eps):
    return x * lax.rsqrt(jnp.mean(x * x, axis=-1, keepdims=True) + eps) * g


def _post_kernel(x_ref, oa_ref, ob_ref, p_ref, gb_ref, gm_ref, gf_ref, wout_ref, wup_ref,
                 wdown_ref, wgate_ref, wple_ref, y_ref):
    obn = _rms(ob_ref[...], gb_ref[...], EPS).astype(BF16)
    h = x_ref[...] + (
        jnp.dot(oa_ref[...], wout_ref[:WIDTH_A, :], preferred_element_type=F32)
        + jnp.dot(obn, wout_ref[WIDTH_A:, :], preferred_element_type=F32))
    c = _rms(h, gm_ref[...], EPS).astype(BF16)
    d_ff = wup_ref.shape[1]
    step = wup_ref.shape[0]
    for j in range(d_ff // step):
        u = jnp.dot(c, wup_ref[:, j * step:(j + 1) * step], preferred_element_type=F32)
        u = jnp.square(jnp.maximum(u, 0.0)).astype(BF16)
        h = h + jnp.dot(u, wdown_ref[j * step:(j + 1) * step, :], preferred_element_type=F32)
    gate = jax.nn.sigmoid(jnp.dot(h.astype(BF16), wgate_ref[...], preferred_element_type=F32))
    h = h + gate * jnp.dot(p_ref[...].astype(BF16), wple_ref[...], preferred_element_type=F32)
    y_ref[...] = _rms(h, gf_ref[...], EPS)


def _post(x, oa, ob, p, gb, gm, gf, wout, wup, wdown, wgate, wple):
    n, d = x.shape
    tm = min(ROW_TILE, n)
    assert n % tm == 0
    row = lambda i: (i, 0)
    fixed = lambda a: pl.BlockSpec(a.shape, lambda i: (0,) * a.ndim, pipeline_mode=pl.Buffered(1))
    vec = lambda a: a.reshape(1, -1).astype(F32)
    gb, gm, gf = vec(gb), vec(gm), vec(gf)
    return pl.pallas_call(
        _post_kernel,
        grid=(n // tm,),
        in_specs=[pl.BlockSpec((tm, d), row), pl.BlockSpec((tm, oa.shape[1]), row),
                  pl.BlockSpec((tm, ob.shape[1]), row), pl.BlockSpec((tm, p.shape[1]), row),
                  fixed(gb), fixed(gm), fixed(gf), fixed(wout), fixed(wup), fixed(wdown),
                  fixed(wgate), fixed(wple)],
        out_specs=pl.BlockSpec((tm, d), row),
        out_shape=jax.ShapeDtypeStruct((n, d), F32),
        compiler_params=pltpu.CompilerParams(dimension_semantics=("arbitrary",),
                                             vmem_limit_bytes=VMEM_LIMIT),
        name="post",
    )(x, oa, ob, p, gb, gm, gf, wout, wup, wdown, wgate, wple)


def _pad_rows(a, rows):
    return jnp.pad(a, ((0, 0), (0, rows - a.shape[1]), (0, 0)))


def _sample_bias(stream_vec_fn, n_keys, tk, tq, first_kpos, past, band):
    def visible(t, a, b):
        kpos = first_kpos + a
        kc = kpos // CHUNK
        qc = (past + b) // CHUNK
        vis = (kc <= qc) & (a < n_keys)
        if band:
            vis = vis & (kc >= qc - BAND_CHUNKS) & (kpos >= 0)
        return vis

    return _bias_tiles(stream_vec_fn, tk, tq, (first_kpos - past,), visible)


def kernel(x_prompt, x_sample, cache_a_k, cache_a_v, cache_b_k, cache_b_v, p_prompt, p_sample,
           t5_table, g_attn, w_in, lambda_q1, lambda_k1, lambda_q2, lambda_k2, subln_g,
           band_table, out_norm_b, w_out, g_mlp, w_up, w_down, w_ple_gate, w_ple_proj, g_final):
    depth = w_in.shape[0]
    assert depth == 1
    i = 0
    B, S, D = x_prompt.shape
    Bs, Ts, _ = x_sample.shape
    past = cache_a_k.shape[2]
    cache_b_len = cache_b_k.shape[2]
    keep = min(WINDOW_B, S)
    assert keep == ROW_TILE and S % ATTN_BLOCK == 0 and Bs * Ts == ROW_TILE

    lam_init = 0.8 - 0.6 * math.exp(-0.3 * i)
    lam = (jnp.exp(jnp.sum(lambda_q1[i].astype(F32) * lambda_k1[i].astype(F32)))
           - jnp.exp(jnp.sum(lambda_q2[i].astype(F32) * lambda_k2[i].astype(F32)))
           + lam_init)

    w_in_bf = w_in[i].astype(BF16)
    post_w = (out_norm_b[i], g_mlp[i], g_final, w_out[i].astype(BF16), w_up[i].astype(BF16),
              w_down[i].astype(BF16), w_ple_gate[i].astype(BF16), w_ple_proj[i].astype(BF16))
    blk = ATTN_BLOCK

    xp = x_prompt.reshape(B * S, D)
    (qa, ka32, ka16, va32, vta, qb, kb16, vtb, kb32, vb32) = _project(
        xp, g_attn[i], w_in_bf, tail_period=S // ROW_TILE)
    to3 = lambda a: a.reshape(B, S, a.shape[-1])
    oa = _attention(to3(qa), to3(ka16), vta, _diff_prompt_bias(t5_table), tq=blk, tk=blk,
                    causal_walk=True, combine="diff", out_dtype=BF16, lam=lam, g=subln_g[i],
                    lam_init=lam_init)
    ob = _attention(to3(qb), to3(kb16), vtb, _band_prompt_bias(band_table[i]), tq=blk, tk=blk,
                    causal_walk=True, combine="band", out_dtype=F32)
    y_prompt = _post(xp, oa.reshape(B * S, WIDTH_A), ob.reshape(B * S, WIDTH_B),
                     p_prompt[i].reshape(B * S, -1), *post_w).reshape(B, S, D)

    xs = x_sample.reshape(Bs * Ts, D)
    (qa_s, ka32_s, ka16_s, va32_s, _, qb_s, kb16_s, _, kb32_s, vb32_s) = _project(
        xs, g_attn[i], w_in_bf, tail_period=1)
    to3s = lambda a: a.reshape(Bs, Ts, a.shape[-1])
    tq_s = LANES
    assert Ts <= tq_s

    def sample_keys(cache, new):
        c = cache.reshape(Bs, cache.shape[1], -1).astype(BF16)
        allk = jnp.concatenate([c, to3s(new).astype(BF16)], axis=1)
        n_keys = allk.shape[1]
        tk = -(-n_keys // LANES) * LANES
        return _pad_rows(allk, tk), n_keys, tk

    ka_all, n_a, tk_a = sample_keys(cache_a_k[i], ka16_s)
    va_all, _, _ = sample_keys(cache_a_v[i], va32_s)
    kb_all, n_b, tk_b = sample_keys(cache_b_k[i], kb16_s)
    vb_all, _, _ = sample_keys(cache_b_v[i], vb32_s)

    bias_a = _sample_bias(_diff_streams(t5_table), n_a, tk_a, tq_s, 0, past, band=False)
    bias_b = _sample_bias(_band_streams(band_table[i]), n_b, tk_b, tq_s,
                          past - cache_b_len, past, band=True)

    oa_s = _attention(_pad_rows(to3s(qa_s), tq_s), ka_all, _transpose_values(va_all, tk_a),
                      bias_a, tq=tq_s, tk=tk_a, causal_walk=False, combine="diff",
                      out_dtype=BF16, lam=lam, g=subln_g[i], lam_init=lam_init)[:, :Ts]
    ob_s = _attention(_pad_rows(to3s(qb_s), tq_s), kb_all, _transpose_values(vb_all, tk_b),
                      bias_b, tq=tq_s, tk=tk_b, causal_walk=False, combine="band",
                      out_dtype=F32)[:, :Ts]
    y_sample = _post(xs, oa_s.reshape(Bs * Ts, WIDTH_A), ob_s.reshape(Bs * Ts, WIDTH_B),
                     p_sample[i].reshape(Bs * Ts, -1), *post_w).reshape(Bs, Ts, D)

    return (y_prompt, y_sample,
            ka32.reshape(1, B, S, N_HEADS_A, 2, HEAD_DIM),
            va32.reshape(1, B, S, N_HEADS_A, 2 * HEAD_DIM),
            kb32.reshape(1, B, keep, N_HEADS_B, HEAD_DIM),
            vb32.reshape(1, B, keep, N_HEADS_B, HEAD_DIM),
            ka32_s.reshape(1, Bs, Ts, N_HEADS_A, 2, HEAD_DIM),
            va32_s.reshape(1, Bs, Ts, N_HEADS_A, 2 * HEAD_DIM),
            kb32_s.reshape(1, Bs, Ts, N_HEADS_B, HEAD_DIM),
            vb32_s.reshape(1, Bs, Ts, N_HEADS_B, HEAD_DIM))
```

```python
import functools
import math

import jax
import jax.numpy as jnp
from jax import lax
from jax.experimental import pallas as pl
from jax.experimental.pallas import tpu as pltpu

CHUNK = 64
HEAD_DIM = 64
N_HEADS_A = 4
N_HEADS_B = 8
WIDTH_A = N_HEADS_A * 2 * HEAD_DIM
WIDTH_B = N_HEADS_B * HEAD_DIM
N_BUCKETS = 32
MAX_DISTANCE = 128
BAND_CHUNKS = 8
WINDOW_B = BAND_CHUNKS * CHUNK
REL_CLIP = 128
EPS = 1e-6
SUBLN_EPS = 1e-5
MASKED = -1e30
LOG2E = math.log2(math.e)
Q_SCALE = LOG2E * HEAD_DIM ** -0.5

LANES = 128
ATTN_BLOCK_K = 256
ATTN_BLOCK_Q = 512
FAR_UNROLL = 4
FAR_HALVES = 2
ROW_TILE = 512
VMEM_LIMIT = 56 * 1024 * 1024

F32 = jnp.float32
BF16 = jnp.bfloat16


def _t5_bucket(rel):
    half = N_BUCKETS // 2
    n = -rel
    ret = jnp.where(n < 0, half, 0)
    n = jnp.abs(n)
    max_exact = half // 2
    nf = jnp.maximum(n, 1).astype(F32)
    large = max_exact + (jnp.log(nf / max_exact) / math.log(MAX_DISTANCE / max_exact)
                         * (half - max_exact)).astype(jnp.int32)
    large = jnp.minimum(large, half - 1)
    return ret + jnp.where(n < max_exact, n, large)


def _bias_kernel(v_ref, o_ref, *, tk, tq, lane_starts, visible):
    a = lax.broadcasted_iota(jnp.int32, (tk, tq), 0)
    b = lax.broadcasted_iota(jnp.int32, (tk, tq), 1)
    for s in range(2):
        x = jnp.broadcast_to(v_ref[s], (tk, v_ref.shape[-1]))
        x = pltpu.roll(x, 0, 1, stride=1, stride_axis=0)
        for t, start in enumerate(lane_starts):
            tile = x[:, start:start + tq]
            vis = visible(t, a, b)
            o_ref[t, :, s * tq:(s + 1) * tq] = tile if vis is None else jnp.where(vis, tile, MASKED)


def _bias_tiles(stream_vec_fn, tk, tq, offsets, visible):
    first_start = -(-tk // LANES) * LANES
    r0 = first_start + max(offsets)
    lane_starts = tuple(r0 - off for off in offsets)
    assert all(s % LANES == 0 for s in lane_starts) and tq % LANES == 0
    n_lanes = max(lane_starts) + tq
    v = stream_vec_fn(r0 - jnp.arange(n_lanes)).astype(F32) * LOG2E
    groups = v.shape[0]
    return pl.pallas_call(
        functools.partial(_bias_kernel, tk=tk, tq=tq, lane_starts=lane_starts, visible=visible),
        grid=(groups,),
        in_specs=[pl.BlockSpec((None, 2, 1, n_lanes), lambda g: (g, 0, 0, 0))],
        out_specs=pl.BlockSpec((len(offsets), None, tk, 2 * tq), lambda g: (0, g, 0, 0)),
        out_shape=jax.ShapeDtypeStruct((len(offsets), groups, tk, 2 * tq), F32),
        compiler_params=pltpu.CompilerParams(dimension_semantics=("arbitrary",),
                                             vmem_limit_bytes=VMEM_LIMIT),
        name="bias_tiles",
    )(v.reshape(groups, 2, 1, n_lanes))


def _diff_streams(table, shift=None):
    def fn(rel):
        v = jnp.transpose(table[_t5_bucket(rel)].astype(F32))
        if shift is not None:
            v = v - shift[:, None]
        return jnp.stack([v, v], axis=1)
    return fn


def _band_streams(table):
    def fn(rel):
        v = jnp.transpose(table[jnp.clip(rel, -REL_CLIP, REL_CLIP) + REL_CLIP].astype(F32))
        return v.reshape(N_HEADS_B // 2, 2, -1)
    return fn


def _diff_prompt_bias(t5_table):
    tk, tq = ATTN_BLOCK_K, ATTN_BLOCK_Q
    assert tk >= MAX_DISTANCE
    far = t5_table[_t5_bucket(jnp.array(-2 * MAX_DISTANCE))].astype(F32)
    offsets = _near_offsets(tk, tq, reach=tk)
    visible = lambda t, a, b: None if offsets[t] < 0 else (a + offsets[t]) // CHUNK <= b // CHUNK
    return _bias_tiles(_diff_streams(t5_table, far), tk, tq, offsets, visible)


def _band_prompt_bias(band_table):
    tk, tq = ATTN_BLOCK_K, ATTN_BLOCK_Q
    offsets = _near_offsets(tk, tq, reach=WINDOW_B)

    def visible(t, a, b):
        kc = (a + offsets[t]) // CHUNK
        qc = b // CHUNK
        return (kc <= qc) & (kc >= qc - BAND_CHUNKS)

    return _bias_tiles(_band_streams(band_table), tk, tq, offsets, visible)


def _near_offsets(tk, tq, reach):
    assert tq % tk == 0 and reach % tk == 0
    return tuple(range(tq - tk, -reach - 1, -tk))


def _store_transposed(vt_ref, v):
    blk = vt_ref.shape[-1]
    for h in range(vt_ref.shape[0]):
        for c in range(vt_ref.shape[1]):
            vt_ref[h, c] = v[c * blk:(c + 1) * blk, h * LANES:(h + 1) * LANES].T.astype(BF16)


def _proj_kernel(x_ref, g_ref, w_ref, qa_ref, ka32_ref, ka16_ref, va32_ref, vta_ref,
                 qb_ref, kb16_ref, vtb_ref, kb32_ref, vb32_ref, *, tail_period):
    x = x_ref[...]
    xn = x * lax.rsqrt(jnp.mean(x * x, axis=-1, keepdims=True) + EPS) * g_ref[...]
    xn = xn.astype(BF16)

    def cols(c):
        return jnp.dot(xn, w_ref[:, c * WIDTH_A:(c + 1) * WIDTH_A], preferred_element_type=F32)

    qa_ref[...] = (cols(0) * Q_SCALE).astype(BF16)
    ka = cols(1)
    ka32_ref[...] = ka
    ka16_ref[...] = ka.astype(BF16)
    va = cols(2)
    va32_ref[...] = va
    _store_transposed(vta_ref, va)
    qb_ref[...] = (cols(3) * Q_SCALE).astype(BF16)
    kb = cols(4)
    kb16_ref[...] = kb.astype(BF16)
    vb = cols(5)
    _store_transposed(vtb_ref, vb)

    @pl.when(pl.program_id(0) % tail_period == tail_period - 1)
    def _():
        kb32_ref[...] = kb
        vb32_ref[...] = vb


def _project(x, g, w_bf, tail_period):
    n, d = x.shape
    tm = ROW_TILE
    blk = ATTN_BLOCK_K
    assert n % (tm * tail_period) == 0 and tm % blk == 0
    groups = WIDTH_A // LANES
    row = lambda i: (i, 0)
    tail = lambda i: (i // tail_period, 0)
    rows = lambda dt, imap=row: (jax.ShapeDtypeStruct((n, WIDTH_A), dt),
                                 pl.BlockSpec((tm, WIDTH_A), imap))
    vt = (jax.ShapeDtypeStruct((groups, n // blk, LANES, blk), BF16),
          pl.BlockSpec((groups, tm // blk, LANES, blk), lambda i: (0, i, 0, 0)))
    n_tail = n // tail_period
    tails = (jax.ShapeDtypeStruct((n_tail, WIDTH_B), F32), pl.BlockSpec((tm, WIDTH_B), tail))
    outs = [rows(BF16), rows(F32), rows(BF16), rows(F32), vt,
            rows(BF16), rows(BF16), vt, tails, tails]
    return pl.pallas_call(
        functools.partial(_proj_kernel, tail_period=tail_period),
        grid=(n // tm,),
        in_specs=[pl.BlockSpec((tm, d), row),
                  pl.BlockSpec((1, d), lambda i: (0, 0)),
                  pl.BlockSpec(w_bf.shape, lambda i: (0, 0), pipeline_mode=pl.Buffered(1))],
        out_specs=[s for _, s in outs],
        out_shape=[s for s, _ in outs],
        compiler_params=pltpu.CompilerParams(dimension_semantics=("arbitrary",),
                                             vmem_limit_bytes=VMEM_LIMIT),
        name="in_proj",
    )(x, g.reshape(1, d), w_bf)


def _attn_kernel(*refs, tq, tk, n_near, causal_walk, combine, lam_init):
    far_walk = causal_walk and combine == "diff"
    if combine == "diff":
        lam_ref, q_ref, k_ref, vt_ref, bias_ref, g_ref, o_ref, m_ref, l_ref, acc_ref = refs[:10]
    else:
        q_ref, k_ref, vt_ref, bias_ref, o_ref, m_ref, l_ref, acc_ref = refs[:8]
    qi = pl.program_id(2)

    q = q_ref[...]
    lane = lax.broadcasted_iota(jnp.int32, q.shape, 1)
    zero = jnp.zeros_like(q)
    qz = jnp.concatenate([jnp.where(lane < HEAD_DIM, q, zero),
                          jnp.where(lane >= HEAD_DIM, q, zero)], axis=0)

    def scores(kb, bias):
        start = kb * tk if isinstance(kb, int) else pl.multiple_of(kb * tk, tk)
        k = k_ref[pl.ds(start, tk), :]
        s = lax.dot_general(k, qz, (((1,), (1,)), ((), ())),
                            preferred_element_type=F32)
        return s if bias is None else s + bias_ref[bias]

    def run_tiles(tiles):
        m = jnp.full(m_ref.shape, MASKED, F32)
        l = jnp.zeros(l_ref.shape, F32)
        acc = jnp.zeros(acc_ref.shape, F32)
        s_next = scores(*tiles[0])
        for t, (kb, _) in enumerate(tiles):
            s = s_next
            if t + 1 < len(tiles):
                s_next = scores(*tiles[t + 1])
            m_new = jnp.maximum(m, jnp.max(s, axis=0, keepdims=True))
            alpha = jnp.exp2(m - m_new)
            p = jnp.exp2(s - m_new)
            l = alpha * l + jnp.sum(p, axis=0, keepdims=True)
            acc = alpha * acc + jnp.dot(vt_ref[kb], p.astype(BF16),
                                        preferred_element_type=F32)
            m = m_new
        m_ref[...] = m
        l_ref[...] = l
        acc_ref[...] = acc

    if causal_walk:
        n_visible = (tq // tk) * (qi + 1)
        short_counts = range(tq // tk, n_near, tq // tk)

    def near_tiles(c):
        return [(n_visible - 1 - j, j) for j in range(c)]

    if far_walk:
        sbuf_ref = refs[10]
        group = FAR_UNROLL
        n_far = jnp.maximum(n_visible - n_near, 0)
        n_groups = n_far // group
        left = n_far - n_groups * group
        for c in short_counts:
            @pl.when(n_visible == c)
            def _(c=c):
                run_tiles(near_tiles(c))
        for v in range(group):
            @pl.when((n_visible >= n_near) & (left == v))
            def _(v=v):
                run_tiles(near_tiles(n_near) + [(n_groups * group + j, None) for j in range(v)])

        gmax_ref = refs[11]
        half = group // FAR_HALVES

        def store_scores(g, t, hmax):
            s = scores(g * group + t, None)
            sbuf_ref[t * tk:(t + 1) * tk, :] = s
            mt = jnp.max(s, axis=0, keepdims=True)
            return mt if hmax is None else jnp.maximum(hmax, mt)

        def group_step(g, has_next):
            m = m_ref[...]
            l = l_ref[...]
            acc = acc_ref[...]
            for h in range(FAR_HALVES):
                m_new = jnp.maximum(m, gmax_ref[h])
                alpha = jnp.exp2(m - m_new)
                l_sum = None
                pv = None
                hmax = None
                for t in range(h * half, (h + 1) * half):
                    s = sbuf_ref[t * tk:(t + 1) * tk, :]
                    if has_next:
                        hmax = store_scores(g + 1, t, hmax)
                    p = jnp.exp2(s - m_new)
                    lt = jnp.sum(p, axis=0, keepdims=True)
                    d = jnp.dot(vt_ref[g * group + t], p.astype(BF16),
                                preferred_element_type=F32)
                    l_sum = lt if l_sum is None else l_sum + lt
                    pv = d if pv is None else pv + d
                if has_next:
                    gmax_ref[h] = hmax
                l = alpha * l + l_sum
                acc = alpha * acc + pv
                m = m_new
            l_ref[...] = l
            acc_ref[...] = acc
            m_ref[...] = m

        @pl.when(n_groups > 0)
        def _():
            for h in range(FAR_HALVES):
                hmax = None
                for t in range(h * half, (h + 1) * half):
                    hmax = store_scores(0, t, hmax)
                gmax_ref[h] = hmax

            def pair(i, carry):
                group_step(2 * i, True)
                group_step(2 * i + 1, True)
                return carry

            n_pairs = (n_groups - 1) // 2
            lax.fori_loop(0, n_pairs, pair, 0)

            @pl.when(n_groups - 1 > 2 * n_pairs)
            def _():
                group_step(n_groups - 2, True)

            group_step(n_groups - 1, False)
    elif causal_walk:
        for c in short_counts:
            @pl.when(n_visible == c)
            def _(c=c):
                run_tiles(near_tiles(c))

        @pl.when(n_visible >= n_near)
        def _():
            run_tiles(near_tiles(n_near))
    else:
        run_tiles([(j, j) for j in range(n_near)])

    o = acc_ref[...] / l_ref[...]
    if combine == "diff":
        d = o[:, :tq] - lam_ref[0] * o[:, tq:]
        y = d * lax.rsqrt(jnp.mean(d * d, axis=0, keepdims=True) + SUBLN_EPS)
        o_ref[...] = ((y.T * g_ref[...]) * (1.0 - lam_init)).astype(o_ref.dtype)
    else:
        row = lax.broadcasted_iota(jnp.int32, (LANES, tq), 0)
        o_ref[...] = jnp.where(row < HEAD_DIM, o[:, :tq], o[:, tq:]).T.astype(o_ref.dtype)


def _attention(q, k, vt, bias, *, tq, tk, causal_walk, combine, out_dtype,
               lam=None, g=None, lam_init=0.0):
    b, tq_total, width = q.shape
    tk_total = k.shape[1]
    groups = width // LANES
    n_near = bias.shape[0]
    nkb = tk_total // tk
    assert tq_total % tq == 0 and vt.shape == (groups, b * nkb, LANES, tk)
    in_specs = [
        pl.BlockSpec((None, tq, LANES), lambda bi, h, qi: (bi, qi, h)),
        pl.BlockSpec((None, tk_total, LANES), lambda bi, h, qi: (bi, 0, h)),
        pl.BlockSpec((None, nkb, LANES, tk), lambda bi, h, qi: (h, bi, 0, 0)),
        pl.BlockSpec((n_near, None, tk, 2 * tq), lambda bi, h, qi: (0, h, 0, 0)),
    ]
    args = [q, k, vt, bias]
    if combine == "diff":
        in_specs = [pl.BlockSpec(memory_space=pltpu.SMEM)] + in_specs
        in_specs.append(pl.BlockSpec((1, LANES), lambda bi, h, qi: (0, 0)))
        args = [lam.reshape(1).astype(F32)] + args + [g.reshape(1, LANES).astype(F32)]
    scratch = [pltpu.VMEM((1, 2 * tq), F32), pltpu.VMEM((1, 2 * tq), F32),
               pltpu.VMEM((LANES, 2 * tq), F32)]
    if causal_walk and combine == "diff":
        scratch.append(pltpu.VMEM((FAR_UNROLL * tk, 2 * tq), F32))
        scratch.append(pltpu.VMEM((FAR_HALVES, 1, 2 * tq), F32))
    return pl.pallas_call(
        functools.partial(_attn_kernel, tq=tq, tk=tk, n_near=n_near, causal_walk=causal_walk,
                          combine=combine, lam_init=lam_init),
        grid=(b, groups, tq_total // tq),
        in_specs=in_specs,
        out_specs=pl.BlockSpec((None, tq, LANES), lambda bi, h, qi: (bi, qi, h)),
        out_shape=jax.ShapeDtypeStruct((b, tq_total, width), out_dtype),
        scratch_shapes=scratch,
        compiler_params=pltpu.CompilerParams(
            dimension_semantics=("arbitrary", "arbitrary", "arbitrary"),
            vmem_limit_bytes=VMEM_LIMIT),
        name="attn_" + combine + ("_walk" if causal_walk else "_flat"),
    )(*args)


def _transpose_values(v, tk):
    b, t, width = v.shape
    return v.reshape(b * (t // tk), tk, width // LANES, LANES).transpose(2, 0, 3, 1)


def _rms(x, g, eps):
    return x * lax.rsqrt(jnp.mean(x * x, axis=-1, keepdims=True) + eps) * g


def _post_kernel(x_ref, oa_ref, ob_ref, p_ref, gb_ref, gm_ref, gf_ref, wout_ref, wup_ref,
                 wdown_ref, wgate_ref, wple_ref, y_ref):
    obn = _rms(ob_ref[...], gb_ref[...], EPS).astype(BF16)
    h = x_ref[...] + (
        jnp.dot(oa_ref[...], wout_ref[:WIDTH_A, :], preferred_element_type=F32)
        + jnp.dot(obn, wout_ref[WIDTH_A:, :], preferred_element_type=F32))
    c = _rms(h, gm_ref[...], EPS).astype(BF16)
    d_ff = wup_ref.shape[1]
    step = wup_ref.shape[0]
    for j in range(d_ff // step):
        u = jnp.dot(c, wup_ref[:, j * step:(j + 1) * step], preferred_element_type=F32)
        u = jnp.square(jnp.maximum(u, 0.0)).astype(BF16)
        h = h + jnp.dot(u, wdown_ref[j * step:(j + 1) * step, :], preferred_element_type=F32)
    gate = jax.nn.sigmoid(jnp.dot(h.astype(BF16), wgate_ref[...], preferred_element_type=F32))
    h = h + gate * jnp.dot(p_ref[...].astype(BF16), wple_ref[...], preferred_element_type=F32)
    y_ref[...] = _rms(h, gf_ref[...], EPS)


def _post(x, oa, ob, p, gb, gm, gf, wout, wup, wdown, wgate, wple):
    n, d = x.shape
    tm = min(ROW_TILE, n)
    assert n % tm == 0
    row = lambda i: (i, 0)
    fixed = lambda a: pl.BlockSpec(a.shape, lambda i: (0,) * a.ndim, pipeline_mode=pl.Buffered(1))
    vec = lambda a: a.reshape(1, -1).astype(F32)
    gb, gm, gf = vec(gb), vec(gm), vec(gf)
    return pl.pallas_call(
        _post_kernel,
        grid=(n // tm,),
        in_specs=[pl.BlockSpec((tm, d), row), pl.BlockSpec((tm, oa.shape[1]), row),
                  pl.BlockSpec((tm, ob.shape[1]), row), pl.BlockSpec((tm, p.shape[1]), row),
                  fixed(gb), fixed(gm), fixed(gf), fixed(wout), fixed(wup), fixed(wdown),
                  fixed(wgate), fixed(wple)],
        out_specs=pl.BlockSpec((tm, d), row),
        out_shape=jax.ShapeDtypeStruct((n, d), F32),
        compiler_params=pltpu.CompilerParams(dimension_semantics=("arbitrary",),
                                             vmem_limit_bytes=VMEM_LIMIT),
        name="post",
    )(x, oa, ob, p, gb, gm, gf, wout, wup, wdown, wgate, wple)


def _pad_rows(a, rows):
    return jnp.pad(a, ((0, 0), (0, rows - a.shape[1]), (0, 0)))


def _sample_bias(stream_vec_fn, n_keys, tk, tq, first_kpos, past, band):
    def visible(t, a, b):
        kpos = first_kpos + a
        kc = kpos // CHUNK
        qc = (past + b) // CHUNK
        vis = (kc <= qc) & (a < n_keys)
        if band:
            vis = vis & (kc >= qc - BAND_CHUNKS) & (kpos >= 0)
        return vis

    return _bias_tiles(stream_vec_fn, tk, tq, (first_kpos - past,), visible)


def kernel(x_prompt, x_sample, cache_a_k, cache_a_v, cache_b_k, cache_b_v, p_prompt, p_sample,
           t5_table, g_attn, w_in, lambda_q1, lambda_k1, lambda_q2, lambda_k2, subln_g,
           band_table, out_norm_b, w_out, g_mlp, w_up, w_down, w_ple_gate, w_ple_proj, g_final):
    depth = w_in.shape[0]
    assert depth == 1
    i = 0
    B, S, D = x_prompt.shape
    Bs, Ts, _ = x_sample.shape
    past = cache_a_k.shape[2]
    cache_b_len = cache_b_k.shape[2]
    keep = min(WINDOW_B, S)
    assert keep == ROW_TILE and S % ATTN_BLOCK_Q == 0 and Bs * Ts == ROW_TILE

    lam_init = 0.8 - 0.6 * math.exp(-0.3 * i)
    lam = (jnp.exp(jnp.sum(lambda_q1[i].astype(F32) * lambda_k1[i].astype(F32)))
           - jnp.exp(jnp.sum(lambda_q2[i].astype(F32) * lambda_k2[i].astype(F32)))
           + lam_init)

    w_in_bf = w_in[i].astype(BF16)
    post_w = (out_norm_b[i], g_mlp[i], g_final, w_out[i].astype(BF16), w_up[i].astype(BF16),
              w_down[i].astype(BF16), w_ple_gate[i].astype(BF16), w_ple_proj[i].astype(BF16))
    tk_p, tq_p = ATTN_BLOCK_K, ATTN_BLOCK_Q

    xp = x_prompt.reshape(B * S, D)
    (qa, ka32, ka16, va32, vta, qb, kb16, vtb, kb32, vb32) = _project(
        xp, g_attn[i], w_in_bf, tail_period=S // ROW_TILE)
    to3 = lambda a: a.reshape(B, S, a.shape[-1])
    oa = _attention(to3(qa), to3(ka16), vta, _diff_prompt_bias(t5_table), tq=tq_p, tk=tk_p,
                    causal_walk=True, combine="diff", out_dtype=BF16, lam=lam, g=subln_g[i],
                    lam_init=lam_init)
    ob = _attention(to3(qb), to3(kb16), vtb, _band_prompt_bias(band_table[i]), tq=tq_p, tk=tk_p,
                    causal_walk=True, combine="band", out_dtype=F32)
    y_prompt = _post(xp, oa.reshape(B * S, WIDTH_A), ob.reshape(B * S, WIDTH_B),
                     p_prompt[i].reshape(B * S, -1), *post_w).reshape(B, S, D)

    xs = x_sample.reshape(Bs * Ts, D)
    (qa_s, ka32_s, ka16_s, va32_s, _, qb_s, kb16_s, _, kb32_s, vb32_s) = _project(
        xs, g_attn[i], w_in_bf, tail_period=1)
    to3s = lambda a: a.reshape(Bs, Ts, a.shape[-1])
    tq_s = LANES
    assert Ts <= tq_s

    def sample_keys(cache, new):
        c = cache.reshape(Bs, cache.shape[1], -1).astype(BF16)
        allk = jnp.concatenate([c, to3s(new).astype(BF16)], axis=1)
        n_keys = allk.shape[1]
        tk = -(-n_keys // LANES) * LANES
        return _pad_rows(allk, tk), n_keys, tk

    ka_all, n_a, tk_a = sample_keys(cache_a_k[i], ka16_s)
    va_all, _, _ = sample_keys(cache_a_v[i], va32_s)
    kb_all, n_b, tk_b = sample_keys(cache_b_k[i], kb16_s)
    vb_all, _, _ = sample_keys(cache_b_v[i], vb32_s)

    bias_a = _sample_bias(_diff_streams(t5_table), n_a, tk_a, tq_s, 0, past, band=False)
    bias_b = _sample_bias(_band_streams(band_table[i]), n_b, tk_b, tq_s,
                          past - cache_b_len, past, band=True)

    oa_s = _attention(_pad_rows(to3s(qa_s), tq_s), ka_all, _transpose_values(va_all, tk_a),
                      bias_a, tq=tq_s, tk=tk_a, causal_walk=False, combine="diff",
                      out_dtype=BF16, lam=lam, g=subln_g[i], lam_init=lam_init)[:, :Ts]
    ob_s = _attention(_pad_rows(to3s(qb_s), tq_s), kb_all, _transpose_values(vb_all, tk_b),
                      bias_b, tq=tq_s, tk=tk_b, causal_walk=False, combine="band",
                      out_dtype=F32)[:, :Ts]
    y_sample = _post(xs, oa_s.reshape(Bs * Ts, WIDTH_A), ob_s.reshape(Bs * Ts, WIDTH_B),
                     p_sample[i].reshape(Bs * Ts, -1), *post_w).reshape(Bs, Ts, D)

    return (y_prompt, y_sample,
            ka32.reshape(1, B, S, N_HEADS_A, 2, HEAD_DIM),
            va32.reshape(1, B, S, N_HEADS_A, 2 * HEAD_DIM),
            kb32.reshape(1, B, keep, N_HEADS_B, HEAD_DIM),
            vb32.reshape(1, B, keep, N_HEADS_B, HEAD_DIM),
            ka32_s.reshape(1, Bs, Ts, N_HEADS_A, 2, HEAD_DIM),
            va32_s.reshape(1, Bs, Ts, N_HEADS_A, 2 * HEAD_DIM),
            kb32_s.reshape(1, Bs, Ts, N_HEADS_B, HEAD_DIM),
            vb32_s.reshape(1, Bs, Ts, N_HEADS_B, HEAD_DIM))
```

```python
import functools
import math

import jax
import jax.numpy as jnp
from jax import lax
from jax.experimental import pallas as pl
from jax.experimental.pallas import tpu as pltpu

CHUNK = 64
HEAD_DIM = 64
N_HEADS_A = 4
N_HEADS_B = 8
WIDTH_A = N_HEADS_A * 2 * HEAD_DIM
WIDTH_B = N_HEADS_B * HEAD_DIM
N_BUCKETS = 32
MAX_DISTANCE = 128
BAND_CHUNKS = 8
WINDOW_B = BAND_CHUNKS * CHUNK
REL_CLIP = 128
EPS = 1e-6
SUBLN_EPS = 1e-5
MASKED = -1e30
LOG2E = math.log2(math.e)
Q_SCALE = LOG2E * HEAD_DIM ** -0.5

LANES = 128
ATTN_BLOCK_K = 256
ATTN_BLOCK_Q = 512
GROUP_TILES = 4
GROUP_HALVES = 2
ROW_TILE = 512
VMEM_LIMIT = 56 * 1024 * 1024

F32 = jnp.float32
BF16 = jnp.bfloat16


def _t5_bucket(rel):
    half = N_BUCKETS // 2
    n = -rel
    ret = jnp.where(n < 0, half, 0)
    n = jnp.abs(n)
    max_exact = half // 2
    nf = jnp.maximum(n, 1).astype(F32)
    large = max_exact + (jnp.log(nf / max_exact) / math.log(MAX_DISTANCE / max_exact)
                         * (half - max_exact)).astype(jnp.int32)
    large = jnp.minimum(large, half - 1)
    return ret + jnp.where(n < max_exact, n, large)


def _bias_kernel(v_ref, o_ref, *, tk, tq, lane_starts, visible):
    a = lax.broadcasted_iota(jnp.int32, (tk, tq), 0)
    b = lax.broadcasted_iota(jnp.int32, (tk, tq), 1)
    for s in range(2):
        x = jnp.broadcast_to(v_ref[s], (tk, v_ref.shape[-1]))
        x = pltpu.roll(x, 0, 1, stride=1, stride_axis=0)
        for t, start in enumerate(lane_starts):
            tile = x[:, start:start + tq]
            vis = visible(t, a, b)
            o_ref[t, :, s * tq:(s + 1) * tq] = tile if vis is None else jnp.where(vis, tile, MASKED)


def _bias_tiles(stream_vec_fn, tk, tq, offsets, visible):
    first_start = -(-tk // LANES) * LANES
    r0 = first_start + max(offsets)
    lane_starts = tuple(r0 - off for off in offsets)
    assert all(s % LANES == 0 for s in lane_starts) and tq % LANES == 0
    n_lanes = max(lane_starts) + tq
    v = stream_vec_fn(r0 - jnp.arange(n_lanes)).astype(F32) * LOG2E
    groups = v.shape[0]
    return pl.pallas_call(
        functools.partial(_bias_kernel, tk=tk, tq=tq, lane_starts=lane_starts, visible=visible),
        grid=(groups,),
        in_specs=[pl.BlockSpec((None, 2, 1, n_lanes), lambda g: (g, 0, 0, 0))],
        out_specs=pl.BlockSpec((len(offsets), None, tk, 2 * tq), lambda g: (0, g, 0, 0)),
        out_shape=jax.ShapeDtypeStruct((len(offsets), groups, tk, 2 * tq), F32),
        compiler_params=pltpu.CompilerParams(dimension_semantics=("arbitrary",),
                                             vmem_limit_bytes=VMEM_LIMIT),
        name="bias_tiles",
    )(v.reshape(groups, 2, 1, n_lanes))


def _diff_streams(table, shift=None):
    def fn(rel):
        v = jnp.transpose(table[_t5_bucket(rel)].astype(F32))
        if shift is not None:
            v = v - shift[:, None]
        return jnp.stack([v, v], axis=1)
    return fn


def _band_streams(table):
    def fn(rel):
        v = jnp.transpose(table[jnp.clip(rel, -REL_CLIP, REL_CLIP) + REL_CLIP].astype(F32))
        return v.reshape(N_HEADS_B // 2, 2, -1)
    return fn


def _diff_prompt_bias(t5_table):
    tk, tq = ATTN_BLOCK_K, ATTN_BLOCK_Q
    assert tk >= MAX_DISTANCE
    far = t5_table[_t5_bucket(jnp.array(-2 * MAX_DISTANCE))].astype(F32)
    offsets = _near_offsets(tk, tq, reach=tk)
    visible = lambda t, a, b: None if offsets[t] < 0 else (a + offsets[t]) // CHUNK <= b // CHUNK
    return _bias_tiles(_diff_streams(t5_table, far), tk, tq, offsets, visible)


def _band_prompt_bias(band_table):
    tk, tq = ATTN_BLOCK_K, ATTN_BLOCK_Q
    offsets = _near_offsets(tk, tq, reach=WINDOW_B)

    def visible(t, a, b):
        kc = (a + offsets[t]) // CHUNK
        qc = b // CHUNK
        return (kc <= qc) & (kc >= qc - BAND_CHUNKS)

    return _bias_tiles(_band_streams(band_table), tk, tq, offsets, visible)


def _near_offsets(tk, tq, reach):
    assert tq % tk == 0 and reach % tk == 0
    return tuple(range(tq - tk, -reach - 1, -tk))


def _store_transposed(vt_ref, v):
    blk = vt_ref.shape[-1]
    for h in range(vt_ref.shape[0]):
        for c in range(vt_ref.shape[1]):
            vt_ref[h, c] = v[c * blk:(c + 1) * blk, h * LANES:(h + 1) * LANES].T.astype(BF16)


def _proj_kernel(x_ref, g_ref, w_ref, qa_ref, ka32_ref, ka16_ref, va32_ref, vta_ref,
                 qb_ref, kb16_ref, vtb_ref, kb32_ref, vb32_ref, *, tail_period):
    x = x_ref[...]
    xn = x * lax.rsqrt(jnp.mean(x * x, axis=-1, keepdims=True) + EPS) * g_ref[...]
    xn = xn.astype(BF16)

    def cols(c):
        return jnp.dot(xn, w_ref[:, c * WIDTH_A:(c + 1) * WIDTH_A], preferred_element_type=F32)

    qa_ref[...] = (cols(0) * Q_SCALE).astype(BF16)
    ka = cols(1)
    ka32_ref[...] = ka
    ka16_ref[...] = ka.astype(BF16)
    va = cols(2)
    va32_ref[...] = va
    _store_transposed(vta_ref, va)
    qb_ref[...] = (cols(3) * Q_SCALE).astype(BF16)
    kb = cols(4)
    kb16_ref[...] = kb.astype(BF16)
    vb = cols(5)
    _store_transposed(vtb_ref, vb)

    @pl.when(pl.program_id(0) % tail_period == tail_period - 1)
    def _():
        kb32_ref[...] = kb
        vb32_ref[...] = vb


def _project(x, g, w_bf, tail_period):
    n, d = x.shape
    tm = ROW_TILE
    blk = ATTN_BLOCK_K
    assert n % (tm * tail_period) == 0 and tm % blk == 0
    groups = WIDTH_A // LANES
    row = lambda i: (i, 0)
    tail = lambda i: (i // tail_period, 0)
    rows = lambda dt, imap=row: (jax.ShapeDtypeStruct((n, WIDTH_A), dt),
                                 pl.BlockSpec((tm, WIDTH_A), imap))
    vt = (jax.ShapeDtypeStruct((groups, n // blk, LANES, blk), BF16),
          pl.BlockSpec((groups, tm // blk, LANES, blk), lambda i: (0, i, 0, 0)))
    n_tail = n // tail_period
    tails = (jax.ShapeDtypeStruct((n_tail, WIDTH_B), F32), pl.BlockSpec((tm, WIDTH_B), tail))
    outs = [rows(BF16), rows(F32), rows(BF16), rows(F32), vt,
            rows(BF16), rows(BF16), vt, tails, tails]
    return pl.pallas_call(
        functools.partial(_proj_kernel, tail_period=tail_period),
        grid=(n // tm,),
        in_specs=[pl.BlockSpec((tm, d), row),
                  pl.BlockSpec((1, d), lambda i: (0, 0)),
                  pl.BlockSpec(w_bf.shape, lambda i: (0, 0), pipeline_mode=pl.Buffered(1))],
        out_specs=[s for _, s in outs],
        out_shape=[s for s, _ in outs],
        compiler_params=pltpu.CompilerParams(dimension_semantics=("arbitrary",),
                                             vmem_limit_bytes=VMEM_LIMIT),
        name="in_proj",
    )(x, g.reshape(1, d), w_bf)


def _attn_kernel(*refs, tq, tk, n_near, causal_walk, combine, lam_init):
    far_walk = causal_walk and combine == "diff"
    if combine == "diff":
        lam_ref, q_ref, k_ref, vt_ref, bias_ref, g_ref, o_ref, m_ref, l_ref, acc_ref = refs[:10]
    else:
        q_ref, k_ref, vt_ref, bias_ref, o_ref, m_ref, l_ref, acc_ref = refs[:8]
    qi = pl.program_id(2)

    q = q_ref[...]
    lane = lax.broadcasted_iota(jnp.int32, q.shape, 1)
    zero = jnp.zeros_like(q)
    qz = jnp.concatenate([jnp.where(lane < HEAD_DIM, q, zero),
                          jnp.where(lane >= HEAD_DIM, q, zero)], axis=0)

    def scores(kb):
        start = kb * tk if isinstance(kb, int) else pl.multiple_of(kb * tk, tk)
        k = k_ref[pl.ds(start, tk), :]
        return lax.dot_general(k, qz, (((1,), (1,)), ((), ())),
                               preferred_element_type=F32)

    if not causal_walk:
        m = jnp.full(m_ref.shape, MASKED, F32)
        l = jnp.zeros(l_ref.shape, F32)
        acc = jnp.zeros(acc_ref.shape, F32)
        for t in range(n_near):
            s = scores(t) + bias_ref[t]
            m_new = jnp.maximum(m, jnp.max(s, axis=0, keepdims=True))
            alpha = jnp.exp2(m - m_new)
            p = jnp.exp2(s - m_new)
            l = alpha * l + jnp.sum(p, axis=0, keepdims=True)
            acc = alpha * acc + jnp.dot(vt_ref[t], p.astype(BF16), preferred_element_type=F32)
            m = m_new
        m_ref[...] = m
        l_ref[...] = l
        acc_ref[...] = acc
    else:
        sbuf_ref, gmax_ref = refs[-2:]
        group = GROUP_TILES
        half = group // GROUP_HALVES
        n_visible = (tq // tk) * (qi + 1)
        n_tiles = n_visible if far_walk else jnp.minimum(n_visible, n_near)

        def key_tile(n):
            return jnp.maximum(n_visible - 1 - n, 0)

        def store_scores(g, t, hmax, masked):
            n = g * group + t
            s = scores(key_tile(n))
            if isinstance(g, int) and n < n_near:
                s = s + bias_ref[n]
            if masked:
                s = jnp.where(n < n_tiles, s, MASKED)
            sbuf_ref[t * tk:(t + 1) * tk, :] = s
            mt = jnp.max(s, axis=0, keepdims=True)
            return mt if hmax is None else jnp.maximum(hmax, mt)

        def group_step(g, produce):
            m = m_ref[...]
            l = l_ref[...]
            acc = acc_ref[...]
            for h in range(GROUP_HALVES):
                m_new = jnp.maximum(m, gmax_ref[h])
                alpha = jnp.exp2(m - m_new)
                l_sum = None
                pv = None
                hmax = None
                for t in range(h * half, (h + 1) * half):
                    s = sbuf_ref[t * tk:(t + 1) * tk, :]
                    if produce is not None:
                        hmax = store_scores(g + 1, t, hmax, produce == "masked")
                    p = jnp.exp2(s - m_new)
                    lt = jnp.sum(p, axis=0, keepdims=True)
                    d = jnp.dot(vt_ref[key_tile(g * group + t)], p.astype(BF16),
                                preferred_element_type=F32)
                    l_sum = lt if l_sum is None else l_sum + lt
                    pv = d if pv is None else pv + d
                if produce is not None:
                    gmax_ref[h] = hmax
                l = alpha * l + l_sum
                acc = alpha * acc + pv
                m = m_new
            l_ref[...] = l
            acc_ref[...] = acc
            m_ref[...] = m

        m_ref[...] = jnp.full(m_ref.shape, MASKED, F32)
        l_ref[...] = jnp.zeros(l_ref.shape, F32)
        acc_ref[...] = jnp.zeros(acc_ref.shape, F32)
        for h in range(GROUP_HALVES):
            hmax = None
            for t in range(h * half, (h + 1) * half):
                hmax = store_scores(0, t, hmax, True)
            gmax_ref[h] = hmax

        if far_walk:
            n_groups = (n_tiles + group - 1) // group
            n_plain = jnp.maximum(n_groups - 2, 0)

            def pair(i, carry):
                group_step(2 * i, "plain")
                group_step(2 * i + 1, "plain")
                return carry

            lax.fori_loop(0, n_plain // 2, pair, 0)

            @pl.when(n_plain % 2 == 1)
            def _():
                group_step(n_plain - 1, "plain")

            @pl.when(n_groups >= 2)
            def _():
                group_step(n_groups - 2, "masked")

            group_step(n_groups - 1, None)
        else:
            assert n_near <= group
            group_step(0, None)

    o = acc_ref[...] / l_ref[...]
    if combine == "diff":
        d = o[:, :tq] - lam_ref[0] * o[:, tq:]
        y = d * lax.rsqrt(jnp.mean(d * d, axis=0, keepdims=True) + SUBLN_EPS)
        o_ref[...] = ((y.T * g_ref[...]) * (1.0 - lam_init)).astype(o_ref.dtype)
    else:
        row = lax.broadcasted_iota(jnp.int32, (LANES, tq), 0)
        o_ref[...] = jnp.where(row < HEAD_DIM, o[:, :tq], o[:, tq:]).T.astype(o_ref.dtype)


def _attention(q, k, vt, bias, *, tq, tk, causal_walk, combine, out_dtype,
               lam=None, g=None, lam_init=0.0):
    b, tq_total, width = q.shape
    tk_total = k.shape[1]
    groups = width // LANES
    n_near = bias.shape[0]
    nkb = tk_total // tk
    assert tq_total % tq == 0 and vt.shape == (groups, b * nkb, LANES, tk)
    in_specs = [
        pl.BlockSpec((None, tq, LANES), lambda bi, h, qi: (bi, qi, h)),
        pl.BlockSpec((None, tk_total, LANES), lambda bi, h, qi: (bi, 0, h)),
        pl.BlockSpec((None, nkb, LANES, tk), lambda bi, h, qi: (h, bi, 0, 0)),
        pl.BlockSpec((n_near, None, tk, 2 * tq), lambda bi, h, qi: (0, h, 0, 0)),
    ]
    args = [q, k, vt, bias]
    if combine == "diff":
        in_specs = [pl.BlockSpec(memory_space=pltpu.SMEM)] + in_specs
        in_specs.append(pl.BlockSpec((1, LANES), lambda bi, h, qi: (0, 0)))
        args = [lam.reshape(1).astype(F32)] + args + [g.reshape(1, LANES).astype(F32)]
    scratch = [pltpu.VMEM((1, 2 * tq), F32), pltpu.VMEM((1, 2 * tq), F32),
               pltpu.VMEM((LANES, 2 * tq), F32)]
    if causal_walk:
        scratch.append(pltpu.VMEM((GROUP_TILES * tk, 2 * tq), F32))
        scratch.append(pltpu.VMEM((GROUP_HALVES, 1, 2 * tq), F32))
    return pl.pallas_call(
        functools.partial(_attn_kernel, tq=tq, tk=tk, n_near=n_near, causal_walk=causal_walk,
                          combine=combine, lam_init=lam_init),
        grid=(b, groups, tq_total // tq),
        in_specs=in_specs,
        out_specs=pl.BlockSpec((None, tq, LANES), lambda bi, h, qi: (bi, qi, h)),
        out_shape=jax.ShapeDtypeStruct((b, tq_total, width), out_dtype),
        scratch_shapes=scratch,
        compiler_params=pltpu.CompilerParams(
            dimension_semantics=("arbitrary", "arbitrary", "arbitrary"),
            vmem_limit_bytes=VMEM_LIMIT),
        name="attn_" + combine + ("_walk" if causal_walk else "_flat"),
    )(*args)


def _transpose_values(v, tk):
    b, t, width = v.shape
    return v.reshape(b * (t // tk), tk, width // LANES, LANES).transpose(2, 0, 3, 1)


def _rms(x, g, eps):
    return x * lax.rsqrt(jnp.mean(x * x, axis=-1, keepdims=True) + eps) * g


def _post_kernel(x_ref, oa_ref, ob_ref, p_ref, gb_ref, gm_ref, gf_ref, wout_ref, wup_ref,
                 wdown_ref, wgate_ref, wple_ref, y_ref):
    obn = _rms(ob_ref[...], gb_ref[...], EPS).astype(BF16)
    h = x_ref[...] + (
        jnp.dot(oa_ref[...], wout_ref[:WIDTH_A, :], preferred_element_type=F32)
        + jnp.dot(obn, wout_ref[WIDTH_A:, :], preferred_element_type=F32))
    c = _rms(h, gm_ref[...], EPS).astype(BF16)
    d_ff = wup_ref.shape[1]
    step = wup_ref.shape[0]
    for j in range(d_ff // step):
        u = jnp.dot(c, wup_ref[:, j * step:(j + 1) * step], preferred_element_type=F32)
        u = jnp.square(jnp.maximum(u, 0.0)).astype(BF16)
        h = h + jnp.dot(u, wdown_ref[j * step:(j + 1) * step, :], preferred_element_type=F32)
    gate = jax.nn.sigmoid(jnp.dot(h.astype(BF16), wgate_ref[...], preferred_element_type=F32))
    h = h + gate * jnp.dot(p_ref[...].astype(BF16), wple_ref[...], preferred_element_type=F32)
    y_ref[...] = _rms(h, gf_ref[...], EPS)


def _post(x, oa, ob, p, gb, gm, gf, wout, wup, wdown, wgate, wple):
    n, d = x.shape
    tm = min(ROW_TILE, n)
    assert n % tm == 0
    row = lambda i: (i, 0)
    fixed = lambda a: pl.BlockSpec(a.shape, lambda i: (0,) * a.ndim, pipeline_mode=pl.Buffered(1))
    vec = lambda a: a.reshape(1, -1).astype(F32)
    gb, gm, gf = vec(gb), vec(gm), vec(gf)
    return pl.pallas_call(
        _post_kernel,
        grid=(n // tm,),
        in_specs=[pl.BlockSpec((tm, d), row), pl.BlockSpec((tm, oa.shape[1]), row),
                  pl.BlockSpec((tm, ob.shape[1]), row), pl.BlockSpec((tm, p.shape[1]), row),
                  fixed(gb), fixed(gm), fixed(gf), fixed(wout), fixed(wup), fixed(wdown),
                  fixed(wgate), fixed(wple)],
        out_specs=pl.BlockSpec((tm, d), row),
        out_shape=jax.ShapeDtypeStruct((n, d), F32),
        compiler_params=pltpu.CompilerParams(dimension_semantics=("arbitrary",),
                                             vmem_limit_bytes=VMEM_LIMIT),
        name="post",
    )(x, oa, ob, p, gb, gm, gf, wout, wup, wdown, wgate, wple)


def _pad_rows(a, rows):
    return jnp.pad(a, ((0, 0), (0, rows - a.shape[1]), (0, 0)))


def _sample_bias(stream_vec_fn, n_keys, tk, tq, first_kpos, past, band):
    def visible(t, a, b):
        kpos = first_kpos + a
        kc = kpos // CHUNK
        qc = (past + b) // CHUNK
        vis = (kc <= qc) & (a < n_keys)
        if band:
            vis = vis & (kc >= qc - BAND_CHUNKS) & (kpos >= 0)
        return vis

    return _bias_tiles(stream_vec_fn, tk, tq, (first_kpos - past,), visible)


def kernel(x_prompt, x_sample, cache_a_k, cache_a_v, cache_b_k, cache_b_v, p_prompt, p_sample,
           t5_table, g_attn, w_in, lambda_q1, lambda_k1, lambda_q2, lambda_k2, subln_g,
           band_table, out_norm_b, w_out, g_mlp, w_up, w_down, w_ple_gate, w_ple_proj, g_final):
    depth = w_in.shape[0]
    assert depth == 1
    i = 0
    B, S, D = x_prompt.shape
    Bs, Ts, _ = x_sample.shape
    past = cache_a_k.shape[2]
    cache_b_len = cache_b_k.shape[2]
    keep = min(WINDOW_B, S)
    assert keep == ROW_TILE and S % ATTN_BLOCK_Q == 0 and Bs * Ts == ROW_TILE

    lam_init = 0.8 - 0.6 * math.exp(-0.3 * i)
    lam = (jnp.exp(jnp.sum(lambda_q1[i].astype(F32) * lambda_k1[i].astype(F32)))
           - jnp.exp(jnp.sum(lambda_q2[i].astype(F32) * lambda_k2[i].astype(F32)))
           + lam_init)

    w_in_bf = w_in[i].astype(BF16)
    post_w = (out_norm_b[i], g_mlp[i], g_final, w_out[i].astype(BF16), w_up[i].astype(BF16),
              w_down[i].astype(BF16), w_ple_gate[i].astype(BF16), w_ple_proj[i].astype(BF16))
    tk_p, tq_p = ATTN_BLOCK_K, ATTN_BLOCK_Q

    xp = x_prompt.reshape(B * S, D)
    (qa, ka32, ka16, va32, vta, qb, kb16, vtb, kb32, vb32) = _project(
        xp, g_attn[i], w_in_bf, tail_period=S // ROW_TILE)
    to3 = lambda a: a.reshape(B, S, a.shape[-1])
    oa = _attention(to3(qa), to3(ka16), vta, _diff_prompt_bias(t5_table), tq=tq_p, tk=tk_p,
                    causal_walk=True, combine="diff", out_dtype=BF16, lam=lam, g=subln_g[i],
                    lam_init=lam_init)
    ob = _attention(to3(qb), to3(kb16), vtb, _band_prompt_bias(band_table[i]), tq=tq_p, tk=tk_p,
                    causal_walk=True, combine="band", out_dtype=F32)
    y_prompt = _post(xp, oa.reshape(B * S, WIDTH_A), ob.reshape(B * S, WIDTH_B),
                     p_prompt[i].reshape(B * S, -1), *post_w).reshape(B, S, D)

    xs = x_sample.reshape(Bs * Ts, D)
    (qa_s, ka32_s, ka16_s, va32_s, _, qb_s, kb16_s, _, kb32_s, vb32_s) = _project(
        xs, g_attn[i], w_in_bf, tail_period=1)
    to3s = lambda a: a.reshape(Bs, Ts, a.shape[-1])
    tq_s = LANES
    assert Ts <= tq_s

    def sample_keys(cache, new):
        c = cache.reshape(Bs, cache.shape[1], -1).astype(BF16)
        allk = jnp.concatenate([c, to3s(new).astype(BF16)], axis=1)
        n_keys = allk.shape[1]
        tk = -(-n_keys // LANES) * LANES
        return _pad_rows(allk, tk), n_keys, tk

    ka_all, n_a, tk_a = sample_keys(cache_a_k[i], ka16_s)
    va_all, _, _ = sample_keys(cache_a_v[i], va32_s)
    kb_all, n_b, tk_b = sample_keys(cache_b_k[i], kb16_s)
    vb_all, _, _ = sample_keys(cache_b_v[i], vb32_s)

    bias_a = _sample_bias(_diff_streams(t5_table), n_a, tk_a, tq_s, 0, past, band=False)
    bias_b = _sample_bias(_band_streams(band_table[i]), n_b, tk_b, tq_s,
                          past - cache_b_len, past, band=True)

    oa_s = _attention(_pad_rows(to3s(qa_s), tq_s), ka_all, _transpose_values(va_all, tk_a),
                      bias_a, tq=tq_s, tk=tk_a, causal_walk=False, combine="diff",
                      out_dtype=BF16, lam=lam, g=subln_g[i], lam_init=lam_init)[:, :Ts]
    ob_s = _attention(_pad_rows(to3s(qb_s), tq_s), kb_all, _transpose_values(vb_all, tk_b),
                      bias_b, tq=tq_s, tk=tk_b, causal_walk=False, combine="band",
                      out_dtype=F32)[:, :Ts]
    y_sample = _post(xs, oa_s.reshape(Bs * Ts, WIDTH_A), ob_s.reshape(Bs * Ts, WIDTH_B),
                     p_sample[i].reshape(Bs * Ts, -1), *post_w).reshape(Bs, Ts, D)

    return (y_prompt, y_sample,
            ka32.reshape(1, B, S, N_HEADS_A, 2, HEAD_DIM),
            va32.reshape(1, B, S, N_HEADS_A, 2 * HEAD_DIM),
            kb32.reshape(1, B, keep, N_HEADS_B, HEAD_DIM),
            vb32.reshape(1, B, keep, N_HEADS_B, HEAD_DIM),
            ka32_s.reshape(1, Bs, Ts, N_HEADS_A, 2, HEAD_DIM),
            va32_s.reshape(1, Bs, Ts, N_HEADS_A, 2 * HEAD_DIM),
            kb32_s.reshape(1, Bs, Ts, N_HEADS_B, HEAD_DIM),
            vb32_s.reshape(1, Bs, Ts, N_HEADS_B, HEAD_DIM))
```

```python
import functools
import math

import jax
import jax.numpy as jnp
from jax import lax
from jax.experimental import pallas as pl
from jax.experimental.pallas import tpu as pltpu

CHUNK = 64
HEAD_DIM = 64
N_HEADS_A = 4
N_HEADS_B = 8
WIDTH_A = N_HEADS_A * 2 * HEAD_DIM
WIDTH_B = N_HEADS_B * HEAD_DIM
N_BUCKETS = 32
MAX_DISTANCE = 128
BAND_CHUNKS = 8
WINDOW_B = BAND_CHUNKS * CHUNK
REL_CLIP = 128
EPS = 1e-6
SUBLN_EPS = 1e-5
MASKED = -1e30
LOG2E = math.log2(math.e)
Q_SCALE = LOG2E * HEAD_DIM ** -0.5

LANES = 128
ATTN_BLOCK_K = 256
ATTN_BLOCK_Q = 512
GROUP_TILES = 4
GROUP_HALVES = 2
ROW_TILE = 512
VMEM_LIMIT = 56 * 1024 * 1024

F32 = jnp.float32
BF16 = jnp.bfloat16


def _t5_bucket(rel):
    half = N_BUCKETS // 2
    n = -rel
    ret = jnp.where(n < 0, half, 0)
    n = jnp.abs(n)
    max_exact = half // 2
    nf = jnp.maximum(n, 1).astype(F32)
    large = max_exact + (jnp.log(nf / max_exact) / math.log(MAX_DISTANCE / max_exact)
                         * (half - max_exact)).astype(jnp.int32)
    large = jnp.minimum(large, half - 1)
    return ret + jnp.where(n < max_exact, n, large)


def _bias_kernel(v_ref, o_ref, *, tk, tq, lane_starts, visible):
    a = lax.broadcasted_iota(jnp.int32, (tk, tq), 0)
    b = lax.broadcasted_iota(jnp.int32, (tk, tq), 1)
    for s in range(2):
        x = jnp.broadcast_to(v_ref[s], (tk, v_ref.shape[-1]))
        x = pltpu.roll(x, 0, 1, stride=1, stride_axis=0)
        for t, start in enumerate(lane_starts):
            tile = x[:, start:start + tq]
            vis = visible(t, a, b)
            o_ref[t, :, s * tq:(s + 1) * tq] = tile if vis is None else jnp.where(vis, tile, MASKED)


def _bias_tiles(stream_vec_fn, tk, tq, offsets, visible):
    first_start = -(-tk // LANES) * LANES
    r0 = first_start + max(offsets)
    lane_starts = tuple(r0 - off for off in offsets)
    assert all(s % LANES == 0 for s in lane_starts) and tq % LANES == 0
    n_lanes = max(lane_starts) + tq
    v = stream_vec_fn(r0 - jnp.arange(n_lanes)).astype(F32) * LOG2E
    groups = v.shape[0]
    return pl.pallas_call(
        functools.partial(_bias_kernel, tk=tk, tq=tq, lane_starts=lane_starts, visible=visible),
        grid=(groups,),
        in_specs=[pl.BlockSpec((None, 2, 1, n_lanes), lambda g: (g, 0, 0, 0))],
        out_specs=pl.BlockSpec((len(offsets), None, tk, 2 * tq), lambda g: (0, g, 0, 0)),
        out_shape=jax.ShapeDtypeStruct((len(offsets), groups, tk, 2 * tq), F32),
        compiler_params=pltpu.CompilerParams(dimension_semantics=("arbitrary",),
                                             vmem_limit_bytes=VMEM_LIMIT),
        name="bias_tiles",
    )(v.reshape(groups, 2, 1, n_lanes))


def _diff_streams(table, shift=None):
    def fn(rel):
        v = jnp.transpose(table[_t5_bucket(rel)].astype(F32))
        if shift is not None:
            v = v - shift[:, None]
        return jnp.stack([v, v], axis=1)
    return fn


def _band_streams(table):
    def fn(rel):
        v = jnp.transpose(table[jnp.clip(rel, -REL_CLIP, REL_CLIP) + REL_CLIP].astype(F32))
        return v.reshape(N_HEADS_B // 2, 2, -1)
    return fn


def _diff_prompt_bias(t5_table):
    tk, tq = ATTN_BLOCK_K, ATTN_BLOCK_Q
    assert tk >= MAX_DISTANCE
    far = t5_table[_t5_bucket(jnp.array(-2 * MAX_DISTANCE))].astype(F32)
    offsets = _near_offsets(tk, tq, reach=tk)
    visible = lambda t, a, b: None if offsets[t] < 0 else (a + offsets[t]) // CHUNK <= b // CHUNK
    return _bias_tiles(_diff_streams(t5_table, far), tk, tq, offsets, visible)


def _band_prompt_bias(band_table):
    tk, tq = ATTN_BLOCK_K, ATTN_BLOCK_Q
    offsets = _near_offsets(tk, tq, reach=WINDOW_B)

    def visible(t, a, b):
        kc = (a + offsets[t]) // CHUNK
        qc = b // CHUNK
        return (kc <= qc) & (kc >= qc - BAND_CHUNKS)

    return _bias_tiles(_band_streams(band_table), tk, tq, offsets, visible)


def _near_offsets(tk, tq, reach):
    assert tq % tk == 0 and reach % tk == 0
    return tuple(range(tq - tk, -reach - 1, -tk))


def _store_transposed(vt_ref, v):
    blk = vt_ref.shape[-1]
    for h in range(vt_ref.shape[0]):
        for c in range(vt_ref.shape[1]):
            vt_ref[h, c] = v[c * blk:(c + 1) * blk, h * LANES:(h + 1) * LANES].T.astype(BF16)


def _proj_kernel(x_ref, g_ref, w_ref, qa_ref, ka32_ref, ka16_ref, va32_ref, vta_ref,
                 qb_ref, kb16_ref, vtb_ref, kb32_ref, vb32_ref, *, tail_period):
    x = x_ref[...]
    xn = x * lax.rsqrt(jnp.mean(x * x, axis=-1, keepdims=True) + EPS) * g_ref[...]
    xn = xn.astype(BF16)

    def cols(c):
        return jnp.dot(xn, w_ref[:, c * WIDTH_A:(c + 1) * WIDTH_A], preferred_element_type=F32)

    qa_ref[...] = (cols(0) * Q_SCALE).astype(BF16)
    ka = cols(1)
    if len(ka32_ref.shape) == 2:
        ka32_ref[...] = ka
    else:
        for h in range(N_HEADS_A):
            t = ka[:, h * LANES:(h + 1) * LANES].T
            ka32_ref[h, 0] = t[:HEAD_DIM]
            ka32_ref[h, 1] = t[HEAD_DIM:]
    ka16_ref[...] = ka.astype(BF16)
    va = cols(2)
    va32_ref[...] = va
    _store_transposed(vta_ref, va)
    qb_ref[...] = (cols(3) * Q_SCALE).astype(BF16)
    kb = cols(4)
    kb16_ref[...] = kb.astype(BF16)
    vb = cols(5)
    _store_transposed(vtb_ref, vb)

    @pl.when(pl.program_id(0) % tail_period == tail_period - 1)
    def _():
        kb32_ref[...] = kb
        vb32_ref[...] = vb


def _project(x, g, w_bf, tail_period, time_major_k=False):
    n, d = x.shape
    tm = ROW_TILE
    blk = ATTN_BLOCK_K
    assert n % (tm * tail_period) == 0 and tm % blk == 0
    groups = WIDTH_A // LANES
    row = lambda i: (i, 0)
    tail = lambda i: (i // tail_period, 0)
    rows = lambda dt, imap=row: (jax.ShapeDtypeStruct((n, WIDTH_A), dt),
                                 pl.BlockSpec((tm, WIDTH_A), imap))
    vt = (jax.ShapeDtypeStruct((groups, n // blk, LANES, blk), BF16),
          pl.BlockSpec((groups, tm // blk, LANES, blk), lambda i: (0, i, 0, 0)))
    n_tail = n // tail_period
    tails = (jax.ShapeDtypeStruct((n_tail, WIDTH_B), F32), pl.BlockSpec((tm, WIDTH_B), tail))
    if time_major_k:
        per_batch = tail_period * tm
        ka32 = (jax.ShapeDtypeStruct((n // per_batch, N_HEADS_A, 2, HEAD_DIM, per_batch), F32),
                pl.BlockSpec((None, N_HEADS_A, 2, HEAD_DIM, tm),
                             lambda i: (i // tail_period, 0, 0, 0, i % tail_period)))
    else:
        ka32 = rows(F32)
    outs = [rows(BF16), ka32, rows(BF16), rows(F32), vt,
            rows(BF16), rows(BF16), vt, tails, tails]
    return pl.pallas_call(
        functools.partial(_proj_kernel, tail_period=tail_period),
        grid=(n // tm,),
        in_specs=[pl.BlockSpec((tm, d), row),
                  pl.BlockSpec((1, d), lambda i: (0, 0)),
                  pl.BlockSpec(w_bf.shape, lambda i: (0, 0), pipeline_mode=pl.Buffered(1))],
        out_specs=[s for _, s in outs],
        out_shape=[s for s, _ in outs],
        compiler_params=pltpu.CompilerParams(dimension_semantics=("arbitrary",),
                                             vmem_limit_bytes=VMEM_LIMIT),
        name="in_proj",
    )(x, g.reshape(1, d), w_bf)


def _attn_kernel(*refs, tq, tk, n_near, causal_walk, combine, lam_init):
    far_walk = causal_walk and combine == "diff"
    if combine == "diff":
        lam_ref, q_ref, k_ref, vt_ref, bias_ref, g_ref, o_ref, m_ref, l_ref, acc_ref = refs[:10]
    else:
        q_ref, k_ref, vt_ref, bias_ref, o_ref, m_ref, l_ref, acc_ref = refs[:8]
    qi = pl.program_id(2)

    q = q_ref[...]
    lane = lax.broadcasted_iota(jnp.int32, q.shape, 1)
    zero = jnp.zeros_like(q)
    qz = jnp.concatenate([jnp.where(lane < HEAD_DIM, q, zero),
                          jnp.where(lane >= HEAD_DIM, q, zero)], axis=0)

    def scores(kb):
        start = kb * tk if isinstance(kb, int) else pl.multiple_of(kb * tk, tk)
        k = k_ref[pl.ds(start, tk), :]
        return lax.dot_general(k, qz, (((1,), (1,)), ((), ())),
                               preferred_element_type=F32)

    if not causal_walk:
        m = jnp.full(m_ref.shape, MASKED, F32)
        l = jnp.zeros(l_ref.shape, F32)
        acc = jnp.zeros(acc_ref.shape, F32)
        for t in range(n_near):
            s = scores(t) + bias_ref[t]
            m_new = jnp.maximum(m, jnp.max(s, axis=0, keepdims=True))
            alpha = jnp.exp2(m - m_new)
            p = jnp.exp2(s - m_new)
            l = alpha * l + jnp.sum(p, axis=0, keepdims=True)
            acc = alpha * acc + jnp.dot(vt_ref[t], p.astype(BF16), preferred_element_type=F32)
            m = m_new
        m_ref[...] = m
        l_ref[...] = l
        acc_ref[...] = acc
    else:
        sbuf_ref, gmax_ref = refs[-2:]
        group = GROUP_TILES
        half = group // GROUP_HALVES
        n_visible = (tq // tk) * (qi + 1)
        n_tiles = n_visible if far_walk else jnp.minimum(n_visible, n_near)

        def key_tile(n):
            return jnp.maximum(n_visible - 1 - n, 0)

        def store_scores(g, t, hmax, masked):
            n = g * group + t
            s = scores(key_tile(n))
            if isinstance(g, int) and n < n_near:
                s = s + bias_ref[n]
            if masked:
                s = jnp.where(n < n_tiles, s, MASKED)
            sbuf_ref[t * tk:(t + 1) * tk, :] = s
            mt = jnp.max(s, axis=0, keepdims=True)
            return mt if hmax is None else jnp.maximum(hmax, mt)

        def group_step(g, produce):
            m = m_ref[...]
            l = l_ref[...]
            acc = acc_ref[...]
            for h in range(GROUP_HALVES):
                m_new = jnp.maximum(m, gmax_ref[h])
                alpha = jnp.exp2(m - m_new)
                l_sum = None
                pv = None
                hmax = None
                for t in range(h * half, (h + 1) * half):
                    s = sbuf_ref[t * tk:(t + 1) * tk, :]
                    if produce is not None:
                        hmax = store_scores(g + 1, t, hmax, produce == "masked")
                    p = jnp.exp2(s - m_new)
                    lt = jnp.sum(p, axis=0, keepdims=True)
                    d = jnp.dot(vt_ref[key_tile(g * group + t)], p.astype(BF16),
                                preferred_element_type=F32)
                    l_sum = lt if l_sum is None else l_sum + lt
                    pv = d if pv is None else pv + d
                if produce is not None:
                    gmax_ref[h] = hmax
                l = alpha * l + l_sum
                acc = alpha * acc + pv
                m = m_new
            l_ref[...] = l
            acc_ref[...] = acc
            m_ref[...] = m

        m_ref[...] = jnp.full(m_ref.shape, MASKED, F32)
        l_ref[...] = jnp.zeros(l_ref.shape, F32)
        acc_ref[...] = jnp.zeros(acc_ref.shape, F32)
        for h in range(GROUP_HALVES):
            hmax = None
            for t in range(h * half, (h + 1) * half):
                hmax = store_scores(0, t, hmax, True)
            gmax_ref[h] = hmax

        if far_walk:
            n_groups = (n_tiles + group - 1) // group
            n_plain = jnp.maximum(n_groups - 2, 0)

            def pair(i, carry):
                group_step(2 * i, "plain")
                group_step(2 * i + 1, "plain")
                return carry

            lax.fori_loop(0, n_plain // 2, pair, 0)

            @pl.when(n_plain % 2 == 1)
            def _():
                group_step(n_plain - 1, "plain")

            @pl.when(n_groups >= 2)
            def _():
                group_step(n_groups - 2, "masked")

            group_step(n_groups - 1, None)
        else:
            assert n_near <= group
            group_step(0, None)

    o = acc_ref[...] / l_ref[...]
    if combine == "diff":
        d = o[:, :tq] - lam_ref[0] * o[:, tq:]
        y = d * lax.rsqrt(jnp.mean(d * d, axis=0, keepdims=True) + SUBLN_EPS)
        o_ref[...] = ((y.T * g_ref[...]) * (1.0 - lam_init)).astype(o_ref.dtype)
    else:
        row = lax.broadcasted_iota(jnp.int32, (LANES, tq), 0)
        o_ref[...] = jnp.where(row < HEAD_DIM, o[:, :tq], o[:, tq:]).T.astype(o_ref.dtype)


def _attention(q, k, vt, bias, *, tq, tk, causal_walk, combine, out_dtype,
               lam=None, g=None, lam_init=0.0):
    b, tq_total, width = q.shape
    tk_total = k.shape[1]
    groups = width // LANES
    n_near = bias.shape[0]
    nkb = tk_total // tk
    assert tq_total % tq == 0 and vt.shape == (groups, b * nkb, LANES, tk)
    in_specs = [
        pl.BlockSpec((None, tq, LANES), lambda bi, h, qi: (bi, qi, h)),
        pl.BlockSpec((None, tk_total, LANES), lambda bi, h, qi: (bi, 0, h)),
        pl.BlockSpec((None, nkb, LANES, tk), lambda bi, h, qi: (h, bi, 0, 0)),
        pl.BlockSpec((n_near, None, tk, 2 * tq), lambda bi, h, qi: (0, h, 0, 0)),
    ]
    args = [q, k, vt, bias]
    if combine == "diff":
        in_specs = [pl.BlockSpec(memory_space=pltpu.SMEM)] + in_specs
        in_specs.append(pl.BlockSpec((1, LANES), lambda bi, h, qi: (0, 0)))
        args = [lam.reshape(1).astype(F32)] + args + [g.reshape(1, LANES).astype(F32)]
    scratch = [pltpu.VMEM((1, 2 * tq), F32), pltpu.VMEM((1, 2 * tq), F32),
               pltpu.VMEM((LANES, 2 * tq), F32)]
    if causal_walk:
        scratch.append(pltpu.VMEM((GROUP_TILES * tk, 2 * tq), F32))
        scratch.append(pltpu.VMEM((GROUP_HALVES, 1, 2 * tq), F32))
    return pl.pallas_call(
        functools.partial(_attn_kernel, tq=tq, tk=tk, n_near=n_near, causal_walk=causal_walk,
                          combine=combine, lam_init=lam_init),
        grid=(b, groups, tq_total // tq),
        in_specs=in_specs,
        out_specs=pl.BlockSpec((None, tq, LANES), lambda bi, h, qi: (bi, qi, h)),
        out_shape=jax.ShapeDtypeStruct((b, tq_total, width), out_dtype),
        scratch_shapes=scratch,
        compiler_params=pltpu.CompilerParams(
            dimension_semantics=("arbitrary", "arbitrary", "arbitrary"),
            vmem_limit_bytes=VMEM_LIMIT),
        name="attn_" + combine + ("_walk" if causal_walk else "_flat"),
    )(*args)


def _transpose_values(v, tk):
    b, t, width = v.shape
    return v.reshape(b * (t // tk), tk, width // LANES, LANES).transpose(2, 0, 3, 1)


def _rms(x, g, eps):
    return x * lax.rsqrt(jnp.mean(x * x, axis=-1, keepdims=True) + eps) * g


def _post_kernel(x_ref, oa_ref, ob_ref, p_ref, gb_ref, gm_ref, gf_ref, wout_ref, wup_ref,
                 wdown_ref, wgate_ref, wple_ref, y_ref):
    obn = _rms(ob_ref[...], gb_ref[...], EPS).astype(BF16)
    h = x_ref[...] + (
        jnp.dot(oa_ref[...], wout_ref[:WIDTH_A, :], preferred_element_type=F32)
        + jnp.dot(obn, wout_ref[WIDTH_A:, :], preferred_element_type=F32))
    c = _rms(h, gm_ref[...], EPS).astype(BF16)
    d_ff = wup_ref.shape[1]
    step = wup_ref.shape[0]
    for j in range(d_ff // step):
        u = jnp.dot(c, wup_ref[:, j * step:(j + 1) * step], preferred_element_type=F32)
        u = jnp.square(jnp.maximum(u, 0.0)).astype(BF16)
        h = h + jnp.dot(u, wdown_ref[j * step:(j + 1) * step, :], preferred_element_type=F32)
    gate = jax.nn.sigmoid(jnp.dot(h.astype(BF16), wgate_ref[...], preferred_element_type=F32))
    h = h + gate * jnp.dot(p_ref[...].astype(BF16), wple_ref[...], preferred_element_type=F32)
    y_ref[...] = _rms(h, gf_ref[...], EPS)


def _post(x, oa, ob, p, gb, gm, gf, wout, wup, wdown, wgate, wple):
    n, d = x.shape
    tm = min(ROW_TILE, n)
    assert n % tm == 0
    row = lambda i: (i, 0)
    fixed = lambda a: pl.BlockSpec(a.shape, lambda i: (0,) * a.ndim, pipeline_mode=pl.Buffered(1))
    vec = lambda a: a.reshape(1, -1).astype(F32)
    gb, gm, gf = vec(gb), vec(gm), vec(gf)
    return pl.pallas_call(
        _post_kernel,
        grid=(n // tm,),
        in_specs=[pl.BlockSpec((tm, d), row), pl.BlockSpec((tm, oa.shape[1]), row),
                  pl.BlockSpec((tm, ob.shape[1]), row), pl.BlockSpec((tm, p.shape[1]), row),
                  fixed(gb), fixed(gm), fixed(gf), fixed(wout), fixed(wup), fixed(wdown),
                  fixed(wgate), fixed(wple)],
        out_specs=pl.BlockSpec((tm, d), row),
        out_shape=jax.ShapeDtypeStruct((n, d), F32),
        compiler_params=pltpu.CompilerParams(dimension_semantics=("arbitrary",),
                                             vmem_limit_bytes=VMEM_LIMIT),
        name="post",
    )(x, oa, ob, p, gb, gm, gf, wout, wup, wdown, wgate, wple)


def _pad_rows(a, rows):
    return jnp.pad(a, ((0, 0), (0, rows - a.shape[1]), (0, 0)))


def _sample_bias(stream_vec_fn, n_keys, tk, tq, first_kpos, past, band):
    def visible(t, a, b):
        kpos = first_kpos + a
        kc = kpos // CHUNK
        qc = (past + b) // CHUNK
        vis = (kc <= qc) & (a < n_keys)
        if band:
            vis = vis & (kc >= qc - BAND_CHUNKS) & (kpos >= 0)
        return vis

    return _bias_tiles(stream_vec_fn, tk, tq, (first_kpos - past,), visible)


def kernel(x_prompt, x_sample, cache_a_k, cache_a_v, cache_b_k, cache_b_v, p_prompt, p_sample,
           t5_table, g_attn, w_in, lambda_q1, lambda_k1, lambda_q2, lambda_k2, subln_g,
           band_table, out_norm_b, w_out, g_mlp, w_up, w_down, w_ple_gate, w_ple_proj, g_final):
    depth = w_in.shape[0]
    assert depth == 1
    i = 0
    B, S, D = x_prompt.shape
    Bs, Ts, _ = x_sample.shape
    past = cache_a_k.shape[2]
    cache_b_len = cache_b_k.shape[2]
    keep = min(WINDOW_B, S)
    assert keep == ROW_TILE and S % ATTN_BLOCK_Q == 0 and Bs * Ts == ROW_TILE

    lam_init = 0.8 - 0.6 * math.exp(-0.3 * i)
    lam = (jnp.exp(jnp.sum(lambda_q1[i].astype(F32) * lambda_k1[i].astype(F32)))
           - jnp.exp(jnp.sum(lambda_q2[i].astype(F32) * lambda_k2[i].astype(F32)))
           + lam_init)

    w_in_bf = w_in[i].astype(BF16)
    post_w = (out_norm_b[i], g_mlp[i], g_final, w_out[i].astype(BF16), w_up[i].astype(BF16),
              w_down[i].astype(BF16), w_ple_gate[i].astype(BF16), w_ple_proj[i].astype(BF16))
    tk_p, tq_p = ATTN_BLOCK_K, ATTN_BLOCK_Q

    xp = x_prompt.reshape(B * S, D)
    (qa, ka32, ka16, va32, vta, qb, kb16, vtb, kb32, vb32) = _project(
        xp, g_attn[i], w_in_bf, tail_period=S // ROW_TILE, time_major_k=True)
    to3 = lambda a: a.reshape(B, S, a.shape[-1])
    oa = _attention(to3(qa), to3(ka16), vta, _diff_prompt_bias(t5_table), tq=tq_p, tk=tk_p,
                    causal_walk=True, combine="diff", out_dtype=BF16, lam=lam, g=subln_g[i],
                    lam_init=lam_init)
    ob = _attention(to3(qb), to3(kb16), vtb, _band_prompt_bias(band_table[i]), tq=tq_p, tk=tk_p,
                    causal_walk=True, combine="band", out_dtype=F32)
    y_prompt = _post(xp, oa.reshape(B * S, WIDTH_A), ob.reshape(B * S, WIDTH_B),
                     p_prompt[i].reshape(B * S, -1), *post_w).reshape(B, S, D)

    xs = x_sample.reshape(Bs * Ts, D)
    (qa_s, ka32_s, ka16_s, va32_s, _, qb_s, kb16_s, _, kb32_s, vb32_s) = _project(
        xs, g_attn[i], w_in_bf, tail_period=1)
    to3s = lambda a: a.reshape(Bs, Ts, a.shape[-1])
    tq_s = LANES
    assert Ts <= tq_s

    def sample_keys(cache, new):
        c = cache.reshape(Bs, cache.shape[1], -1).astype(BF16)
        allk = jnp.concatenate([c, to3s(new).astype(BF16)], axis=1)
        n_keys = allk.shape[1]
        tk = -(-n_keys // LANES) * LANES
        return _pad_rows(allk, tk), n_keys, tk

    ka_all, n_a, tk_a = sample_keys(cache_a_k[i], ka16_s)
    va_all, _, _ = sample_keys(cache_a_v[i], va32_s)
    kb_all, n_b, tk_b = sample_keys(cache_b_k[i], kb16_s)
    vb_all, _, _ = sample_keys(cache_b_v[i], vb32_s)

    bias_a = _sample_bias(_diff_streams(t5_table), n_a, tk_a, tq_s, 0, past, band=False)
    bias_b = _sample_bias(_band_streams(band_table[i]), n_b, tk_b, tq_s,
                          past - cache_b_len, past, band=True)

    oa_s = _attention(_pad_rows(to3s(qa_s), tq_s), ka_all, _transpose_values(va_all, tk_a),
                      bias_a, tq=tq_s, tk=tk_a, causal_walk=False, combine="diff",
                      out_dtype=BF16, lam=lam, g=subln_g[i], lam_init=lam_init)[:, :Ts]
    ob_s = _attention(_pad_rows(to3s(qb_s), tq_s), kb_all, _transpose_values(vb_all, tk_b),
                      bias_b, tq=tq_s, tk=tk_b, causal_walk=False, combine="band",
                      out_dtype=F32)[:, :Ts]
    y_sample = _post(xs, oa_s.reshape(Bs * Ts, WIDTH_A), ob_s.reshape(Bs * Ts, WIDTH_B),
                     p_sample[i].reshape(Bs * Ts, -1), *post_w).reshape(Bs, Ts, D)

    return (y_prompt, y_sample,
            ka32.transpose(0, 4, 1, 2, 3)[None],
            va32.reshape(1, B, S, N_HEADS_A, 2 * HEAD_DIM),
            kb32.reshape(1, B, keep, N_HEADS_B, HEAD_DIM),
            vb32.reshape(1, B, keep, N_HEADS_B, HEAD_DIM),
            ka32_s.reshape(1, Bs, Ts, N_HEADS_A, 2, HEAD_DIM),
            va32_s.reshape(1, Bs, Ts, N_HEADS_A, 2 * HEAD_DIM),
            kb32_s.reshape(1, Bs, Ts, N_HEADS_B, HEAD_DIM),
            vb32_s.reshape(1, Bs, Ts, N_HEADS_B, HEAD_DIM))
```

```python
import functools
import math

import jax
import jax.numpy as jnp
from jax import lax
from jax.experimental import pallas as pl
from jax.experimental.pallas import tpu as pltpu

CHUNK = 64
HEAD_DIM = 64
N_HEADS_A = 4
N_HEADS_B = 8
WIDTH_A = N_HEADS_A * 2 * HEAD_DIM
WIDTH_B = N_HEADS_B * HEAD_DIM
N_BUCKETS = 32
MAX_DISTANCE = 128
BAND_CHUNKS = 8
WINDOW_B = BAND_CHUNKS * CHUNK
REL_CLIP = 128
EPS = 1e-6
SUBLN_EPS = 1e-5
MASKED = -1e30
LOG2E = math.log2(math.e)
Q_SCALE = LOG2E * HEAD_DIM ** -0.5

LANES = 128
VT_ROWS = LANES + 16
ATTN_BLOCK_K = 256
ATTN_BLOCK_Q = 512
GROUP_TILES = 4
GROUP_HALVES = 2
ROW_TILE = 512
VMEM_LIMIT = 56 * 1024 * 1024

F32 = jnp.float32
BF16 = jnp.bfloat16


def _t5_bucket(rel):
    half = N_BUCKETS // 2
    n = -rel
    ret = jnp.where(n < 0, half, 0)
    n = jnp.abs(n)
    max_exact = half // 2
    nf = jnp.maximum(n, 1).astype(F32)
    large = max_exact + (jnp.log(nf / max_exact) / math.log(MAX_DISTANCE / max_exact)
                         * (half - max_exact)).astype(jnp.int32)
    large = jnp.minimum(large, half - 1)
    return ret + jnp.where(n < max_exact, n, large)


def _bias_kernel(v_ref, o_ref, *, tk, tq, lane_starts, visible):
    a = lax.broadcasted_iota(jnp.int32, (tk, tq), 0)
    b = lax.broadcasted_iota(jnp.int32, (tk, tq), 1)
    for s in range(2):
        x = jnp.broadcast_to(v_ref[s], (tk, v_ref.shape[-1]))
        x = pltpu.roll(x, 0, 1, stride=1, stride_axis=0)
        for t, start in enumerate(lane_starts):
            tile = x[:, start:start + tq]
            vis = visible(t, a, b)
            o_ref[t, :, s * tq:(s + 1) * tq] = tile if vis is None else jnp.where(vis, tile, MASKED)


def _bias_tiles(stream_vec_fn, tk, tq, offsets, visible):
    first_start = -(-tk // LANES) * LANES
    r0 = first_start + max(offsets)
    lane_starts = tuple(r0 - off for off in offsets)
    assert all(s % LANES == 0 for s in lane_starts) and tq % LANES == 0
    n_lanes = max(lane_starts) + tq
    v = stream_vec_fn(r0 - jnp.arange(n_lanes)).astype(F32) * LOG2E
    groups = v.shape[0]
    return pl.pallas_call(
        functools.partial(_bias_kernel, tk=tk, tq=tq, lane_starts=lane_starts, visible=visible),
        grid=(groups,),
        in_specs=[pl.BlockSpec((None, 2, 1, n_lanes), lambda g: (g, 0, 0, 0))],
        out_specs=pl.BlockSpec((len(offsets), None, tk, 2 * tq), lambda g: (0, g, 0, 0)),
        out_shape=jax.ShapeDtypeStruct((len(offsets), groups, tk, 2 * tq), F32),
        compiler_params=pltpu.CompilerParams(dimension_semantics=("arbitrary",),
                                             vmem_limit_bytes=VMEM_LIMIT),
        name="bias_tiles",
    )(v.reshape(groups, 2, 1, n_lanes))


def _diff_streams(table, shift=None):
    def fn(rel):
        v = jnp.transpose(table[_t5_bucket(rel)].astype(F32))
        if shift is not None:
            v = v - shift[:, None]
        return jnp.stack([v, v], axis=1)
    return fn


def _band_streams(table):
    def fn(rel):
        v = jnp.transpose(table[jnp.clip(rel, -REL_CLIP, REL_CLIP) + REL_CLIP].astype(F32))
        return v.reshape(N_HEADS_B // 2, 2, -1)
    return fn


def _diff_prompt_bias(t5_table):
    tk, tq = ATTN_BLOCK_K, ATTN_BLOCK_Q
    assert tk >= MAX_DISTANCE
    far = t5_table[_t5_bucket(jnp.array(-2 * MAX_DISTANCE))].astype(F32)
    offsets = _near_offsets(tk, tq, reach=tk)
    visible = lambda t, a, b: None if offsets[t] < 0 else (a + offsets[t]) // CHUNK <= b // CHUNK
    return _bias_tiles(_diff_streams(t5_table, far), tk, tq, offsets, visible)


def _band_prompt_bias(band_table):
    tk, tq = ATTN_BLOCK_K, ATTN_BLOCK_Q
    offsets = _near_offsets(tk, tq, reach=WINDOW_B)

    def visible(t, a, b):
        kc = (a + offsets[t]) // CHUNK
        qc = b // CHUNK
        return (kc <= qc) & (kc >= qc - BAND_CHUNKS)

    return _bias_tiles(_band_streams(band_table), tk, tq, offsets, visible)


def _near_offsets(tk, tq, reach):
    assert tq % tk == 0 and reach % tk == 0
    return tuple(range(tq - tk, -reach - 1, -tk))


def _store_transposed(vt_ref, v):
    blk = vt_ref.shape[-1]
    ones_rows = _ones_rows(blk)
    for h in range(vt_ref.shape[0]):
        for c in range(vt_ref.shape[1]):
            vt_ref[h, c, :LANES] = v[c * blk:(c + 1) * blk, h * LANES:(h + 1) * LANES].T.astype(BF16)
            vt_ref[h, c, LANES:] = ones_rows


def _ones_rows(n):
    row = lax.broadcasted_iota(jnp.int32, (VT_ROWS - LANES, n), 0)
    return jnp.where(row == 0, 1.0, 0.0).astype(BF16)


def _proj_kernel(x_ref, g_ref, w_ref, qa_ref, ka32_ref, ka16_ref, va32_ref, vta_ref,
                 qb_ref, kb16_ref, vtb_ref, kb32_ref, vb32_ref, *, tail_period):
    x = x_ref[...]
    xn = x * lax.rsqrt(jnp.mean(x * x, axis=-1, keepdims=True) + EPS) * g_ref[...]
    xn = xn.astype(BF16)

    def cols(c):
        return jnp.dot(xn, w_ref[:, c * WIDTH_A:(c + 1) * WIDTH_A], preferred_element_type=F32)

    qa_ref[...] = (cols(0) * Q_SCALE).astype(BF16)
    ka = cols(1)
    if len(ka32_ref.shape) == 2:
        ka32_ref[...] = ka
    else:
        for h in range(N_HEADS_A):
            t = ka[:, h * LANES:(h + 1) * LANES].T
            ka32_ref[h, 0] = t[:HEAD_DIM]
            ka32_ref[h, 1] = t[HEAD_DIM:]
    ka16_ref[...] = ka.astype(BF16)
    va = cols(2)
    va32_ref[...] = va
    _store_transposed(vta_ref, va)
    qb_ref[...] = (cols(3) * Q_SCALE).astype(BF16)
    kb = cols(4)
    kb16_ref[...] = kb.astype(BF16)
    vb = cols(5)
    _store_transposed(vtb_ref, vb)

    @pl.when(pl.program_id(0) % tail_period == tail_period - 1)
    def _():
        kb32_ref[...] = kb
        vb32_ref[...] = vb


def _project(x, g, w_bf, tail_period, time_major_k=False):
    n, d = x.shape
    tm = ROW_TILE
    blk = ATTN_BLOCK_K
    assert n % (tm * tail_period) == 0 and tm % blk == 0
    groups = WIDTH_A // LANES
    row = lambda i: (i, 0)
    tail = lambda i: (i // tail_period, 0)
    rows = lambda dt, imap=row: (jax.ShapeDtypeStruct((n, WIDTH_A), dt),
                                 pl.BlockSpec((tm, WIDTH_A), imap))
    vt = (jax.ShapeDtypeStruct((groups, n // blk, VT_ROWS, blk), BF16),
          pl.BlockSpec((groups, tm // blk, VT_ROWS, blk), lambda i: (0, i, 0, 0)))
    n_tail = n // tail_period
    tails = (jax.ShapeDtypeStruct((n_tail, WIDTH_B), F32), pl.BlockSpec((tm, WIDTH_B), tail))
    if time_major_k:
        per_batch = tail_period * tm
        ka32 = (jax.ShapeDtypeStruct((n // per_batch, N_HEADS_A, 2, HEAD_DIM, per_batch), F32),
                pl.BlockSpec((None, N_HEADS_A, 2, HEAD_DIM, tm),
                             lambda i: (i // tail_period, 0, 0, 0, i % tail_period)))
    else:
        ka32 = rows(F32)
    outs = [rows(BF16), ka32, rows(BF16), rows(F32), vt,
            rows(BF16), rows(BF16), vt, tails, tails]
    return pl.pallas_call(
        functools.partial(_proj_kernel, tail_period=tail_period),
        grid=(n // tm,),
        in_specs=[pl.BlockSpec((tm, d), row),
                  pl.BlockSpec((1, d), lambda i: (0, 0)),
                  pl.BlockSpec(w_bf.shape, lambda i: (0, 0), pipeline_mode=pl.Buffered(1))],
        out_specs=[s for _, s in outs],
        out_shape=[s for s, _ in outs],
        compiler_params=pltpu.CompilerParams(dimension_semantics=("arbitrary",),
                                             vmem_limit_bytes=VMEM_LIMIT),
        name="in_proj",
    )(x, g.reshape(1, d), w_bf)


def _attn_kernel(*refs, tq, tk, n_near, causal_walk, combine, lam_init):
    far_walk = causal_walk and combine == "diff"
    if combine == "diff":
        lam_ref, q_ref, k_ref, vt_ref, bias_ref, g_ref, o_ref, m_ref, acc_ref = refs[:9]
    else:
        q_ref, k_ref, vt_ref, bias_ref, o_ref, m_ref, acc_ref = refs[:7]
    qi = pl.program_id(2)

    q = q_ref[...]
    lane = lax.broadcasted_iota(jnp.int32, q.shape, 1)
    zero = jnp.zeros_like(q)
    qz = jnp.concatenate([jnp.where(lane < HEAD_DIM, q, zero),
                          jnp.where(lane >= HEAD_DIM, q, zero)], axis=0)

    def scores(kb):
        start = kb * tk if isinstance(kb, int) else pl.multiple_of(kb * tk, tk)
        k = k_ref[pl.ds(start, tk), :]
        return lax.dot_general(k, qz, (((1,), (1,)), ((), ())),
                               preferred_element_type=F32)

    if not causal_walk:
        m = jnp.full(m_ref.shape, MASKED, F32)
        acc = jnp.zeros(acc_ref.shape, F32)
        for t in range(n_near):
            s = scores(t) + bias_ref[t]
            m_new = jnp.maximum(m, jnp.max(s, axis=0, keepdims=True))
            alpha = jnp.exp2(m - m_new)
            p = jnp.exp2(s - m_new)
            acc = alpha * acc + jnp.dot(vt_ref[t], p.astype(BF16), preferred_element_type=F32)
            m = m_new
        m_ref[...] = m
        acc_ref[...] = acc
    else:
        sbuf_ref, gmax_ref = refs[-2:]
        group = GROUP_TILES
        half = group // GROUP_HALVES
        n_visible = (tq // tk) * (qi + 1)
        n_tiles = n_visible if far_walk else jnp.minimum(n_visible, n_near)

        def key_tile(n):
            return jnp.maximum(n_visible - 1 - n, 0)

        def store_scores(g, t, hmax, masked):
            n = g * group + t
            s = scores(key_tile(n))
            if isinstance(g, int) and n < n_near:
                s = s + bias_ref[n]
            if masked:
                s = jnp.where(n < n_tiles, s, MASKED)
            sbuf_ref[t * tk:(t + 1) * tk, :] = s
            mt = jnp.max(s, axis=0, keepdims=True)
            return mt if hmax is None else jnp.maximum(hmax, mt)

        def group_step(g, produce):
            m = m_ref[...]
            acc = acc_ref[...]
            for h in range(GROUP_HALVES):
                m_new = jnp.maximum(m, gmax_ref[h])
                alpha = jnp.exp2(m - m_new)
                pv = None
                hmax = None
                for t in range(h * half, (h + 1) * half):
                    s = sbuf_ref[t * tk:(t + 1) * tk, :]
                    if produce is not None:
                        hmax = store_scores(g + 1, t, hmax, produce == "masked")
                    p = jnp.exp2(s - m_new)
                    d = jnp.dot(vt_ref[key_tile(g * group + t)], p.astype(BF16),
                                preferred_element_type=F32)
                    pv = d if pv is None else pv + d
                if produce is not None:
                    gmax_ref[h] = hmax
                acc = alpha * acc + pv
                m = m_new
            acc_ref[...] = acc
            m_ref[...] = m

        m_ref[...] = jnp.full(m_ref.shape, MASKED, F32)
        acc_ref[...] = jnp.zeros(acc_ref.shape, F32)
        for h in range(GROUP_HALVES):
            hmax = None
            for t in range(h * half, (h + 1) * half):
                hmax = store_scores(0, t, hmax, True)
            gmax_ref[h] = hmax

        if far_walk:
            n_groups = (n_tiles + group - 1) // group
            n_plain = jnp.maximum(n_groups - 2, 0)

            def pair(i, carry):
                group_step(2 * i, "plain")
                group_step(2 * i + 1, "plain")
                return carry

            lax.fori_loop(0, n_plain // 2, pair, 0)

            @pl.when(n_plain % 2 == 1)
            def _():
                group_step(n_plain - 1, "plain")

            @pl.when(n_groups >= 2)
            def _():
                group_step(n_groups - 2, "masked")

            group_step(n_groups - 1, None)
        else:
            assert n_near <= group
            group_step(0, None)

    o = acc_ref[:LANES, :] / acc_ref[LANES:LANES + 1, :]
    if combine == "diff":
        d = o[:, :tq] - lam_ref[0] * o[:, tq:]
        y = d * lax.rsqrt(jnp.mean(d * d, axis=0, keepdims=True) + SUBLN_EPS)
        o_ref[...] = ((y.T * g_ref[...]) * (1.0 - lam_init)).astype(o_ref.dtype)
    else:
        row = lax.broadcasted_iota(jnp.int32, (LANES, tq), 0)
        o_ref[...] = jnp.where(row < HEAD_DIM, o[:, :tq], o[:, tq:]).T.astype(o_ref.dtype)


def _attention(q, k, vt, bias, *, tq, tk, causal_walk, combine, out_dtype,
               lam=None, g=None, lam_init=0.0):
    b, tq_total, width = q.shape
    tk_total = k.shape[1]
    groups = width // LANES
    n_near = bias.shape[0]
    nkb = tk_total // tk
    assert tq_total % tq == 0 and vt.shape == (groups, b * nkb, VT_ROWS, tk)
    in_specs = [
        pl.BlockSpec((None, tq, LANES), lambda bi, h, qi: (bi, qi, h)),
        pl.BlockSpec((None, tk_total, LANES), lambda bi, h, qi: (bi, 0, h)),
        pl.BlockSpec((None, nkb, VT_ROWS, tk), lambda bi, h, qi: (h, bi, 0, 0)),
        pl.BlockSpec((n_near, None, tk, 2 * tq), lambda bi, h, qi: (0, h, 0, 0)),
    ]
    args = [q, k, vt, bias]
    if combine == "diff":
        in_specs = [pl.BlockSpec(memory_space=pltpu.SMEM)] + in_specs
        in_specs.append(pl.BlockSpec((1, LANES), lambda bi, h, qi: (0, 0)))
        args = [lam.reshape(1).astype(F32)] + args + [g.reshape(1, LANES).astype(F32)]
    scratch = [pltpu.VMEM((1, 2 * tq), F32), pltpu.VMEM((VT_ROWS, 2 * tq), F32)]
    if causal_walk:
        scratch.append(pltpu.VMEM((GROUP_TILES * tk, 2 * tq), F32))
        scratch.append(pltpu.VMEM((GROUP_HALVES, 1, 2 * tq), F32))
    return pl.pallas_call(
        functools.partial(_attn_kernel, tq=tq, tk=tk, n_near=n_near, causal_walk=causal_walk,
                          combine=combine, lam_init=lam_init),
        grid=(b, groups, tq_total // tq),
        in_specs=in_specs,
        out_specs=pl.BlockSpec((None, tq, LANES), lambda bi, h, qi: (bi, qi, h)),
        out_shape=jax.ShapeDtypeStruct((b, tq_total, width), out_dtype),
        scratch_shapes=scratch,
        compiler_params=pltpu.CompilerParams(
            dimension_semantics=("arbitrary", "arbitrary", "arbitrary"),
            vmem_limit_bytes=VMEM_LIMIT),
        name="attn_" + combine + ("_walk" if causal_walk else "_flat"),
    )(*args)


def _transpose_values(v, tk):
    b, t, width = v.shape
    vt = v.reshape(b * (t // tk), tk, width // LANES, LANES).transpose(2, 0, 3, 1)
    ones = jnp.broadcast_to(_ones_rows(tk), vt.shape[:2] + (VT_ROWS - LANES, tk))
    return jnp.concatenate([vt, ones], axis=2)


def _rms(x, g, eps):
    return x * lax.rsqrt(jnp.mean(x * x, axis=-1, keepdims=True) + eps) * g


def _post_kernel(x_ref, oa_ref, ob_ref, p_ref, gb_ref, gm_ref, gf_ref, wout_ref, wup_ref,
                 wdown_ref, wgate_ref, wple_ref, y_ref):
    obn = _rms(ob_ref[...], gb_ref[...], EPS).astype(BF16)
    h = x_ref[...] + (
        jnp.dot(oa_ref[...], wout_ref[:WIDTH_A, :], preferred_element_type=F32)
        + jnp.dot(obn, wout_ref[WIDTH_A:, :], preferred_element_type=F32))
    c = _rms(h, gm_ref[...], EPS).astype(BF16)
    d_ff = wup_ref.shape[1]
    step = wup_ref.shape[0]
    for j in range(d_ff // step):
        u = jnp.dot(c, wup_ref[:, j * step:(j + 1) * step], preferred_element_type=F32)
        u = jnp.square(jnp.maximum(u, 0.0)).astype(BF16)
        h = h + jnp.dot(u, wdown_ref[j * step:(j + 1) * step, :], preferred_element_type=F32)
    gate = jax.nn.sigmoid(jnp.dot(h.astype(BF16), wgate_ref[...], preferred_element_type=F32))
    h = h + gate * jnp.dot(p_ref[...].astype(BF16), wple_ref[...], preferred_element_type=F32)
    y_ref[...] = _rms(h, gf_ref[...], EPS)


def _post(x, oa, ob, p, gb, gm, gf, wout, wup, wdown, wgate, wple):
    n, d = x.shape
    tm = min(ROW_TILE, n)
    assert n % tm == 0
    row = lambda i: (i, 0)
    fixed = lambda a: pl.BlockSpec(a.shape, lambda i: (0,) * a.ndim, pipeline_mode=pl.Buffered(1))
    vec = lambda a: a.reshape(1, -1).astype(F32)
    gb, gm, gf = vec(gb), vec(gm), vec(gf)
    return pl.pallas_call(
        _post_kernel,
        grid=(n // tm,),
        in_specs=[pl.BlockSpec((tm, d), row), pl.BlockSpec((tm, oa.shape[1]), row),
                  pl.BlockSpec((tm, ob.shape[1]), row), pl.BlockSpec((tm, p.shape[1]), row),
                  fixed(gb), fixed(gm), fixed(gf), fixed(wout), fixed(wup), fixed(wdown),
                  fixed(wgate), fixed(wple)],
        out_specs=pl.BlockSpec((tm, d), row),
        out_shape=jax.ShapeDtypeStruct((n, d), F32),
        compiler_params=pltpu.CompilerParams(dimension_semantics=("arbitrary",),
                                             vmem_limit_bytes=VMEM_LIMIT),
        name="post",
    )(x, oa, ob, p, gb, gm, gf, wout, wup, wdown, wgate, wple)


def _pad_rows(a, rows):
    return jnp.pad(a, ((0, 0), (0, rows - a.shape[1]), (0, 0)))


def _sample_bias(stream_vec_fn, n_keys, tk, tq, first_kpos, past, band):
    def visible(t, a, b):
        kpos = first_kpos + a
        kc = kpos // CHUNK
        qc = (past + b) // CHUNK
        vis = (kc <= qc) & (a < n_keys)
        if band:
            vis = vis & (kc >= qc - BAND_CHUNKS) & (kpos >= 0)
        return vis

    return _bias_tiles(stream_vec_fn, tk, tq, (first_kpos - past,), visible)


def kernel(x_prompt, x_sample, cache_a_k, cache_a_v, cache_b_k, cache_b_v, p_prompt, p_sample,
           t5_table, g_attn, w_in, lambda_q1, lambda_k1, lambda_q2, lambda_k2, subln_g,
           band_table, out_norm_b, w_out, g_mlp, w_up, w_down, w_ple_gate, w_ple_proj, g_final):
    depth = w_in.shape[0]
    assert depth == 1
    i = 0
    B, S, D = x_prompt.shape
    Bs, Ts, _ = x_sample.shape
    past = cache_a_k.shape[2]
    cache_b_len = cache_b_k.shape[2]
    keep = min(WINDOW_B, S)
    assert keep == ROW_TILE and S % ATTN_BLOCK_Q == 0 and Bs * Ts == ROW_TILE

    lam_init = 0.8 - 0.6 * math.exp(-0.3 * i)
    lam = (jnp.exp(jnp.sum(lambda_q1[i].astype(F32) * lambda_k1[i].astype(F32)))
           - jnp.exp(jnp.sum(lambda_q2[i].astype(F32) * lambda_k2[i].astype(F32)))
           + lam_init)

    w_in_bf = w_in[i].astype(BF16)
    post_w = (out_norm_b[i], g_mlp[i], g_final, w_out[i].astype(BF16), w_up[i].astype(BF16),
              w_down[i].astype(BF16), w_ple_gate[i].astype(BF16), w_ple_proj[i].astype(BF16))
    tk_p, tq_p = ATTN_BLOCK_K, ATTN_BLOCK_Q

    xp = x_prompt.reshape(B * S, D)
    (qa, ka32, ka16, va32, vta, qb, kb16, vtb, kb32, vb32) = _project(
        xp, g_attn[i], w_in_bf, tail_period=S // ROW_TILE, time_major_k=True)
    to3 = lambda a: a.reshape(B, S, a.shape[-1])
    oa = _attention(to3(qa), to3(ka16), vta, _diff_prompt_bias(t5_table), tq=tq_p, tk=tk_p,
                    causal_walk=True, combine="diff", out_dtype=BF16, lam=lam, g=subln_g[i],
                    lam_init=lam_init)
    ob = _attention(to3(qb), to3(kb16), vtb, _band_prompt_bias(band_table[i]), tq=tq_p, tk=tk_p,
                    causal_walk=True, combine="band", out_dtype=F32)
    y_prompt = _post(xp, oa.reshape(B * S, WIDTH_A), ob.reshape(B * S, WIDTH_B),
                     p_prompt[i].reshape(B * S, -1), *post_w).reshape(B, S, D)

    xs = x_sample.reshape(Bs * Ts, D)
    (qa_s, ka32_s, ka16_s, va32_s, _, qb_s, kb16_s, _, kb32_s, vb32_s) = _project(
        xs, g_attn[i], w_in_bf, tail_period=1)
    to3s = lambda a: a.reshape(Bs, Ts, a.shape[-1])
    tq_s = LANES
    assert Ts <= tq_s

    def sample_keys(cache, new):
        c = cache.reshape(Bs, cache.shape[1], -1).astype(BF16)
        allk = jnp.concatenate([c, to3s(new).astype(BF16)], axis=1)
        n_keys = allk.shape[1]
        tk = -(-n_keys // LANES) * LANES
        return _pad_rows(allk, tk), n_keys, tk

    ka_all, n_a, tk_a = sample_keys(cache_a_k[i], ka16_s)
    va_all, _, _ = sample_keys(cache_a_v[i], va32_s)
    kb_all, n_b, tk_b = sample_keys(cache_b_k[i], kb16_s)
    vb_all, _, _ = sample_keys(cache_b_v[i], vb32_s)

    bias_a = _sample_bias(_diff_streams(t5_table), n_a, tk_a, tq_s, 0, past, band=False)
    bias_b = _sample_bias(_band_streams(band_table[i]), n_b, tk_b, tq_s,
                          past - cache_b_len, past, band=True)

    oa_s = _attention(_pad_rows(to3s(qa_s), tq_s), ka_all, _transpose_values(va_all, tk_a),
                      bias_a, tq=tq_s, tk=tk_a, causal_walk=False, combine="diff",
                      out_dtype=BF16, lam=lam, g=subln_g[i], lam_init=lam_init)[:, :Ts]
    ob_s = _attention(_pad_rows(to3s(qb_s), tq_s), kb_all, _transpose_values(vb_all, tk_b),
                      bias_b, tq=tq_s, tk=tk_b, causal_walk=False, combine="band",
                      out_dtype=F32)[:, :Ts]
    y_sample = _post(xs, oa_s.reshape(Bs * Ts, WIDTH_A), ob_s.reshape(Bs * Ts, WIDTH_B),
                     p_sample[i].reshape(Bs * Ts, -1), *post_w).reshape(Bs, Ts, D)

    return (y_prompt, y_sample,
            ka32.transpose(0, 4, 1, 2, 3)[None],
            va32.reshape(1, B, S, N_HEADS_A, 2 * HEAD_DIM),
            kb32.reshape(1, B, keep, N_HEADS_B, HEAD_DIM),
            vb32.reshape(1, B, keep, N_HEADS_B, HEAD_DIM),
            ka32_s.reshape(1, Bs, Ts, N_HEADS_A, 2, HEAD_DIM),
            va32_s.reshape(1, Bs, Ts, N_HEADS_A, 2 * HEAD_DIM),
            kb32_s.reshape(1, Bs, Ts, N_HEADS_B, HEAD_DIM),
            vb32_s.reshape(1, Bs, Ts, N_HEADS_B, HEAD_DIM))
```

```python
import functools
import math

import jax
import jax.numpy as jnp
from jax import lax
from jax.experimental import pallas as pl
from jax.experimental.pallas import tpu as pltpu

CHUNK = 64
HEAD_DIM = 64
N_HEADS_A = 4
N_HEADS_B = 8
WIDTH_A = N_HEADS_A * 2 * HEAD_DIM
WIDTH_B = N_HEADS_B * HEAD_DIM
N_BUCKETS = 32
MAX_DISTANCE = 128
BAND_CHUNKS = 8
WINDOW_B = BAND_CHUNKS * CHUNK
REL_CLIP = 128
EPS = 1e-6
SUBLN_EPS = 1e-5
MASKED = -1e30
LOG2E = math.log2(math.e)
Q_SCALE = LOG2E * HEAD_DIM ** -0.5

LANES = 128
VT_ROWS = LANES + 16
ATTN_BLOCK_K = 256
ATTN_BLOCK_Q = 512
GROUP_TILES = 4
GROUP_HALVES = 2
ROW_TILE = 512
VMEM_LIMIT = 56 * 1024 * 1024

F32 = jnp.float32
BF16 = jnp.bfloat16


def _t5_bucket(rel):
    half = N_BUCKETS // 2
    n = -rel
    ret = jnp.where(n < 0, half, 0)
    n = jnp.abs(n)
    max_exact = half // 2
    nf = jnp.maximum(n, 1).astype(F32)
    large = max_exact + (jnp.log(nf / max_exact) / math.log(MAX_DISTANCE / max_exact)
                         * (half - max_exact)).astype(jnp.int32)
    large = jnp.minimum(large, half - 1)
    return ret + jnp.where(n < max_exact, n, large)


def _bias_kernel(v_ref, o_ref, *, tk, tq, lane_starts, visible):
    a = lax.broadcasted_iota(jnp.int32, (tk, tq), 0)
    b = lax.broadcasted_iota(jnp.int32, (tk, tq), 1)
    for s in range(2):
        x = jnp.broadcast_to(v_ref[s], (tk, v_ref.shape[-1]))
        x = pltpu.roll(x, 0, 1, stride=1, stride_axis=0)
        for t, start in enumerate(lane_starts):
            tile = x[:, start:start + tq]
            vis = visible(t, a, b)
            o_ref[t, :, s * tq:(s + 1) * tq] = tile if vis is None else jnp.where(vis, tile, MASKED)


def _bias_tiles(stream_vec_fn, tk, tq, offsets, visible):
    first_start = -(-tk // LANES) * LANES
    r0 = first_start + max(offsets)
    lane_starts = tuple(r0 - off for off in offsets)
    assert all(s % LANES == 0 for s in lane_starts) and tq % LANES == 0
    n_lanes = max(lane_starts) + tq
    v = stream_vec_fn(r0 - jnp.arange(n_lanes)).astype(F32) * LOG2E
    groups = v.shape[0]
    return pl.pallas_call(
        functools.partial(_bias_kernel, tk=tk, tq=tq, lane_starts=lane_starts, visible=visible),
        grid=(groups,),
        in_specs=[pl.BlockSpec((None, 2, 1, n_lanes), lambda g: (g, 0, 0, 0))],
        out_specs=pl.BlockSpec((len(offsets), None, tk, 2 * tq), lambda g: (0, g, 0, 0)),
        out_shape=jax.ShapeDtypeStruct((len(offsets), groups, tk, 2 * tq), F32),
        compiler_params=pltpu.CompilerParams(dimension_semantics=("arbitrary",),
                                             vmem_limit_bytes=VMEM_LIMIT),
        name="bias_tiles",
    )(v.reshape(groups, 2, 1, n_lanes))


def _diff_streams(table, shift=None):
    def fn(rel):
        v = jnp.transpose(table[_t5_bucket(rel)].astype(F32))
        if shift is not None:
            v = v - shift[:, None]
        return jnp.stack([v, v], axis=1)
    return fn


def _band_streams(table):
    def fn(rel):
        v = jnp.transpose(table[jnp.clip(rel, -REL_CLIP, REL_CLIP) + REL_CLIP].astype(F32))
        return v.reshape(N_HEADS_B // 2, 2, -1)
    return fn


def _diff_prompt_bias(t5_table):
    tk, tq = ATTN_BLOCK_K, ATTN_BLOCK_Q
    assert tk >= MAX_DISTANCE
    far = t5_table[_t5_bucket(jnp.array(-2 * MAX_DISTANCE))].astype(F32)
    offsets = _near_offsets(tk, tq, reach=tk)
    visible = lambda t, a, b: None if offsets[t] < 0 else (a + offsets[t]) // CHUNK <= b // CHUNK
    return _bias_tiles(_diff_streams(t5_table, far), tk, tq, offsets, visible)


def _band_prompt_bias(band_table):
    tk, tq = ATTN_BLOCK_K, ATTN_BLOCK_Q
    offsets = _near_offsets(tk, tq, reach=WINDOW_B)

    def visible(t, a, b):
        kc = (a + offsets[t]) // CHUNK
        qc = b // CHUNK
        return (kc <= qc) & (kc >= qc - BAND_CHUNKS)

    return _bias_tiles(_band_streams(band_table), tk, tq, offsets, visible)


def _near_offsets(tk, tq, reach):
    assert tq % tk == 0 and reach % tk == 0
    return tuple(range(tq - tk, -reach - 1, -tk))


def _store_transposed(vt_ref, v):
    blk = vt_ref.shape[-1]
    ones_rows = _ones_rows(blk)
    for h in range(vt_ref.shape[0]):
        for c in range(vt_ref.shape[1]):
            vt_ref[h, c, :LANES] = v[c * blk:(c + 1) * blk, h * LANES:(h + 1) * LANES].T.astype(BF16)
            vt_ref[h, c, LANES:] = ones_rows


def _ones_rows(n):
    row = lax.broadcasted_iota(jnp.int32, (VT_ROWS - LANES, n), 0)
    return jnp.where(row == 0, 1.0, 0.0).astype(BF16)


def _proj_kernel(x_ref, g_ref, w_ref, qa_ref, ka32_ref, ka16_ref, va32_ref, vta_ref,
                 qb_ref, kb16_ref, vtb_ref, kb32_ref, vb32_ref, *, tail_period):
    x = x_ref[...]
    xn = x * lax.rsqrt(jnp.mean(x * x, axis=-1, keepdims=True) + EPS) * g_ref[...]
    xn = xn.astype(BF16)

    def cols(c):
        return jnp.dot(xn, w_ref[:, c * WIDTH_A:(c + 1) * WIDTH_A], preferred_element_type=F32)

    qa_ref[...] = (cols(0) * Q_SCALE).astype(BF16)
    ka = cols(1)
    if len(ka32_ref.shape) == 2:
        ka32_ref[...] = ka
    else:
        for h in range(N_HEADS_A):
            t = ka[:, h * LANES:(h + 1) * LANES].T
            ka32_ref[h, 0] = t[:HEAD_DIM]
            ka32_ref[h, 1] = t[HEAD_DIM:]
    ka16_ref[...] = ka.astype(BF16)
    va = cols(2)
    va32_ref[...] = va
    _store_transposed(vta_ref, va)
    qb_ref[...] = (cols(3) * Q_SCALE).astype(BF16)
    kb = cols(4)
    kb16_ref[...] = kb.astype(BF16)
    vb = cols(5)
    _store_transposed(vtb_ref, vb)

    @pl.when(pl.program_id(0) % tail_period == tail_period - 1)
    def _():
        kb32_ref[...] = kb
        vb32_ref[...] = vb


def _project(x, g, w_bf, tail_period, time_major_k=False):
    n, d = x.shape
    tm = ROW_TILE
    blk = ATTN_BLOCK_K
    assert n % (tm * tail_period) == 0 and tm % blk == 0
    groups = WIDTH_A // LANES
    row = lambda i: (i, 0)
    tail = lambda i: (i // tail_period, 0)
    rows = lambda dt, imap=row: (jax.ShapeDtypeStruct((n, WIDTH_A), dt),
                                 pl.BlockSpec((tm, WIDTH_A), imap))
    vt = (jax.ShapeDtypeStruct((groups, n // blk, VT_ROWS, blk), BF16),
          pl.BlockSpec((groups, tm // blk, VT_ROWS, blk), lambda i: (0, i, 0, 0)))
    n_tail = n // tail_period
    tails = (jax.ShapeDtypeStruct((n_tail, WIDTH_B), F32), pl.BlockSpec((tm, WIDTH_B), tail))
    if time_major_k:
        per_batch = tail_period * tm
        ka32 = (jax.ShapeDtypeStruct((n // per_batch, N_HEADS_A, 2, HEAD_DIM, per_batch), F32),
                pl.BlockSpec((None, N_HEADS_A, 2, HEAD_DIM, tm),
                             lambda i: (i // tail_period, 0, 0, 0, i % tail_period)))
    else:
        ka32 = rows(F32)
    outs = [rows(BF16), ka32, rows(BF16), rows(F32), vt,
            rows(BF16), rows(BF16), vt, tails, tails]
    return pl.pallas_call(
        functools.partial(_proj_kernel, tail_period=tail_period),
        grid=(n // tm,),
        in_specs=[pl.BlockSpec((tm, d), row),
                  pl.BlockSpec((1, d), lambda i: (0, 0)),
                  pl.BlockSpec(w_bf.shape, lambda i: (0, 0), pipeline_mode=pl.Buffered(1))],
        out_specs=[s for _, s in outs],
        out_shape=[s for s, _ in outs],
        compiler_params=pltpu.CompilerParams(dimension_semantics=("arbitrary",),
                                             vmem_limit_bytes=VMEM_LIMIT),
        name="in_proj",
    )(x, g.reshape(1, d), w_bf)


def _attn_kernel(*refs, tq, tk, n_near, causal_walk, combine, lam_init):
    far_walk = causal_walk and combine == "diff"
    refs = list(refs)
    lam_ref = refs.pop(0) if combine == "diff" else None
    q_ref = refs.pop(0)
    q_next_ref = refs.pop(0) if causal_walk else None
    k_ref, vt_ref, bias_ref = refs[:3]
    del refs[:3]
    g_ref = refs.pop(0) if combine == "diff" else None
    o_ref, m_ref, acc_ref = refs[:3]
    qi = pl.program_id(2)

    def two_streams(ref):
        q = ref[...]
        lane = lax.broadcasted_iota(jnp.int32, q.shape, 1)
        zero = jnp.zeros_like(q)
        return jnp.concatenate([jnp.where(lane < HEAD_DIM, q, zero),
                                jnp.where(lane >= HEAD_DIM, q, zero)], axis=0)

    def scores(qz, kb):
        start = kb * tk if isinstance(kb, int) else pl.multiple_of(kb * tk, tk)
        k = k_ref[pl.ds(start, tk), :]
        return lax.dot_general(k, qz, (((1,), (1,)), ((), ())),
                               preferred_element_type=F32)

    if not causal_walk:
        qz = two_streams(q_ref)
        m = jnp.full(m_ref.shape, MASKED, F32)
        acc = jnp.zeros(acc_ref.shape, F32)
        for t in range(n_near):
            s = scores(qz, t) + bias_ref[t]
            m_new = jnp.maximum(m, jnp.max(s, axis=0, keepdims=True))
            alpha = jnp.exp2(m - m_new)
            p = jnp.exp2(s - m_new)
            acc = alpha * acc + jnp.dot(vt_ref[t], p.astype(BF16), preferred_element_type=F32)
            m = m_new
        m_ref[...] = m
        acc_ref[...] = acc
    else:
        sbuf_ref, gmax_ref = refs[-2:]
        group = GROUP_TILES
        half = group // GROUP_HALVES

        class Walk:
            def __init__(self, ref, tile):
                self.qz = two_streams(ref)
                self.n_visible = (tq // tk) * (tile + 1)
                self.n_tiles = (self.n_visible if far_walk
                                else jnp.minimum(self.n_visible, n_near))

            def key_tile(self, n):
                return jnp.maximum(self.n_visible - 1 - n, 0)

        this = Walk(q_ref, qi)
        following = Walk(q_next_ref, jnp.minimum(qi + 1, pl.num_programs(2) - 1))

        def store_scores(walk, g, t, hmax, masked):
            n = g * group + t
            s = scores(walk.qz, walk.key_tile(n))
            if isinstance(g, int) and n < n_near:
                s = s + bias_ref[n]
            if masked:
                s = jnp.where(n < walk.n_tiles, s, MASKED)
            sbuf_ref[t * tk:(t + 1) * tk, :] = s
            mt = jnp.max(s, axis=0, keepdims=True)
            return mt if hmax is None else jnp.maximum(hmax, mt)

        def first_group(walk, t, hmax):
            return store_scores(walk, 0, t, hmax, True)

        def group_step(g, produce):
            m = m_ref[...]
            acc = acc_ref[...]
            for h in range(GROUP_HALVES):
                m_new = jnp.maximum(m, gmax_ref[h])
                alpha = jnp.exp2(m - m_new)
                pv = None
                hmax = None
                for t in range(h * half, (h + 1) * half):
                    s = sbuf_ref[t * tk:(t + 1) * tk, :]
                    if produce == "following":
                        hmax = first_group(following, t, hmax)
                    else:
                        hmax = store_scores(this, g + 1, t, hmax, produce == "masked")
                    p = jnp.exp2(s - m_new)
                    d = jnp.dot(vt_ref[this.key_tile(g * group + t)], p.astype(BF16),
                                preferred_element_type=F32)
                    pv = d if pv is None else pv + d
                gmax_ref[h] = hmax
                acc = alpha * acc + pv
                m = m_new
            acc_ref[...] = acc
            m_ref[...] = m

        m_ref[...] = jnp.full(m_ref.shape, MASKED, F32)
        acc_ref[...] = jnp.zeros(acc_ref.shape, F32)

        @pl.when(qi == 0)
        def _():
            for h in range(GROUP_HALVES):
                hmax = None
                for t in range(h * half, (h + 1) * half):
                    hmax = first_group(this, t, hmax)
                gmax_ref[h] = hmax

        if far_walk:
            n_groups = (this.n_tiles + group - 1) // group
            n_plain = jnp.maximum(n_groups - 2, 0)

            def pair(i, carry):
                group_step(2 * i, "plain")
                group_step(2 * i + 1, "plain")
                return carry

            lax.fori_loop(0, n_plain // 2, pair, 0)

            @pl.when(n_plain % 2 == 1)
            def _():
                group_step(n_plain - 1, "plain")

            @pl.when(n_groups >= 2)
            def _():
                group_step(n_groups - 2, "masked")

            group_step(n_groups - 1, "following")
        else:
            assert n_near <= group
            group_step(0, "following")

    o = acc_ref[:LANES, :] / acc_ref[LANES:LANES + 1, :]
    if combine == "diff":
        d = o[:, :tq] - lam_ref[0] * o[:, tq:]
        y = d * lax.rsqrt(jnp.mean(d * d, axis=0, keepdims=True) + SUBLN_EPS)
        o_ref[...] = ((y.T * g_ref[...]) * (1.0 - lam_init)).astype(o_ref.dtype)
    else:
        row = lax.broadcasted_iota(jnp.int32, (LANES, tq), 0)
        o_ref[...] = jnp.where(row < HEAD_DIM, o[:, :tq], o[:, tq:]).T.astype(o_ref.dtype)


def _attention(q, k, vt, bias, *, tq, tk, causal_walk, combine, out_dtype,
               lam=None, g=None, lam_init=0.0):
    b, tq_total, width = q.shape
    tk_total = k.shape[1]
    groups = width // LANES
    n_near = bias.shape[0]
    nkb = tk_total // tk
    assert tq_total % tq == 0 and vt.shape == (groups, b * nkb, VT_ROWS, tk)
    nq = tq_total // tq
    in_specs = [
        pl.BlockSpec((None, tq, LANES), lambda bi, h, qi: (bi, qi, h)),
        pl.BlockSpec((None, tk_total, LANES), lambda bi, h, qi: (bi, 0, h)),
        pl.BlockSpec((None, nkb, VT_ROWS, tk), lambda bi, h, qi: (h, bi, 0, 0)),
        pl.BlockSpec((n_near, None, tk, 2 * tq), lambda bi, h, qi: (0, h, 0, 0)),
    ]
    args = [q, k, vt, bias]
    if causal_walk:
        in_specs.insert(1, pl.BlockSpec((None, tq, LANES),
                                        lambda bi, h, qi: (bi, jnp.minimum(qi + 1, nq - 1), h)))
        args.insert(1, q)
    if combine == "diff":
        in_specs = [pl.BlockSpec(memory_space=pltpu.SMEM)] + in_specs
        in_specs.append(pl.BlockSpec((1, LANES), lambda bi, h, qi: (0, 0)))
        args = [lam.reshape(1).astype(F32)] + args + [g.reshape(1, LANES).astype(F32)]
    scratch = [pltpu.VMEM((1, 2 * tq), F32), pltpu.VMEM((VT_ROWS, 2 * tq), F32)]
    if causal_walk:
        scratch.append(pltpu.VMEM((GROUP_TILES * tk, 2 * tq), F32))
        scratch.append(pltpu.VMEM((GROUP_HALVES, 1, 2 * tq), F32))
    return pl.pallas_call(
        functools.partial(_attn_kernel, tq=tq, tk=tk, n_near=n_near, causal_walk=causal_walk,
                          combine=combine, lam_init=lam_init),
        grid=(b, groups, nq),
        in_specs=in_specs,
        out_specs=pl.BlockSpec((None, tq, LANES), lambda bi, h, qi: (bi, qi, h)),
        out_shape=jax.ShapeDtypeStruct((b, tq_total, width), out_dtype),
        scratch_shapes=scratch,
        compiler_params=pltpu.CompilerParams(
            dimension_semantics=("arbitrary", "arbitrary", "arbitrary"),
            vmem_limit_bytes=VMEM_LIMIT),
        name="attn_" + combine + ("_walk" if causal_walk else "_flat"),
    )(*args)


def _transpose_values(v, tk):
    b, t, width = v.shape
    vt = v.reshape(b * (t // tk), tk, width // LANES, LANES).transpose(2, 0, 3, 1)
    ones = jnp.broadcast_to(_ones_rows(tk), vt.shape[:2] + (VT_ROWS - LANES, tk))
    return jnp.concatenate([vt, ones], axis=2)


def _rms(x, g, eps):
    return x * lax.rsqrt(jnp.mean(x * x, axis=-1, keepdims=True) + eps) * g


def _post_kernel(x_ref, oa_ref, ob_ref, p_ref, gb_ref, gm_ref, gf_ref, wout_ref, wup_ref,
                 wdown_ref, wgate_ref, wple_ref, y_ref):
    obn = _rms(ob_ref[...], gb_ref[...], EPS).astype(BF16)
    h = x_ref[...] + (
        jnp.dot(oa_ref[...], wout_ref[:WIDTH_A, :], preferred_element_type=F32)
        + jnp.dot(obn, wout_ref[WIDTH_A:, :], preferred_element_type=F32))
    c = _rms(h, gm_ref[...], EPS).astype(BF16)
    d_ff = wup_ref.shape[1]
    step = wup_ref.shape[0]
    for j in range(d_ff // step):
        u = jnp.dot(c, wup_ref[:, j * step:(j + 1) * step], preferred_element_type=F32)
        u = jnp.square(jnp.maximum(u, 0.0)).astype(BF16)
        h = h + jnp.dot(u, wdown_ref[j * step:(j + 1) * step, :], preferred_element_type=F32)
    gate = jax.nn.sigmoid(jnp.dot(h.astype(BF16), wgate_ref[...], preferred_element_type=F32))
    h = h + gate * jnp.dot(p_ref[...].astype(BF16), wple_ref[...], preferred_element_type=F32)
    y_ref[...] = _rms(h, gf_ref[...], EPS)


def _post(x, oa, ob, p, gb, gm, gf, wout, wup, wdown, wgate, wple):
    n, d = x.shape
    tm = min(ROW_TILE, n)
    assert n % tm == 0
    row = lambda i: (i, 0)
    fixed = lambda a: pl.BlockSpec(a.shape, lambda i: (0,) * a.ndim, pipeline_mode=pl.Buffered(1))
    vec = lambda a: a.reshape(1, -1).astype(F32)
    gb, gm, gf = vec(gb), vec(gm), vec(gf)
    return pl.pallas_call(
        _post_kernel,
        grid=(n // tm,),
        in_specs=[pl.BlockSpec((tm, d), row), pl.BlockSpec((tm, oa.shape[1]), row),
                  pl.BlockSpec((tm, ob.shape[1]), row), pl.BlockSpec((tm, p.shape[1]), row),
                  fixed(gb), fixed(gm), fixed(gf), fixed(wout), fixed(wup), fixed(wdown),
                  fixed(wgate), fixed(wple)],
        out_specs=pl.BlockSpec((tm, d), row),
        out_shape=jax.ShapeDtypeStruct((n, d), F32),
        compiler_params=pltpu.CompilerParams(dimension_semantics=("arbitrary",),
                                             vmem_limit_bytes=VMEM_LIMIT),
        name="post",
    )(x, oa, ob, p, gb, gm, gf, wout, wup, wdown, wgate, wple)


def _pad_rows(a, rows):
    return jnp.pad(a, ((0, 0), (0, rows - a.shape[1]), (0, 0)))


def _sample_bias(stream_vec_fn, n_keys, tk, tq, first_kpos, past, band):
    def visible(t, a, b):
        kpos = first_kpos + a
        kc = kpos // CHUNK
        qc = (past + b) // CHUNK
        vis = (kc <= qc) & (a < n_keys)
        if band:
            vis = vis & (kc >= qc - BAND_CHUNKS) & (kpos >= 0)
        return vis

    return _bias_tiles(stream_vec_fn, tk, tq, (first_kpos - past,), visible)


def kernel(x_prompt, x_sample, cache_a_k, cache_a_v, cache_b_k, cache_b_v, p_prompt, p_sample,
           t5_table, g_attn, w_in, lambda_q1, lambda_k1, lambda_q2, lambda_k2, subln_g,
           band_table, out_norm_b, w_out, g_mlp, w_up, w_down, w_ple_gate, w_ple_proj, g_final):
    depth = w_in.shape[0]
    assert depth == 1
    i = 0
    B, S, D = x_prompt.shape
    Bs, Ts, _ = x_sample.shape
    past = cache_a_k.shape[2]
    cache_b_len = cache_b_k.shape[2]
    keep = min(WINDOW_B, S)
    assert keep == ROW_TILE and S % ATTN_BLOCK_Q == 0 and Bs * Ts == ROW_TILE

    lam_init = 0.8 - 0.6 * math.exp(-0.3 * i)
    lam = (jnp.exp(jnp.sum(lambda_q1[i].astype(F32) * lambda_k1[i].astype(F32)))
           - jnp.exp(jnp.sum(lambda_q2[i].astype(F32) * lambda_k2[i].astype(F32)))
           + lam_init)

    w_in_bf = w_in[i].astype(BF16)
    post_w = (out_norm_b[i], g_mlp[i], g_final, w_out[i].astype(BF16), w_up[i].astype(BF16),
              w_down[i].astype(BF16), w_ple_gate[i].astype(BF16), w_ple_proj[i].astype(BF16))
    tk_p, tq_p = ATTN_BLOCK_K, ATTN_BLOCK_Q

    xp = x_prompt.reshape(B * S, D)
    (qa, ka32, ka16, va32, vta, qb, kb16, vtb, kb32, vb32) = _project(
        xp, g_attn[i], w_in_bf, tail_period=S // ROW_TILE, time_major_k=True)
    to3 = lambda a: a.reshape(B, S, a.shape[-1])
    oa = _attention(to3(qa), to3(ka16), vta, _diff_prompt_bias(t5_table), tq=tq_p, tk=tk_p,
                    causal_walk=True, combine="diff", out_dtype=BF16, lam=lam, g=subln_g[i],
                    lam_init=lam_init)
    ob = _attention(to3(qb), to3(kb16), vtb, _band_prompt_bias(band_table[i]), tq=tq_p, tk=tk_p,
                    causal_walk=True, combine="band", out_dtype=F32)
    y_prompt = _post(xp, oa.reshape(B * S, WIDTH_A), ob.reshape(B * S, WIDTH_B),
                     p_prompt[i].reshape(B * S, -1), *post_w).reshape(B, S, D)

    xs = x_sample.reshape(Bs * Ts, D)
    (qa_s, ka32_s, ka16_s, va32_s, _, qb_s, kb16_s, _, kb32_s, vb32_s) = _project(
        xs, g_attn[i], w_in_bf, tail_period=1)
    to3s = lambda a: a.reshape(Bs, Ts, a.shape[-1])
    tq_s = LANES
    assert Ts <= tq_s

    def sample_keys(cache, new):
        c = cache.reshape(Bs, cache.shape[1], -1).astype(BF16)
        allk = jnp.concatenate([c, to3s(new).astype(BF16)], axis=1)
        n_keys = allk.shape[1]
        tk = -(-n_keys // LANES) * LANES
        return _pad_rows(allk, tk), n_keys, tk

    ka_all, n_a, tk_a = sample_keys(cache_a_k[i], ka16_s)
    va_all, _, _ = sample_keys(cache_a_v[i], va32_s)
    kb_all, n_b, tk_b = sample_keys(cache_b_k[i], kb16_s)
    vb_all, _, _ = sample_keys(cache_b_v[i], vb32_s)

    bias_a = _sample_bias(_diff_streams(t5_table), n_a, tk_a, tq_s, 0, past, band=False)
    bias_b = _sample_bias(_band_streams(band_table[i]), n_b, tk_b, tq_s,
                          past - cache_b_len, past, band=True)

    oa_s = _attention(_pad_rows(to3s(qa_s), tq_s), ka_all, _transpose_values(va_all, tk_a),
                      bias_a, tq=tq_s, tk=tk_a, causal_walk=False, combine="diff",
                      out_dtype=BF16, lam=lam, g=subln_g[i], lam_init=lam_init)[:, :Ts]
    ob_s = _attention(_pad_rows(to3s(qb_s), tq_s), kb_all, _transpose_values(vb_all, tk_b),
                      bias_b, tq=tq_s, tk=tk_b, causal_walk=False, combine="band",
                      out_dtype=F32)[:, :Ts]
    y_sample = _post(xs, oa_s.reshape(Bs * Ts, WIDTH_A), ob_s.reshape(Bs * Ts, WIDTH_B),
                     p_sample[i].reshape(Bs * Ts, -1), *post_w).reshape(Bs, Ts, D)

    return (y_prompt, y_sample,
            ka32.transpose(0, 4, 1, 2, 3)[None],
            va32.reshape(1, B, S, N_HEADS_A, 2 * HEAD_DIM),
            kb32.reshape(1, B, keep, N_HEADS_B, HEAD_DIM),
            vb32.reshape(1, B, keep, N_HEADS_B, HEAD_DIM),
            ka32_s.reshape(1, Bs, Ts, N_HEADS_A, 2, HEAD_DIM),
            va32_s.reshape(1, Bs, Ts, N_HEADS_A, 2 * HEAD_DIM),
            kb32_s.reshape(1, Bs, Ts, N_HEADS_B, HEAD_DIM),
            vb32_s.reshape(1, Bs, Ts, N_HEADS_B, HEAD_DIM))
```

```python
import functools
import math

import jax
import jax.numpy as jnp
from jax import lax
from jax.experimental import pallas as pl
from jax.experimental.pallas import tpu as pltpu

CHUNK = 64
HEAD_DIM = 64
N_HEADS_A = 4
N_HEADS_B = 8
WIDTH_A = N_HEADS_A * 2 * HEAD_DIM
WIDTH_B = N_HEADS_B * HEAD_DIM
N_BUCKETS = 32
MAX_DISTANCE = 128
BAND_CHUNKS = 8
WINDOW_B = BAND_CHUNKS * CHUNK
REL_CLIP = 128
EPS = 1e-6
SUBLN_EPS = 1e-5
MASKED = -1e30
LOG2E = math.log2(math.e)
Q_SCALE = LOG2E * HEAD_DIM ** -0.5

LANES = 128
VT_ROWS = LANES + 16
ATTN_BLOCK_K = 256
ATTN_BLOCK_Q = 512
GROUP_TILES = 4
GROUP_HALVES = 2
ROW_TILE = 512
VMEM_LIMIT = 56 * 1024 * 1024

F32 = jnp.float32
BF16 = jnp.bfloat16


def _t5_bucket(rel):
    half = N_BUCKETS // 2
    n = -rel
    ret = jnp.where(n < 0, half, 0)
    n = jnp.abs(n)
    max_exact = half // 2
    nf = jnp.maximum(n, 1).astype(F32)
    large = max_exact + (jnp.log(nf / max_exact) / math.log(MAX_DISTANCE / max_exact)
                         * (half - max_exact)).astype(jnp.int32)
    large = jnp.minimum(large, half - 1)
    return ret + jnp.where(n < max_exact, n, large)


def _bias_kernel(v_ref, o_ref, *, tk, tq, lane_starts, visible):
    a = lax.broadcasted_iota(jnp.int32, (tk, tq), 0)
    b = lax.broadcasted_iota(jnp.int32, (tk, tq), 1)
    for s in range(2):
        x = jnp.broadcast_to(v_ref[s], (tk, v_ref.shape[-1]))
        x = pltpu.roll(x, 0, 1, stride=1, stride_axis=0)
        for t, start in enumerate(lane_starts):
            tile = x[:, start:start + tq]
            vis = visible(t, a, b)
            o_ref[t, :, s * tq:(s + 1) * tq] = tile if vis is None else jnp.where(vis, tile, MASKED)


def _bias_tiles(stream_vec_fn, tk, tq, offsets, visible):
    first_start = -(-tk // LANES) * LANES
    r0 = first_start + max(offsets)
    lane_starts = tuple(r0 - off for off in offsets)
    assert all(s % LANES == 0 for s in lane_starts) and tq % LANES == 0
    n_lanes = max(lane_starts) + tq
    v = stream_vec_fn(r0 - jnp.arange(n_lanes)).astype(F32) * LOG2E
    groups = v.shape[0]
    return pl.pallas_call(
        functools.partial(_bias_kernel, tk=tk, tq=tq, lane_starts=lane_starts, visible=visible),
        grid=(groups,),
        in_specs=[pl.BlockSpec((None, 2, 1, n_lanes), lambda g: (g, 0, 0, 0))],
        out_specs=pl.BlockSpec((len(offsets), None, tk, 2 * tq), lambda g: (0, g, 0, 0)),
        out_shape=jax.ShapeDtypeStruct((len(offsets), groups, tk, 2 * tq), F32),
        compiler_params=pltpu.CompilerParams(dimension_semantics=("arbitrary",),
                                             vmem_limit_bytes=VMEM_LIMIT),
        name="bias_tiles",
    )(v.reshape(groups, 2, 1, n_lanes))


def _diff_streams(table, shift=None):
    def fn(rel):
        v = jnp.transpose(table[_t5_bucket(rel)].astype(F32))
        if shift is not None:
            v = v - shift[:, None]
        return jnp.stack([v, v], axis=1)
    return fn


def _band_streams(table):
    def fn(rel):
        v = jnp.transpose(table[jnp.clip(rel, -REL_CLIP, REL_CLIP) + REL_CLIP].astype(F32))
        return v.reshape(N_HEADS_B // 2, 2, -1)
    return fn


def _diff_prompt_bias(t5_table):
    tk, tq = ATTN_BLOCK_K, ATTN_BLOCK_Q
    assert tk >= MAX_DISTANCE
    far = t5_table[_t5_bucket(jnp.array(-2 * MAX_DISTANCE))].astype(F32)
    offsets = _near_offsets(tk, tq, reach=tk)
    visible = lambda t, a, b: None if offsets[t] < 0 else (a + offsets[t]) // CHUNK <= b // CHUNK
    return _bias_tiles(_diff_streams(t5_table, far), tk, tq, offsets, visible)


def _band_prompt_bias(band_table):
    tk, tq = ATTN_BLOCK_K, ATTN_BLOCK_Q
    offsets = _near_offsets(tk, tq, reach=WINDOW_B)

    def visible(t, a, b):
        kc = (a + offsets[t]) // CHUNK
        qc = b // CHUNK
        return (kc <= qc) & (kc >= qc - BAND_CHUNKS)

    return _bias_tiles(_band_streams(band_table), tk, tq, offsets, visible)


def _near_offsets(tk, tq, reach):
    assert tq % tk == 0 and reach % tk == 0
    return tuple(range(tq - tk, -reach - 1, -tk))


def _store_transposed(vt_ref, v):
    blk = vt_ref.shape[-1]
    ones_rows = _ones_rows(blk)
    for h in range(vt_ref.shape[0]):
        for c in range(vt_ref.shape[1]):
            vt_ref[h, c, :LANES] = v[c * blk:(c + 1) * blk, h * LANES:(h + 1) * LANES].T.astype(BF16)
            vt_ref[h, c, LANES:] = ones_rows


def _ones_rows(n):
    row = lax.broadcasted_iota(jnp.int32, (VT_ROWS - LANES, n), 0)
    return jnp.where(row == 0, 1.0, 0.0).astype(BF16)


def _proj_kernel(x_ref, g_ref, w_ref, qa_ref, ka32_ref, ka16_ref, va32_ref, vta_ref,
                 qb_ref, kb16_ref, vtb_ref, kb32_ref, vb32_ref, *, tail_period):
    x = x_ref[...]
    xn = x * lax.rsqrt(jnp.mean(x * x, axis=-1, keepdims=True) + EPS) * g_ref[...]
    xn = xn.astype(BF16)

    def cols(c):
        return jnp.dot(xn, w_ref[:, c * WIDTH_A:(c + 1) * WIDTH_A], preferred_element_type=F32)

    qa_ref[...] = (cols(0) * Q_SCALE).astype(BF16)
    ka = cols(1)
    if len(ka32_ref.shape) == 2:
        ka32_ref[...] = ka
    else:
        for h in range(N_HEADS_A):
            t = ka[:, h * LANES:(h + 1) * LANES].T
            ka32_ref[h, 0] = t[:HEAD_DIM]
            ka32_ref[h, 1] = t[HEAD_DIM:]
    ka16_ref[...] = ka.astype(BF16)
    va = cols(2)
    va32_ref[...] = va
    _store_transposed(vta_ref, va)
    qb_ref[...] = (cols(3) * Q_SCALE).astype(BF16)
    kb = cols(4)
    kb16_ref[...] = kb.astype(BF16)
    vb = cols(5)
    _store_transposed(vtb_ref, vb)

    @pl.when(pl.program_id(0) % tail_period == tail_period - 1)
    def _():
        kb32_ref[...] = kb
        vb32_ref[...] = vb


def _project(x, g, w_bf, tail_period, time_major_k=False):
    n, d = x.shape
    tm = ROW_TILE
    blk = ATTN_BLOCK_K
    assert n % (tm * tail_period) == 0 and tm % blk == 0
    groups = WIDTH_A // LANES
    row = lambda i: (i, 0)
    tail = lambda i: (i // tail_period, 0)
    rows = lambda dt, imap=row: (jax.ShapeDtypeStruct((n, WIDTH_A), dt),
                                 pl.BlockSpec((tm, WIDTH_A), imap))
    vt = (jax.ShapeDtypeStruct((groups, n // blk, VT_ROWS, blk), BF16),
          pl.BlockSpec((groups, tm // blk, VT_ROWS, blk), lambda i: (0, i, 0, 0)))
    n_tail = n // tail_period
    tails = (jax.ShapeDtypeStruct((n_tail, WIDTH_B), F32), pl.BlockSpec((tm, WIDTH_B), tail))
    if time_major_k:
        per_batch = tail_period * tm
        ka32 = (jax.ShapeDtypeStruct((n // per_batch, N_HEADS_A, 2, HEAD_DIM, per_batch), F32),
                pl.BlockSpec((None, N_HEADS_A, 2, HEAD_DIM, tm),
                             lambda i: (i // tail_period, 0, 0, 0, i % tail_period)))
    else:
        ka32 = rows(F32)
    outs = [rows(BF16), ka32, rows(BF16), rows(F32), vt,
            rows(BF16), rows(BF16), vt, tails, tails]
    return pl.pallas_call(
        functools.partial(_proj_kernel, tail_period=tail_period),
        grid=(n // tm,),
        in_specs=[pl.BlockSpec((tm, d), row),
                  pl.BlockSpec((1, d), lambda i: (0, 0)),
                  pl.BlockSpec(w_bf.shape, lambda i: (0, 0), pipeline_mode=pl.Buffered(1))],
        out_specs=[s for _, s in outs],
        out_shape=[s for s, _ in outs],
        compiler_params=pltpu.CompilerParams(dimension_semantics=("arbitrary",),
                                             vmem_limit_bytes=VMEM_LIMIT),
        name="in_proj",
    )(x, g.reshape(1, d), w_bf)


def _attn_kernel(*refs, tq, tk, n_near, causal_walk, combine, lam_init):
    far_walk = causal_walk and combine == "diff"
    refs = list(refs)
    lam_ref = refs.pop(0) if combine == "diff" else None
    q_ref = refs.pop(0)
    q_next_ref = refs.pop(0) if causal_walk else None
    k_ref, vt_ref, bias_ref = refs[:3]
    del refs[:3]
    g_ref = refs.pop(0) if combine == "diff" else None
    o_ref, m_ref, acc_ref = refs[:3]
    qi = pl.program_id(2)

    def two_streams(ref):
        q = ref[...]
        lane = lax.broadcasted_iota(jnp.int32, q.shape, 1)
        zero = jnp.zeros_like(q)
        return jnp.concatenate([jnp.where(lane < HEAD_DIM, q, zero),
                                jnp.where(lane >= HEAD_DIM, q, zero)], axis=0)

    def scores(qz, kb):
        start = kb * tk if isinstance(kb, int) else pl.multiple_of(kb * tk, tk)
        k = k_ref[pl.ds(start, tk), :]
        return lax.dot_general(k, qz, (((1,), (1,)), ((), ())),
                               preferred_element_type=F32)

    if not causal_walk:
        qz = two_streams(q_ref)
        m = jnp.full(m_ref.shape, MASKED, F32)
        acc = jnp.zeros(acc_ref.shape, F32)
        for t in range(n_near):
            s = scores(qz, t) + bias_ref[t]
            m_new = jnp.maximum(m, jnp.max(s, axis=0, keepdims=True))
            alpha = jnp.exp2(m - m_new)
            p = jnp.exp2(s - m_new)
            acc = alpha * acc + jnp.dot(vt_ref[t], p.astype(BF16), preferred_element_type=F32)
            m = m_new
        m_ref[...] = m
        acc_ref[...] = acc
    else:
        sbuf_ref, gmax_ref = refs[-2:]
        group = GROUP_TILES
        half = group // GROUP_HALVES

        class Walk:
            def __init__(self, ref, tile):
                self.qz = two_streams(ref)
                self.n_visible = (tq // tk) * (tile + 1)
                self.n_tiles = (self.n_visible if far_walk
                                else jnp.minimum(self.n_visible, n_near))

            def key_tile(self, n):
                return jnp.maximum(self.n_visible - 1 - n, 0)

        this = Walk(q_ref, qi)
        following = Walk(q_next_ref, jnp.minimum(qi + 1, pl.num_programs(2) - 1))

        def store_scores(walk, g, t, hmax, masked):
            n = g * group + t
            s = scores(walk.qz, walk.key_tile(n))
            if isinstance(g, int) and n < n_near:
                s = s + bias_ref[n]
            if masked:
                s = jnp.where(n < walk.n_tiles, s, MASKED)
            sbuf_ref[t * tk:(t + 1) * tk, :] = s
            mt = jnp.max(s, axis=0, keepdims=True)
            return mt if hmax is None else jnp.maximum(hmax, mt)

        def first_group(walk, t, hmax):
            return store_scores(walk, 0, t, hmax, True)

        def group_step(g, produce):
            m = m_ref[...]
            acc = acc_ref[...]
            for h in range(GROUP_HALVES):
                m_new = jnp.maximum(m, gmax_ref[h])
                alpha = jnp.exp2(m - m_new)
                pv = None
                hmax = None
                for t in range(h * half, (h + 1) * half):
                    s = sbuf_ref[t * tk:(t + 1) * tk, :]
                    if produce == "following":
                        hmax = first_group(following, t, hmax)
                    else:
                        hmax = store_scores(this, g + 1, t, hmax, produce == "masked")
                    p = jnp.exp2(s - m_new)
                    d = jnp.dot(vt_ref[this.key_tile(g * group + t)], p.astype(BF16),
                                preferred_element_type=F32)
                    pv = d if pv is None else pv + d
                gmax_ref[h] = hmax
                acc = alpha * acc + pv
                m = m_new
            acc_ref[...] = acc
            m_ref[...] = m

        m_ref[...] = jnp.full(m_ref.shape, MASKED, F32)
        acc_ref[...] = jnp.zeros(acc_ref.shape, F32)

        @pl.when(qi == 0)
        def _():
            for h in range(GROUP_HALVES):
                hmax = None
                for t in range(h * half, (h + 1) * half):
                    hmax = first_group(this, t, hmax)
                gmax_ref[h] = hmax

        if far_walk:
            n_groups = (this.n_tiles + group - 1) // group
            n_plain = jnp.maximum(n_groups - 2, 0)

            def plain_steps(first, count):
                for j in range(count):
                    group_step(first + j, "plain")

            def quad(i, carry):
                plain_steps(4 * i, 4)
                return carry

            lax.fori_loop(0, n_plain // 4, quad, 0)
            done = (n_plain // 4) * 4

            @pl.when(n_plain - done >= 2)
            def _():
                plain_steps(done, 2)

            @pl.when(n_plain % 2 == 1)
            def _():
                plain_steps(n_plain - 1, 1)

            @pl.when(n_groups >= 2)
            def _():
                group_step(n_groups - 2, "masked")
                group_step(n_groups - 1, "following")

            @pl.when(n_groups < 2)
            def _():
                group_step(n_groups - 1, "following")
        else:
            assert n_near <= group
            group_step(0, "following")

    o = acc_ref[:LANES, :] / acc_ref[LANES:LANES + 1, :]
    if combine == "diff":
        d = o[:, :tq] - lam_ref[0] * o[:, tq:]
        y = d * lax.rsqrt(jnp.mean(d * d, axis=0, keepdims=True) + SUBLN_EPS)
        o_ref[...] = ((y.T * g_ref[...]) * (1.0 - lam_init)).astype(o_ref.dtype)
    else:
        row = lax.broadcasted_iota(jnp.int32, (LANES, tq), 0)
        o_ref[...] = jnp.where(row < HEAD_DIM, o[:, :tq], o[:, tq:]).T.astype(o_ref.dtype)


def _attention(q, k, vt, bias, *, tq, tk, causal_walk, combine, out_dtype,
               lam=None, g=None, lam_init=0.0):
    b, tq_total, width = q.shape
    tk_total = k.shape[1]
    groups = width // LANES
    n_near = bias.shape[0]
    nkb = tk_total // tk
    assert tq_total % tq == 0 and vt.shape == (groups, b * nkb, VT_ROWS, tk)
    nq = tq_total // tq
    if causal_walk:
        grid = (b, groups, nq)
        at = lambda f: f
    else:
        grid = (groups, b, nq)
        at = lambda f: lambda h, bi, qi: f(bi, h, qi)
    in_specs = [
        pl.BlockSpec((None, tq, LANES), at(lambda bi, h, qi: (bi, qi, h))),
        pl.BlockSpec((None, tk_total, LANES), at(lambda bi, h, qi: (bi, 0, h))),
        pl.BlockSpec((None, nkb, VT_ROWS, tk), at(lambda bi, h, qi: (h, bi, 0, 0))),
        pl.BlockSpec((n_near, None, tk, 2 * tq), at(lambda bi, h, qi: (0, h, 0, 0))),
    ]
    args = [q, k, vt, bias]
    if causal_walk:
        in_specs.insert(1, pl.BlockSpec((None, tq, LANES),
                                        lambda bi, h, qi: (bi, jnp.minimum(qi + 1, nq - 1), h)))
        args.insert(1, q)
    if combine == "diff":
        in_specs = [pl.BlockSpec(memory_space=pltpu.SMEM)] + in_specs
        in_specs.append(pl.BlockSpec((1, LANES), lambda *_: (0, 0)))
        args = [lam.reshape(1).astype(F32)] + args + [g.reshape(1, LANES).astype(F32)]
    scratch = [pltpu.VMEM((1, 2 * tq), F32), pltpu.VMEM((VT_ROWS, 2 * tq), F32)]
    if causal_walk:
        scratch.append(pltpu.VMEM((GROUP_TILES * tk, 2 * tq), F32))
        scratch.append(pltpu.VMEM((GROUP_HALVES, 1, 2 * tq), F32))
    return pl.pallas_call(
        functools.partial(_attn_kernel, tq=tq, tk=tk, n_near=n_near, causal_walk=causal_walk,
                          combine=combine, lam_init=lam_init),
        grid=grid,
        in_specs=in_specs,
        out_specs=pl.BlockSpec((None, tq, LANES), at(lambda bi, h, qi: (bi, qi, h))),
        out_shape=jax.ShapeDtypeStruct((b, tq_total, width), out_dtype),
        scratch_shapes=scratch,
        compiler_params=pltpu.CompilerParams(
            dimension_semantics=("arbitrary", "arbitrary", "arbitrary"),
            vmem_limit_bytes=VMEM_LIMIT),
        name="attn_" + combine + ("_walk" if causal_walk else "_flat"),
    )(*args)


def _transpose_values(v, tk):
    b, t, width = v.shape
    vt = v.reshape(b * (t // tk), tk, width // LANES, LANES).transpose(2, 0, 3, 1)
    ones = jnp.broadcast_to(_ones_rows(tk), vt.shape[:2] + (VT_ROWS - LANES, tk))
    return jnp.concatenate([vt, ones], axis=2)


def _rms(x, g, eps):
    return x * lax.rsqrt(jnp.mean(x * x, axis=-1, keepdims=True) + eps) * g


def _post_kernel(x_ref, oa_ref, ob_ref, p_ref, gb_ref, gm_ref, gf_ref, wout_ref, wup_ref,
                 wdown_ref, wgate_ref, wple_ref, y_ref):
    obn = _rms(ob_ref[...], gb_ref[...], EPS).astype(BF16)
    h = x_ref[...] + (
        jnp.dot(oa_ref[...], wout_ref[:WIDTH_A, :], preferred_element_type=F32)
        + jnp.dot(obn, wout_ref[WIDTH_A:, :], preferred_element_type=F32))
    c = _rms(h, gm_ref[...], EPS).astype(BF16)
    d_ff = wup_ref.shape[1]
    step = wup_ref.shape[0]
    for j in range(d_ff // step):
        u = jnp.dot(c, wup_ref[:, j * step:(j + 1) * step], preferred_element_type=F32)
        u = jnp.square(jnp.maximum(u, 0.0)).astype(BF16)
        h = h + jnp.dot(u, wdown_ref[j * step:(j + 1) * step, :], preferred_element_type=F32)
    gate = jax.nn.sigmoid(jnp.dot(h.astype(BF16), wgate_ref[...], preferred_element_type=F32))
    h = h + gate * jnp.dot(p_ref[...].astype(BF16), wple_ref[...], preferred_element_type=F32)
    y_ref[...] = _rms(h, gf_ref[...], EPS)


def _post(x, oa, ob, p, gb, gm, gf, wout, wup, wdown, wgate, wple):
    n, d = x.shape
    tm = min(ROW_TILE, n)
    assert n % tm == 0
    row = lambda i: (i, 0)
    fixed = lambda a: pl.BlockSpec(a.shape, lambda i: (0,) * a.ndim, pipeline_mode=pl.Buffered(1))
    vec = lambda a: a.reshape(1, -1).astype(F32)
    gb, gm, gf = vec(gb), vec(gm), vec(gf)
    return pl.pallas_call(
        _post_kernel,
        grid=(n // tm,),
        in_specs=[pl.BlockSpec((tm, d), row), pl.BlockSpec((tm, oa.shape[1]), row),
                  pl.BlockSpec((tm, ob.shape[1]), row), pl.BlockSpec((tm, p.shape[1]), row),
                  fixed(gb), fixed(gm), fixed(gf), fixed(wout), fixed(wup), fixed(wdown),
                  fixed(wgate), fixed(wple)],
        out_specs=pl.BlockSpec((tm, d), row),
        out_shape=jax.ShapeDtypeStruct((n, d), F32),
        compiler_params=pltpu.CompilerParams(dimension_semantics=("arbitrary",),
                                             vmem_limit_bytes=VMEM_LIMIT),
        name="post",
    )(x, oa, ob, p, gb, gm, gf, wout, wup, wdown, wgate, wple)


def _pad_rows(a, rows):
    return jnp.pad(a, ((0, 0), (0, rows - a.shape[1]), (0, 0)))


def _sample_bias(stream_vec_fn, n_keys, tk, tq, first_kpos, past, band):
    def visible(t, a, b):
        kpos = first_kpos + a
        kc = kpos // CHUNK
        qc = (past + b) // CHUNK
        vis = (kc <= qc) & (a < n_keys)
        if band:
            vis = vis & (kc >= qc - BAND_CHUNKS) & (kpos >= 0)
        return vis

    return _bias_tiles(stream_vec_fn, tk, tq, (first_kpos - past,), visible)


def kernel(x_prompt, x_sample, cache_a_k, cache_a_v, cache_b_k, cache_b_v, p_prompt, p_sample,
           t5_table, g_attn, w_in, lambda_q1, lambda_k1, lambda_q2, lambda_k2, subln_g,
           band_table, out_norm_b, w_out, g_mlp, w_up, w_down, w_ple_gate, w_ple_proj, g_final):
    depth = w_in.shape[0]
    assert depth == 1
    i = 0
    B, S, D = x_prompt.shape
    Bs, Ts, _ = x_sample.shape
    past = cache_a_k.shape[2]
    cache_b_len = cache_b_k.shape[2]
    keep = min(WINDOW_B, S)
    assert keep == ROW_TILE and S % ATTN_BLOCK_Q == 0 and Bs * Ts == ROW_TILE

    lam_init = 0.8 - 0.6 * math.exp(-0.3 * i)
    lam = (jnp.exp(jnp.sum(lambda_q1[i].astype(F32) * lambda_k1[i].astype(F32)))
           - jnp.exp(jnp.sum(lambda_q2[i].astype(F32) * lambda_k2[i].astype(F32)))
           + lam_init)

    w_in_bf = w_in[i].astype(BF16)
    post_w = (out_norm_b[i], g_mlp[i], g_final, w_out[i].astype(BF16), w_up[i].astype(BF16),
              w_down[i].astype(BF16), w_ple_gate[i].astype(BF16), w_ple_proj[i].astype(BF16))
    tk_p, tq_p = ATTN_BLOCK_K, ATTN_BLOCK_Q

    xp = x_prompt.reshape(B * S, D)
    (qa, ka32, ka16, va32, vta, qb, kb16, vtb, kb32, vb32) = _project(
        xp, g_attn[i], w_in_bf, tail_period=S // ROW_TILE, time_major_k=True)
    to3 = lambda a: a.reshape(B, S, a.shape[-1])
    oa = _attention(to3(qa), to3(ka16), vta, _diff_prompt_bias(t5_table), tq=tq_p, tk=tk_p,
                    causal_walk=True, combine="diff", out_dtype=BF16, lam=lam, g=subln_g[i],
                    lam_init=lam_init)
    ob = _attention(to3(qb), to3(kb16), vtb, _band_prompt_bias(band_table[i]), tq=tq_p, tk=tk_p,
                    causal_walk=True, combine="band", out_dtype=F32)
    y_prompt = _post(xp, oa.reshape(B * S, WIDTH_A), ob.reshape(B * S, WIDTH_B),
                     p_prompt[i].reshape(B * S, -1), *post_w).reshape(B, S, D)

    xs = x_sample.reshape(Bs * Ts, D)
    (qa_s, ka32_s, ka16_s, va32_s, _, qb_s, kb16_s, _, kb32_s, vb32_s) = _project(
        xs, g_attn[i], w_in_bf, tail_period=1)
    to3s = lambda a: a.reshape(Bs, Ts, a.shape[-1])
    tq_s = LANES
    assert Ts <= tq_s

    def sample_keys(cache, new):
        c = cache.reshape(Bs, cache.shape[1], -1).astype(BF16)
        allk = jnp.concatenate([c, to3s(new).astype(BF16)], axis=1)
        n_keys = allk.shape[1]
        tk = -(-n_keys // LANES) * LANES
        return _pad_rows(allk, tk), n_keys, tk

    ka_all, n_a, tk_a = sample_keys(cache_a_k[i], ka16_s)
    va_all, _, _ = sample_keys(cache_a_v[i], va32_s)
    kb_all, n_b, tk_b = sample_keys(cache_b_k[i], kb16_s)
    vb_all, _, _ = sample_keys(cache_b_v[i], vb32_s)

    bias_a = _sample_bias(_diff_streams(t5_table), n_a, tk_a, tq_s, 0, past, band=False)
    bias_b = _sample_bias(_band_streams(band_table[i]), n_b, tk_b, tq_s,
                          past - cache_b_len, past, band=True)

    oa_s = _attention(_pad_rows(to3s(qa_s), tq_s), ka_all, _transpose_values(va_all, tk_a),
                      bias_a, tq=tq_s, tk=tk_a, causal_walk=False, combine="diff",
                      out_dtype=BF16, lam=lam, g=subln_g[i], lam_init=lam_init)[:, :Ts]
    ob_s = _attention(_pad_rows(to3s(qb_s), tq_s), kb_all, _transpose_values(vb_all, tk_b),
                      bias_b, tq=tq_s, tk=tk_b, causal_walk=False, combine="band",
                      out_dtype=F32)[:, :Ts]
    y_sample = _post(xs, oa_s.reshape(Bs * Ts, WIDTH_A), ob_s.reshape(Bs * Ts, WIDTH_B),
                     p_sample[i].reshape(Bs * Ts, -1), *post_w).reshape(Bs, Ts, D)

    return (y_prompt, y_sample,
            ka32.transpose(0, 4, 1, 2, 3)[None],
            va32.reshape(1, B, S, N_HEADS_A, 2 * HEAD_DIM),
            kb32.reshape(1, B, keep, N_HEADS_B, HEAD_DIM),
            vb32.reshape(1, B, keep, N_HEADS_B, HEAD_DIM),
            ka32_s.reshape(1, Bs, Ts, N_HEADS_A, 2, HEAD_DIM),
            va32_s.reshape(1, Bs, Ts, N_HEADS_A, 2 * HEAD_DIM),
            kb32_s.reshape(1, Bs, Ts, N_HEADS_B, HEAD_DIM),
            vb32_s.reshape(1, Bs, Ts, N_HEADS_B, HEAD_DIM))
```

```python
import functools
import math

import jax
import jax.numpy as jnp
from jax import lax
from jax.experimental import pallas as pl
from jax.experimental.pallas import tpu as pltpu

CHUNK = 64
HEAD_DIM = 64
N_HEADS_A = 4
N_HEADS_B = 8
WIDTH_A = N_HEADS_A * 2 * HEAD_DIM
WIDTH_B = N_HEADS_B * HEAD_DIM
N_BUCKETS = 32
MAX_DISTANCE = 128
BAND_CHUNKS = 8
WINDOW_B = BAND_CHUNKS * CHUNK
REL_CLIP = 128
EPS = 1e-6
SUBLN_EPS = 1e-5
MASKED = -1e30
LOG2E = math.log2(math.e)
Q_SCALE = LOG2E * HEAD_DIM ** -0.5

LANES = 128
VT_ROWS = LANES + 16
ATTN_BLOCK_K = 256
ATTN_BLOCK_Q = 512
GROUP_TILES = 4
GROUP_HALVES = 2
ROW_TILE = 512
VMEM_LIMIT = 56 * 1024 * 1024

F32 = jnp.float32
BF16 = jnp.bfloat16


def _t5_bucket(rel):
    half = N_BUCKETS // 2
    n = -rel
    ret = jnp.where(n < 0, half, 0)
    n = jnp.abs(n)
    max_exact = half // 2
    nf = jnp.maximum(n, 1).astype(F32)
    large = max_exact + (jnp.log(nf / max_exact) / math.log(MAX_DISTANCE / max_exact)
                         * (half - max_exact)).astype(jnp.int32)
    large = jnp.minimum(large, half - 1)
    return ret + jnp.where(n < max_exact, n, large)


def _bias_kernel(v_ref, o_ref, *, tk, tq, lane_starts, visible):
    a = lax.broadcasted_iota(jnp.int32, (tk, tq), 0)
    b = lax.broadcasted_iota(jnp.int32, (tk, tq), 1)
    for s in range(2):
        x = jnp.broadcast_to(v_ref[s], (tk, v_ref.shape[-1]))
        x = pltpu.roll(x, 0, 1, stride=1, stride_axis=0)
        for t, start in enumerate(lane_starts):
            tile = x[:, start:start + tq]
            vis = visible(t, a, b)
            o_ref[t, :, s * tq:(s + 1) * tq] = tile if vis is None else jnp.where(vis, tile, MASKED)


def _bias_tiles(stream_vec_fn, tk, tq, offsets, visible):
    first_start = -(-tk // LANES) * LANES
    r0 = first_start + max(offsets)
    lane_starts = tuple(r0 - off for off in offsets)
    assert all(s % LANES == 0 for s in lane_starts) and tq % LANES == 0
    n_lanes = max(lane_starts) + tq
    v = stream_vec_fn(r0 - jnp.arange(n_lanes)).astype(F32) * LOG2E
    groups = v.shape[0]
    return pl.pallas_call(
        functools.partial(_bias_kernel, tk=tk, tq=tq, lane_starts=lane_starts, visible=visible),
        grid=(groups,),
        in_specs=[pl.BlockSpec((None, 2, 1, n_lanes), lambda g: (g, 0, 0, 0))],
        out_specs=pl.BlockSpec((len(offsets), None, tk, 2 * tq), lambda g: (0, g, 0, 0)),
        out_shape=jax.ShapeDtypeStruct((len(offsets), groups, tk, 2 * tq), F32),
        compiler_params=pltpu.CompilerParams(dimension_semantics=("arbitrary",),
                                             vmem_limit_bytes=VMEM_LIMIT),
        name="bias_tiles",
    )(v.reshape(groups, 2, 1, n_lanes))


def _diff_streams(table, shift=None):
    def fn(rel):
        v = jnp.transpose(table[_t5_bucket(rel)].astype(F32))
        if shift is not None:
            v = v - shift[:, None]
        return jnp.stack([v, v], axis=1)
    return fn


def _band_streams(table):
    def fn(rel):
        v = jnp.transpose(table[jnp.clip(rel, -REL_CLIP, REL_CLIP) + REL_CLIP].astype(F32))
        return v.reshape(N_HEADS_B // 2, 2, -1)
    return fn


def _diff_prompt_bias(t5_table):
    tk, tq = ATTN_BLOCK_K, ATTN_BLOCK_Q
    assert tk >= MAX_DISTANCE
    far = t5_table[_t5_bucket(jnp.array(-2 * MAX_DISTANCE))].astype(F32)
    offsets = _near_offsets(tk, tq, reach=tk)
    visible = lambda t, a, b: None if offsets[t] < 0 else (a + offsets[t]) // CHUNK <= b // CHUNK
    return _bias_tiles(_diff_streams(t5_table, far), tk, tq, offsets, visible)


def _band_prompt_bias(band_table):
    tk, tq = ATTN_BLOCK_K, ATTN_BLOCK_Q
    offsets = _near_offsets(tk, tq, reach=WINDOW_B)

    def visible(t, a, b):
        kc = (a + offsets[t]) // CHUNK
        qc = b // CHUNK
        return (kc <= qc) & (kc >= qc - BAND_CHUNKS)

    return _bias_tiles(_band_streams(band_table), tk, tq, offsets, visible)


def _near_offsets(tk, tq, reach):
    assert tq % tk == 0 and reach % tk == 0
    return tuple(range(tq - tk, -reach - 1, -tk))


def _store_transposed(vt_ref, v):
    blk = vt_ref.shape[-1]
    ones_rows = _ones_rows(blk)
    for h in range(vt_ref.shape[0]):
        for c in range(vt_ref.shape[1]):
            vt_ref[h, c, :LANES] = v[c * blk:(c + 1) * blk, h * LANES:(h + 1) * LANES].T.astype(BF16)
            vt_ref[h, c, LANES:] = ones_rows


def _ones_rows(n):
    row = lax.broadcasted_iota(jnp.int32, (VT_ROWS - LANES, n), 0)
    return jnp.where(row == 0, 1.0, 0.0).astype(BF16)


def _proj_kernel(x_ref, g_ref, w_ref, qa_ref, ka32_ref, ka16_ref, va32_ref, vta_ref,
                 qb_ref, kb16_ref, vtb_ref, kb32_ref, vb32_ref, *, tail_period):
    x = x_ref[...]
    xn = x * lax.rsqrt(jnp.mean(x * x, axis=-1, keepdims=True) + EPS) * g_ref[...]
    xn = xn.astype(BF16)

    def cols(c):
        return jnp.dot(xn, w_ref[:, c * WIDTH_A:(c + 1) * WIDTH_A], preferred_element_type=F32)

    qa_ref[...] = (cols(0) * Q_SCALE).astype(BF16)
    ka = cols(1)
    if len(ka32_ref.shape) == 2:
        ka32_ref[...] = ka
    else:
        for h in range(N_HEADS_A):
            t = ka[:, h * LANES:(h + 1) * LANES].T
            ka32_ref[h, 0] = t[:HEAD_DIM]
            ka32_ref[h, 1] = t[HEAD_DIM:]
    ka16_ref[...] = ka.astype(BF16)
    va = cols(2)
    for h in range(N_HEADS_A):
        va32_ref[:, h, :] = va[:, h * LANES:(h + 1) * LANES]
    _store_transposed(vta_ref, va)
    qb_ref[...] = (cols(3) * Q_SCALE).astype(BF16)
    kb = cols(4)
    kb16_ref[...] = kb.astype(BF16)
    vb = cols(5)
    _store_transposed(vtb_ref, vb)

    @pl.when(pl.program_id(0) % tail_period == tail_period - 1)
    def _():
        kb32_ref[...] = kb
        vb32_ref[...] = vb


def _project(x, g, w_bf, tail_period, time_major_k=False):
    n, d = x.shape
    tm = ROW_TILE
    blk = ATTN_BLOCK_K
    assert n % (tm * tail_period) == 0 and tm % blk == 0
    groups = WIDTH_A // LANES
    row = lambda i: (i, 0)
    tail = lambda i: (i // tail_period, 0)
    rows = lambda dt, imap=row: (jax.ShapeDtypeStruct((n, WIDTH_A), dt),
                                 pl.BlockSpec((tm, WIDTH_A), imap))
    vt = (jax.ShapeDtypeStruct((groups, n // blk, VT_ROWS, blk), BF16),
          pl.BlockSpec((groups, tm // blk, VT_ROWS, blk), lambda i: (0, i, 0, 0)))
    n_tail = n // tail_period
    tails = (jax.ShapeDtypeStruct((n_tail, WIDTH_B), F32), pl.BlockSpec((tm, WIDTH_B), tail))
    if time_major_k:
        per_batch = tail_period * tm
        ka32 = (jax.ShapeDtypeStruct((n // per_batch, N_HEADS_A, 2, HEAD_DIM, per_batch), F32),
                pl.BlockSpec((None, N_HEADS_A, 2, HEAD_DIM, tm),
                             lambda i: (i // tail_period, 0, 0, 0, i % tail_period)))
    else:
        ka32 = rows(F32)
    va32 = (jax.ShapeDtypeStruct((n, N_HEADS_A, LANES), F32),
            pl.BlockSpec((tm, N_HEADS_A, LANES), lambda i: (i, 0, 0)))
    outs = [rows(BF16), ka32, rows(BF16), va32, vt,
            rows(BF16), rows(BF16), vt, tails, tails]
    return pl.pallas_call(
        functools.partial(_proj_kernel, tail_period=tail_period),
        grid=(n // tm,),
        in_specs=[pl.BlockSpec((tm, d), row),
                  pl.BlockSpec((1, d), lambda i: (0, 0)),
                  pl.BlockSpec(w_bf.shape, lambda i: (0, 0), pipeline_mode=pl.Buffered(1))],
        out_specs=[s for _, s in outs],
        out_shape=[s for s, _ in outs],
        compiler_params=pltpu.CompilerParams(dimension_semantics=("arbitrary",),
                                             vmem_limit_bytes=VMEM_LIMIT),
        name="in_proj",
    )(x, g.reshape(1, d), w_bf)


def _attn_kernel(*refs, tq, tk, n_near, causal_walk, combine, lam_init):
    far_walk = causal_walk and combine == "diff"
    refs = list(refs)
    lam_ref = refs.pop(0) if combine == "diff" else None
    q_ref = refs.pop(0)
    q_next_ref = refs.pop(0) if causal_walk else None
    k_ref, vt_ref, bias_ref = refs[:3]
    del refs[:3]
    g_ref = refs.pop(0) if combine == "diff" else None
    o_ref, m_ref, acc_ref = refs[:3]
    qi = pl.program_id(2)

    def two_streams(ref):
        q = ref[...]
        lane = lax.broadcasted_iota(jnp.int32, q.shape, 1)
        zero = jnp.zeros_like(q)
        return jnp.concatenate([jnp.where(lane < HEAD_DIM, q, zero),
                                jnp.where(lane >= HEAD_DIM, q, zero)], axis=0)

    def scores(qz, kb):
        start = kb * tk if isinstance(kb, int) else pl.multiple_of(kb * tk, tk)
        k = k_ref[pl.ds(start, tk), :]
        return lax.dot_general(k, qz, (((1,), (1,)), ((), ())),
                               preferred_element_type=F32)

    if not causal_walk:
        qz = two_streams(q_ref)
        m = jnp.full(m_ref.shape, MASKED, F32)
        acc = jnp.zeros(acc_ref.shape, F32)
        for t in range(n_near):
            s = scores(qz, t) + bias_ref[t]
            m_new = jnp.maximum(m, jnp.max(s, axis=0, keepdims=True))
            alpha = jnp.exp2(m - m_new)
            p = jnp.exp2(s - m_new)
            acc = alpha * acc + jnp.dot(vt_ref[t], p.astype(BF16), preferred_element_type=F32)
            m = m_new
        m_ref[...] = m
        acc_ref[...] = acc
    else:
        sbuf_ref, gmax_ref = refs[-2:]
        group = GROUP_TILES
        half = group // GROUP_HALVES

        class Walk:
            def __init__(self, ref, tile):
                self.qz = two_streams(ref)
                self.n_visible = (tq // tk) * (tile + 1)
                self.n_tiles = (self.n_visible if far_walk
                                else jnp.minimum(self.n_visible, n_near))

            def key_tile(self, n):
                return jnp.maximum(self.n_visible - 1 - n, 0)

        this = Walk(q_ref, qi)
        following = Walk(q_next_ref, jnp.minimum(qi + 1, pl.num_programs(2) - 1))

        def store_scores(walk, g, t, hmax, masked):
            n = g * group + t
            s = scores(walk.qz, walk.key_tile(n))
            if isinstance(g, int) and n < n_near:
                s = s + bias_ref[n]
            if masked:
                s = jnp.where(n < walk.n_tiles, s, MASKED)
            sbuf_ref[t * tk:(t + 1) * tk, :] = s
            mt = jnp.max(s, axis=0, keepdims=True)
            return mt if hmax is None else jnp.maximum(hmax, mt)

        assert 2 * (tq // tk) >= group

        def first_group(walk, t, hmax):
            return store_scores(walk, 0, t, hmax, walk is this)

        def group_step(g, produce):
            m = m_ref[...]
            acc = acc_ref[...]
            for h in range(GROUP_HALVES):
                m_new = jnp.maximum(m, gmax_ref[h])
                alpha = jnp.exp2(m - m_new)
                pv = None
                hmax = None
                for t in range(h * half, (h + 1) * half):
                    s = sbuf_ref[t * tk:(t + 1) * tk, :]
                    if produce == "following":
                        hmax = first_group(following, t, hmax)
                    else:
                        hmax = store_scores(this, g + 1, t, hmax, produce == "masked")
                    p = jnp.exp2(s - m_new)
                    d = jnp.dot(vt_ref[this.key_tile(g * group + t)], p.astype(BF16),
                                preferred_element_type=F32)
                    pv = d if pv is None else pv + d
                gmax_ref[h] = hmax
                acc = alpha * acc + pv
                m = m_new
            acc_ref[...] = acc
            m_ref[...] = m

        m_ref[...] = jnp.full(m_ref.shape, MASKED, F32)
        acc_ref[...] = jnp.zeros(acc_ref.shape, F32)

        @pl.when(qi == 0)
        def _():
            for h in range(GROUP_HALVES):
                hmax = None
                for t in range(h * half, (h + 1) * half):
                    hmax = first_group(this, t, hmax)
                gmax_ref[h] = hmax

        if far_walk:
            n_groups = (this.n_tiles + group - 1) // group
            n_plain = jnp.maximum(n_groups - 2, 0)

            def plain_steps(first, count):
                for j in range(count):
                    group_step(first + j, "plain")

            def quad(i, carry):
                plain_steps(4 * i, 4)
                return carry

            lax.fori_loop(0, n_plain // 4, quad, 0)
            done = (n_plain // 4) * 4

            @pl.when(n_plain - done >= 2)
            def _():
                plain_steps(done, 2)

            @pl.when(n_plain % 2 == 1)
            def _():
                plain_steps(n_plain - 1, 1)

            @pl.when(n_groups >= 2)
            def _():
                group_step(n_groups - 2, "masked")
                group_step(n_groups - 1, "following")

            @pl.when(n_groups < 2)
            def _():
                group_step(n_groups - 1, "following")
        else:
            assert n_near <= group
            group_step(0, "following")

    o = acc_ref[:LANES, :] * (1.0 / acc_ref[LANES:LANES + 1, :])
    if combine == "diff":
        d = o[:, :tq] - lam_ref[0] * o[:, tq:]
        y = d * lax.rsqrt(jnp.mean(d * d, axis=0, keepdims=True) + SUBLN_EPS)
        o_ref[...] = ((y.T * g_ref[...]) * (1.0 - lam_init)).astype(o_ref.dtype)
    else:
        row = lax.broadcasted_iota(jnp.int32, (LANES, tq), 0)
        o_ref[...] = jnp.where(row < HEAD_DIM, o[:, :tq], o[:, tq:]).T.astype(o_ref.dtype)


def _attention(q, k, vt, bias, *, tq, tk, causal_walk, combine, out_dtype,
               lam=None, g=None, lam_init=0.0):
    b, tq_total, width = q.shape
    tk_total = k.shape[1]
    groups = width // LANES
    n_near = bias.shape[0]
    nkb = tk_total // tk
    assert tq_total % tq == 0 and vt.shape == (groups, b * nkb, VT_ROWS, tk)
    nq = tq_total // tq
    if causal_walk:
        grid = (b, groups, nq)
        at = lambda f: f
    else:
        grid = (groups, b, nq)
        at = lambda f: lambda h, bi, qi: f(bi, h, qi)
    in_specs = [
        pl.BlockSpec((None, tq, LANES), at(lambda bi, h, qi: (bi, qi, h))),
        pl.BlockSpec((None, tk_total, LANES), at(lambda bi, h, qi: (bi, 0, h))),
        pl.BlockSpec((None, nkb, VT_ROWS, tk), at(lambda bi, h, qi: (h, bi, 0, 0))),
        pl.BlockSpec((n_near, None, tk, 2 * tq), at(lambda bi, h, qi: (0, h, 0, 0))),
    ]
    args = [q, k, vt, bias]
    if causal_walk:
        in_specs.insert(1, pl.BlockSpec((None, tq, LANES),
                                        lambda bi, h, qi: (bi, jnp.minimum(qi + 1, nq - 1), h)))
        args.insert(1, q)
    if combine == "diff":
        in_specs = [pl.BlockSpec(memory_space=pltpu.SMEM)] + in_specs
        in_specs.append(pl.BlockSpec((1, LANES), lambda *_: (0, 0)))
        args = [lam.reshape(1).astype(F32)] + args + [g.reshape(1, LANES).astype(F32)]
    scratch = [pltpu.VMEM((1, 2 * tq), F32), pltpu.VMEM((VT_ROWS, 2 * tq), F32)]
    if causal_walk:
        scratch.append(pltpu.VMEM((GROUP_TILES * tk, 2 * tq), F32))
        scratch.append(pltpu.VMEM((GROUP_HALVES, 1, 2 * tq), F32))
    return pl.pallas_call(
        functools.partial(_attn_kernel, tq=tq, tk=tk, n_near=n_near, causal_walk=causal_walk,
                          combine=combine, lam_init=lam_init),
        grid=grid,
        in_specs=in_specs,
        out_specs=pl.BlockSpec((None, tq, LANES), at(lambda bi, h, qi: (bi, qi, h))),
        out_shape=jax.ShapeDtypeStruct((b, tq_total, width), out_dtype),
        scratch_shapes=scratch,
        compiler_params=pltpu.CompilerParams(
            dimension_semantics=("arbitrary", "arbitrary", "arbitrary"),
            vmem_limit_bytes=VMEM_LIMIT),
        name="attn_" + combine + ("_walk" if causal_walk else "_flat"),
    )(*args)


def _transpose_values(v, tk):
    b, t, width = v.shape
    vt = v.reshape(b * (t // tk), tk, width // LANES, LANES).transpose(2, 0, 3, 1)
    ones = jnp.broadcast_to(_ones_rows(tk), vt.shape[:2] + (VT_ROWS - LANES, tk))
    return jnp.concatenate([vt, ones], axis=2)


def _rms(x, g, eps):
    return x * lax.rsqrt(jnp.mean(x * x, axis=-1, keepdims=True) + eps) * g


def _post_kernel(x_ref, oa_ref, ob_ref, p_ref, gb_ref, gm_ref, gf_ref, wout_ref, wup_ref,
                 wdown_ref, wgate_ref, wple_ref, y_ref):
    obn = _rms(ob_ref[...], gb_ref[...], EPS).astype(BF16)
    h = x_ref[...] + (
        jnp.dot(oa_ref[...], wout_ref[:WIDTH_A, :], preferred_element_type=F32)
        + jnp.dot(obn, wout_ref[WIDTH_A:, :], preferred_element_type=F32))
    c = _rms(h, gm_ref[...], EPS).astype(BF16)
    d_ff = wup_ref.shape[1]
    step = wup_ref.shape[0]
    for j in range(d_ff // step):
        u = jnp.dot(c, wup_ref[:, j * step:(j + 1) * step], preferred_element_type=F32)
        u = jnp.square(jnp.maximum(u, 0.0)).astype(BF16)
        h = h + jnp.dot(u, wdown_ref[j * step:(j + 1) * step, :], preferred_element_type=F32)
    gate = jax.nn.sigmoid(jnp.dot(h.astype(BF16), wgate_ref[...], preferred_element_type=F32))
    h = h + gate * jnp.dot(p_ref[...].astype(BF16), wple_ref[...], preferred_element_type=F32)
    y_ref[...] = _rms(h, gf_ref[...], EPS)


def _post(x, oa, ob, p, gb, gm, gf, wout, wup, wdown, wgate, wple):
    n, d = x.shape
    tm = min(ROW_TILE, n)
    assert n % tm == 0
    row = lambda i: (i, 0)
    fixed = lambda a: pl.BlockSpec(a.shape, lambda i: (0,) * a.ndim, pipeline_mode=pl.Buffered(1))
    vec = lambda a: a.reshape(1, -1).astype(F32)
    gb, gm, gf = vec(gb), vec(gm), vec(gf)
    return pl.pallas_call(
        _post_kernel,
        grid=(n // tm,),
        in_specs=[pl.BlockSpec((tm, d), row), pl.BlockSpec((tm, oa.shape[1]), row),
                  pl.BlockSpec((tm, ob.shape[1]), row), pl.BlockSpec((tm, p.shape[1]), row),
                  fixed(gb), fixed(gm), fixed(gf), fixed(wout), fixed(wup), fixed(wdown),
                  fixed(wgate), fixed(wple)],
        out_specs=pl.BlockSpec((tm, d), row),
        out_shape=jax.ShapeDtypeStruct((n, d), F32),
        compiler_params=pltpu.CompilerParams(dimension_semantics=("arbitrary",),
                                             vmem_limit_bytes=VMEM_LIMIT),
        name="post",
    )(x, oa, ob, p, gb, gm, gf, wout, wup, wdown, wgate, wple)


def _pad_rows(a, rows):
    return jnp.pad(a, ((0, 0), (0, rows - a.shape[1]), (0, 0)))


def _sample_bias(stream_vec_fn, n_keys, tk, tq, first_kpos, past, band):
    def visible(t, a, b):
        kpos = first_kpos + a
        kc = kpos // CHUNK
        qc = (past + b) // CHUNK
        vis = (kc <= qc) & (a < n_keys)
        if band:
            vis = vis & (kc >= qc - BAND_CHUNKS) & (kpos >= 0)
        return vis

    return _bias_tiles(stream_vec_fn, tk, tq, (first_kpos - past,), visible)


def kernel(x_prompt, x_sample, cache_a_k, cache_a_v, cache_b_k, cache_b_v, p_prompt, p_sample,
           t5_table, g_attn, w_in, lambda_q1, lambda_k1, lambda_q2, lambda_k2, subln_g,
           band_table, out_norm_b, w_out, g_mlp, w_up, w_down, w_ple_gate, w_ple_proj, g_final):
    depth = w_in.shape[0]
    assert depth == 1
    i = 0
    B, S, D = x_prompt.shape
    Bs, Ts, _ = x_sample.shape
    past = cache_a_k.shape[2]
    cache_b_len = cache_b_k.shape[2]
    keep = min(WINDOW_B, S)
    assert keep == ROW_TILE and S % ATTN_BLOCK_Q == 0 and Bs * Ts == ROW_TILE

    lam_init = 0.8 - 0.6 * math.exp(-0.3 * i)
    lam = (jnp.exp(jnp.sum(lambda_q1[i].astype(F32) * lambda_k1[i].astype(F32)))
           - jnp.exp(jnp.sum(lambda_q2[i].astype(F32) * lambda_k2[i].astype(F32)))
           + lam_init)

    w_in_bf = w_in[i].astype(BF16)
    post_w = (out_norm_b[i], g_mlp[i], g_final, w_out[i].astype(BF16), w_up[i].astype(BF16),
              w_down[i].astype(BF16), w_ple_gate[i].astype(BF16), w_ple_proj[i].astype(BF16))
    tk_p, tq_p = ATTN_BLOCK_K, ATTN_BLOCK_Q

    xp = x_prompt.reshape(B * S, D)
    (qa, ka32, ka16, va32, vta, qb, kb16, vtb, kb32, vb32) = _project(
        xp, g_attn[i], w_in_bf, tail_period=S // ROW_TILE, time_major_k=True)
    to3 = lambda a: a.reshape(B, S, a.shape[-1])
    oa = _attention(to3(qa), to3(ka16), vta, _diff_prompt_bias(t5_table), tq=tq_p, tk=tk_p,
                    causal_walk=True, combine="diff", out_dtype=BF16, lam=lam, g=subln_g[i],
                    lam_init=lam_init)
    ob = _attention(to3(qb), to3(kb16), vtb, _band_prompt_bias(band_table[i]), tq=tq_p, tk=tk_p,
                    causal_walk=True, combine="band", out_dtype=F32)
    y_prompt = _post(xp, oa.reshape(B * S, WIDTH_A), ob.reshape(B * S, WIDTH_B),
                     p_prompt[i].reshape(B * S, -1), *post_w).reshape(B, S, D)

    xs = x_sample.reshape(Bs * Ts, D)
    (qa_s, ka32_s, ka16_s, va32_s, _, qb_s, kb16_s, _, kb32_s, vb32_s) = _project(
        xs, g_attn[i], w_in_bf, tail_period=1)
    to3s = lambda a: a.reshape(Bs, Ts, -1)
    tq_s = LANES
    assert Ts <= tq_s

    def sample_keys(cache, new):
        c = cache.reshape(Bs, cache.shape[1], -1).astype(BF16)
        allk = jnp.concatenate([c, to3s(new).astype(BF16)], axis=1)
        n_keys = allk.shape[1]
        tk = -(-n_keys // LANES) * LANES
        return _pad_rows(allk, tk), n_keys, tk

    ka_all, n_a, tk_a = sample_keys(cache_a_k[i], ka16_s)
    va_all, _, _ = sample_keys(cache_a_v[i], va32_s)
    kb_all, n_b, tk_b = sample_keys(cache_b_k[i], kb16_s)
    vb_all, _, _ = sample_keys(cache_b_v[i], vb32_s)

    bias_a = _sample_bias(_diff_streams(t5_table), n_a, tk_a, tq_s, 0, past, band=False)
    bias_b = _sample_bias(_band_streams(band_table[i]), n_b, tk_b, tq_s,
                          past - cache_b_len, past, band=True)

    oa_s = _attention(_pad_rows(to3s(qa_s), tq_s), ka_all, _transpose_values(va_all, tk_a),
                      bias_a, tq=tq_s, tk=tk_a, causal_walk=False, combine="diff",
                      out_dtype=BF16, lam=lam, g=subln_g[i], lam_init=lam_init)[:, :Ts]
    ob_s = _attention(_pad_rows(to3s(qb_s), tq_s), kb_all, _transpose_values(vb_all, tk_b),
                      bias_b, tq=tq_s, tk=tk_b, causal_walk=False, combine="band",
                      out_dtype=F32)[:, :Ts]
    y_sample = _post(xs, oa_s.reshape(Bs * Ts, WIDTH_A), ob_s.reshape(Bs * Ts, WIDTH_B),
                     p_sample[i].reshape(Bs * Ts, -1), *post_w).reshape(Bs, Ts, D)

    return (y_prompt, y_sample,
            ka32.transpose(0, 4, 1, 2, 3)[None],
            va32.reshape(1, B, S, N_HEADS_A, 2 * HEAD_DIM),
            kb32.reshape(1, B, keep, N_HEADS_B, HEAD_DIM),
            vb32.reshape(1, B, keep, N_HEADS_B, HEAD_DIM),
            ka32_s.reshape(1, Bs, Ts, N_HEADS_A, 2, HEAD_DIM),
            va32_s.reshape(1, Bs, Ts, N_HEADS_A, 2 * HEAD_DIM),
            kb32_s.reshape(1, Bs, Ts, N_HEADS_B, HEAD_DIM),
            vb32_s.reshape(1, Bs, Ts, N_HEADS_B, HEAD_DIM))
```

```python
import functools
import math

import jax
import jax.numpy as jnp
from jax import lax
from jax.experimental import pallas as pl
from jax.experimental.pallas import tpu as pltpu

CHUNK = 64
HEAD_DIM = 64
N_HEADS_A = 4
N_HEADS_B = 8
WIDTH_A = N_HEADS_A * 2 * HEAD_DIM
WIDTH_B = N_HEADS_B * HEAD_DIM
N_BUCKETS = 32
MAX_DISTANCE = 128
BAND_CHUNKS = 8
WINDOW_B = BAND_CHUNKS * CHUNK
REL_CLIP = 128
EPS = 1e-6
SUBLN_EPS = 1e-5
MASKED = -1e30
LOG2E = math.log2(math.e)
Q_SCALE = LOG2E * HEAD_DIM ** -0.5

LANES = 128
VT_ROWS = LANES + 16
ATTN_BLOCK_K = 256
ATTN_BLOCK_Q = 512
GROUP_TILES = 4
GROUP_HALVES = 2
ROW_TILE = 512
CAST_ROWS = 256
VMEM_LIMIT = 56 * 1024 * 1024

F32 = jnp.float32
BF16 = jnp.bfloat16


def _t5_bucket(rel):
    half = N_BUCKETS // 2
    n = -rel
    ret = jnp.where(n < 0, half, 0)
    n = jnp.abs(n)
    max_exact = half // 2
    nf = jnp.maximum(n, 1).astype(F32)
    large = max_exact + (jnp.log(nf / max_exact) / math.log(MAX_DISTANCE / max_exact)
                         * (half - max_exact)).astype(jnp.int32)
    large = jnp.minimum(large, half - 1)
    return ret + jnp.where(n < max_exact, n, large)


def _bias_kernel(v_ref, o_ref, *, tk, tq, lane_starts, visible):
    a = lax.broadcasted_iota(jnp.int32, (tk, tq), 0)
    b = lax.broadcasted_iota(jnp.int32, (tk, tq), 1)
    for s in range(2):
        x = jnp.broadcast_to(v_ref[s], (tk, v_ref.shape[-1]))
        x = pltpu.roll(x, 0, 1, stride=1, stride_axis=0)
        for t, start in enumerate(lane_starts):
            tile = x[:, start:start + tq]
            vis = visible(t, a, b)
            o_ref[t, :, s * tq:(s + 1) * tq] = tile if vis is None else jnp.where(vis, tile, MASKED)


def _bias_tiles(stream_vec_fn, tk, tq, offsets, visible):
    first_start = -(-tk // LANES) * LANES
    r0 = first_start + max(offsets)
    lane_starts = tuple(r0 - off for off in offsets)
    assert all(s % LANES == 0 for s in lane_starts) and tq % LANES == 0
    n_lanes = max(lane_starts) + tq
    v = stream_vec_fn(r0 - jnp.arange(n_lanes)).astype(F32) * LOG2E
    groups = v.shape[0]
    return pl.pallas_call(
        functools.partial(_bias_kernel, tk=tk, tq=tq, lane_starts=lane_starts, visible=visible),
        grid=(groups,),
        in_specs=[pl.BlockSpec((None, 2, 1, n_lanes), lambda g: (g, 0, 0, 0))],
        out_specs=pl.BlockSpec((len(offsets), None, tk, 2 * tq), lambda g: (0, g, 0, 0)),
        out_shape=jax.ShapeDtypeStruct((len(offsets), groups, tk, 2 * tq), F32),
        compiler_params=pltpu.CompilerParams(dimension_semantics=("arbitrary",),
                                             vmem_limit_bytes=VMEM_LIMIT),
        name="bias_tiles",
    )(v.reshape(groups, 2, 1, n_lanes))


def _diff_streams(table, shift=None):
    def fn(rel):
        v = jnp.transpose(table[_t5_bucket(rel)].astype(F32))
        if shift is not None:
            v = v - shift[:, None]
        return jnp.stack([v, v], axis=1)
    return fn


def _band_streams(table):
    def fn(rel):
        v = jnp.transpose(table[jnp.clip(rel, -REL_CLIP, REL_CLIP) + REL_CLIP].astype(F32))
        return v.reshape(N_HEADS_B // 2, 2, -1)
    return fn


def _diff_prompt_bias(t5_table):
    tk, tq = ATTN_BLOCK_K, ATTN_BLOCK_Q
    assert tk >= MAX_DISTANCE
    far = t5_table[_t5_bucket(jnp.array(-2 * MAX_DISTANCE))].astype(F32)
    offsets = _near_offsets(tk, tq, reach=tk)
    visible = lambda t, a, b: None if offsets[t] < 0 else (a + offsets[t]) // CHUNK <= b // CHUNK
    return _bias_tiles(_diff_streams(t5_table, far), tk, tq, offsets, visible)


def _band_prompt_bias(band_table):
    tk, tq = ATTN_BLOCK_K, ATTN_BLOCK_Q
    offsets = _near_offsets(tk, tq, reach=WINDOW_B)

    def visible(t, a, b):
        kc = (a + offsets[t]) // CHUNK
        qc = b // CHUNK
        return (kc <= qc) & (kc >= qc - BAND_CHUNKS)

    return _bias_tiles(_band_streams(band_table), tk, tq, offsets, visible)


def _near_offsets(tk, tq, reach):
    assert tq % tk == 0 and reach % tk == 0
    return tuple(range(tq - tk, -reach - 1, -tk))


def _store_transposed(vt_ref, v):
    blk = vt_ref.shape[-1]
    ones_rows = _ones_rows(blk)
    for h in range(vt_ref.shape[0]):
        for c in range(vt_ref.shape[1]):
            vt_ref[h, c, :LANES] = v[c * blk:(c + 1) * blk, h * LANES:(h + 1) * LANES].T.astype(BF16)
            vt_ref[h, c, LANES:] = ones_rows


def _ones_rows(n):
    row = lax.broadcasted_iota(jnp.int32, (VT_ROWS - LANES, n), 0)
    return jnp.where(row == 0, 1.0, 0.0).astype(BF16)


def _proj_kernel(x_ref, g_ref, w_ref, qa_ref, ka32_ref, ka16_ref, va32_ref, vta_ref,
                 qb_ref, kb16_ref, vtb_ref, kb32_ref, vb32_ref, *, tail_period):
    x = x_ref[...]
    xn = x * lax.rsqrt(jnp.mean(x * x, axis=-1, keepdims=True) + EPS) * g_ref[...]
    xn = xn.astype(BF16)

    def cols(c):
        return jnp.dot(xn, w_ref[:, c * WIDTH_A:(c + 1) * WIDTH_A], preferred_element_type=F32)

    qa_ref[...] = (cols(0) * Q_SCALE).astype(BF16)
    ka = cols(1)
    if len(ka32_ref.shape) == 2:
        ka32_ref[...] = ka
    else:
        for h in range(N_HEADS_A):
            t = ka[:, h * LANES:(h + 1) * LANES].T
            ka32_ref[h, 0] = t[:HEAD_DIM]
            ka32_ref[h, 1] = t[HEAD_DIM:]
    ka16_ref[...] = ka.astype(BF16)
    va = cols(2)
    for h in range(N_HEADS_A):
        va32_ref[:, h, :] = va[:, h * LANES:(h + 1) * LANES]
    _store_transposed(vta_ref, va)
    qb_ref[...] = (cols(3) * Q_SCALE).astype(BF16)
    kb = cols(4)
    kb16_ref[...] = kb.astype(BF16)
    vb = cols(5)
    _store_transposed(vtb_ref, vb)

    @pl.when(pl.program_id(0) % tail_period == tail_period - 1)
    def _():
        kb32_ref[...] = kb
        vb32_ref[...] = vb


def _project(x, g, w_bf, tail_period, time_major_k=False):
    n, d = x.shape
    tm = ROW_TILE
    blk = ATTN_BLOCK_K
    assert n % (tm * tail_period) == 0 and tm % blk == 0
    groups = WIDTH_A // LANES
    row = lambda i: (i, 0)
    tail = lambda i: (i // tail_period, 0)
    rows = lambda dt, imap=row: (jax.ShapeDtypeStruct((n, WIDTH_A), dt),
                                 pl.BlockSpec((tm, WIDTH_A), imap))
    vt = (jax.ShapeDtypeStruct((groups, n // blk, VT_ROWS, blk), BF16),
          pl.BlockSpec((groups, tm // blk, VT_ROWS, blk), lambda i: (0, i, 0, 0)))
    n_tail = n // tail_period
    tails = (jax.ShapeDtypeStruct((n_tail, WIDTH_B), F32), pl.BlockSpec((tm, WIDTH_B), tail))
    if time_major_k:
        per_batch = tail_period * tm
        ka32 = (jax.ShapeDtypeStruct((n // per_batch, N_HEADS_A, 2, HEAD_DIM, per_batch), F32),
                pl.BlockSpec((None, N_HEADS_A, 2, HEAD_DIM, tm),
                             lambda i: (i // tail_period, 0, 0, 0, i % tail_period)))
    else:
        ka32 = rows(F32)
    va32 = (jax.ShapeDtypeStruct((n, N_HEADS_A, LANES), F32),
            pl.BlockSpec((tm, N_HEADS_A, LANES), lambda i: (i, 0, 0)))
    outs = [rows(BF16), ka32, rows(BF16), va32, vt,
            rows(BF16), rows(BF16), vt, tails, tails]
    return pl.pallas_call(
        functools.partial(_proj_kernel, tail_period=tail_period),
        grid=(n // tm,),
        in_specs=[pl.BlockSpec((tm, d), row),
                  pl.BlockSpec((1, d), lambda i: (0, 0)),
                  pl.BlockSpec(w_bf.shape, lambda i: (0, 0), pipeline_mode=pl.Buffered(1))],
        out_specs=[s for _, s in outs],
        out_shape=[s for s, _ in outs],
        compiler_params=pltpu.CompilerParams(dimension_semantics=("arbitrary",),
                                             vmem_limit_bytes=VMEM_LIMIT),
        name="in_proj",
    )(x, g.reshape(1, d), w_bf)


def _attn_kernel(*refs, tq, tk, n_near, causal_walk, combine, lam_init):
    far_walk = causal_walk and combine == "diff"
    refs = list(refs)
    lam_ref = refs.pop(0) if combine == "diff" else None
    q_ref = refs.pop(0)
    q_next_ref = refs.pop(0) if causal_walk else None
    k_ref, vt_ref, bias_ref = refs[:3]
    del refs[:3]
    g_ref = refs.pop(0) if combine == "diff" else None
    o_ref, m_ref, acc_ref = refs[:3]
    qi = pl.program_id(2)

    def two_streams(ref):
        q = ref[...]
        lane = lax.broadcasted_iota(jnp.int32, q.shape, 1)
        zero = jnp.zeros_like(q)
        return jnp.concatenate([jnp.where(lane < HEAD_DIM, q, zero),
                                jnp.where(lane >= HEAD_DIM, q, zero)], axis=0)

    def scores(qz, kb):
        start = kb * tk if isinstance(kb, int) else pl.multiple_of(kb * tk, tk)
        k = k_ref[pl.ds(start, tk), :]
        return lax.dot_general(k, qz, (((1,), (1,)), ((), ())),
                               preferred_element_type=F32)

    if not causal_walk:
        qz = two_streams(q_ref)
        m = jnp.full(m_ref.shape, MASKED, F32)
        acc = jnp.zeros(acc_ref.shape, F32)
        for t in range(n_near):
            s = scores(qz, t) + bias_ref[t]
            m_new = jnp.maximum(m, jnp.max(s, axis=0, keepdims=True))
            alpha = jnp.exp2(m - m_new)
            p = jnp.exp2(s - m_new)
            acc = alpha * acc + jnp.dot(vt_ref[t], p.astype(BF16), preferred_element_type=F32)
            m = m_new
        m_ref[...] = m
        acc_ref[...] = acc
    else:
        sbuf_ref, gmax_ref = refs[-2:]
        group = GROUP_TILES
        half = group // GROUP_HALVES

        class Walk:
            def __init__(self, ref, tile):
                self.qz = two_streams(ref)
                self.n_visible = (tq // tk) * (tile + 1)
                self.n_tiles = (self.n_visible if far_walk
                                else jnp.minimum(self.n_visible, n_near))

            def key_tile(self, n):
                return jnp.maximum(self.n_visible - 1 - n, 0)

        this = Walk(q_ref, qi)
        following = Walk(q_next_ref, jnp.minimum(qi + 1, pl.num_programs(2) - 1))

        def store_scores(walk, g, t, hmax, masked):
            n = g * group + t
            s = scores(walk.qz, walk.key_tile(n))
            if isinstance(g, int) and n < n_near:
                s = s + bias_ref[n]
            if masked:
                s = jnp.where(n < walk.n_tiles, s, MASKED)
            sbuf_ref[t * tk:(t + 1) * tk, :] = s
            mt = jnp.max(s, axis=0, keepdims=True)
            return mt if hmax is None else jnp.maximum(hmax, mt)

        assert 2 * (tq // tk) >= group

        def first_group(walk, t, hmax):
            return store_scores(walk, 0, t, hmax, walk is this)

        def group_step(g, produce):
            m = m_ref[...]
            acc = acc_ref[...]
            for h in range(GROUP_HALVES):
                m_new = jnp.maximum(m, gmax_ref[h])
                alpha = jnp.exp2(m - m_new)
                pv = None
                hmax = None
                for t in range(h * half, (h + 1) * half):
                    s = sbuf_ref[t * tk:(t + 1) * tk, :]
                    if produce == "following":
                        hmax = first_group(following, t, hmax)
                    else:
                        hmax = store_scores(this, g + 1, t, hmax, produce == "masked")
                    p = jnp.exp2(s - m_new)
                    d = jnp.dot(vt_ref[this.key_tile(g * group + t)], p.astype(BF16),
                                preferred_element_type=F32)
                    pv = d if pv is None else pv + d
                gmax_ref[h] = hmax
                acc = alpha * acc + pv
                m = m_new
            acc_ref[...] = acc
            m_ref[...] = m

        m_ref[...] = jnp.full(m_ref.shape, MASKED, F32)
        acc_ref[...] = jnp.zeros(acc_ref.shape, F32)

        @pl.when(qi == 0)
        def _():
            for h in range(GROUP_HALVES):
                hmax = None
                for t in range(h * half, (h + 1) * half):
                    hmax = first_group(this, t, hmax)
                gmax_ref[h] = hmax

        if far_walk:
            n_groups = (this.n_tiles + group - 1) // group
            n_plain = jnp.maximum(n_groups - 2, 0)

            def plain_steps(first, count):
                for j in range(count):
                    group_step(first + j, "plain")

            def quad(i, carry):
                plain_steps(4 * i, 4)
                return carry

            lax.fori_loop(0, n_plain // 4, quad, 0)
            done = (n_plain // 4) * 4

            @pl.when(n_plain - done >= 2)
            def _():
                plain_steps(done, 2)

            @pl.when(n_plain % 2 == 1)
            def _():
                plain_steps(n_plain - 1, 1)

            @pl.when(n_groups >= 2)
            def _():
                group_step(n_groups - 2, "masked")
                group_step(n_groups - 1, "following")

            @pl.when(n_groups < 2)
            def _():
                group_step(n_groups - 1, "following")
        else:
            assert n_near <= group
            group_step(0, "following")

    o = acc_ref[:LANES, :] * (1.0 / acc_ref[LANES:LANES + 1, :])
    if combine == "diff":
        d = o[:, :tq] - lam_ref[0] * o[:, tq:]
        y = d * lax.rsqrt(jnp.mean(d * d, axis=0, keepdims=True) + SUBLN_EPS)
        o_ref[...] = ((y.T * g_ref[...]) * (1.0 - lam_init)).astype(o_ref.dtype)
    else:
        row = lax.broadcasted_iota(jnp.int32, (LANES, tq), 0)
        o_ref[...] = jnp.where(row < HEAD_DIM, o[:, :tq], o[:, tq:]).T.astype(o_ref.dtype)


def _attention(q, k, vt, bias, *, tq, tk, causal_walk, combine, out_dtype,
               lam=None, g=None, lam_init=0.0):
    b, tq_total, width = q.shape
    tk_total = k.shape[1]
    groups = width // LANES
    n_near = bias.shape[0]
    nkb = tk_total // tk
    assert tq_total % tq == 0 and vt.shape == (groups, b * nkb, VT_ROWS, tk)
    nq = tq_total // tq
    if causal_walk:
        grid = (b, groups, nq)
        at = lambda f: f
    else:
        grid = (groups, b, nq)
        at = lambda f: lambda h, bi, qi: f(bi, h, qi)
    in_specs = [
        pl.BlockSpec((None, tq, LANES), at(lambda bi, h, qi: (bi, qi, h))),
        pl.BlockSpec((None, tk_total, LANES), at(lambda bi, h, qi: (bi, 0, h))),
        pl.BlockSpec((None, nkb, VT_ROWS, tk), at(lambda bi, h, qi: (h, bi, 0, 0))),
        pl.BlockSpec((n_near, None, tk, 2 * tq), at(lambda bi, h, qi: (0, h, 0, 0))),
    ]
    args = [q, k, vt, bias]
    if causal_walk:
        in_specs.insert(1, pl.BlockSpec((None, tq, LANES),
                                        lambda bi, h, qi: (bi, jnp.minimum(qi + 1, nq - 1), h)))
        args.insert(1, q)
    if combine == "diff":
        in_specs = [pl.BlockSpec(memory_space=pltpu.SMEM)] + in_specs
        in_specs.append(pl.BlockSpec((1, LANES), lambda *_: (0, 0)))
        args = [lam.reshape(1).astype(F32)] + args + [g.reshape(1, LANES).astype(F32)]
    scratch = [pltpu.VMEM((1, 2 * tq), F32), pltpu.VMEM((VT_ROWS, 2 * tq), F32)]
    if causal_walk:
        scratch.append(pltpu.VMEM((GROUP_TILES * tk, 2 * tq), F32))
        scratch.append(pltpu.VMEM((GROUP_HALVES, 1, 2 * tq), F32))
    return pl.pallas_call(
        functools.partial(_attn_kernel, tq=tq, tk=tk, n_near=n_near, causal_walk=causal_walk,
                          combine=combine, lam_init=lam_init),
        grid=grid,
        in_specs=in_specs,
        out_specs=pl.BlockSpec((None, tq, LANES), at(lambda bi, h, qi: (bi, qi, h))),
        out_shape=jax.ShapeDtypeStruct((b, tq_total, width), out_dtype),
        scratch_shapes=scratch,
        compiler_params=pltpu.CompilerParams(
            dimension_semantics=("arbitrary", "arbitrary", "arbitrary"),
            vmem_limit_bytes=VMEM_LIMIT),
        name="attn_" + combine + ("_walk" if causal_walk else "_flat"),
    )(*args)


def _transpose_values(v, tk):
    b, t, width = v.shape
    vt = v.reshape(b * (t // tk), tk, width // LANES, LANES).transpose(2, 0, 3, 1)
    ones = jnp.broadcast_to(_ones_rows(tk), vt.shape[:2] + (VT_ROWS - LANES, tk))
    return jnp.concatenate([vt, ones], axis=2)


def _rms(x, g, eps):
    return x * lax.rsqrt(jnp.mean(x * x, axis=-1, keepdims=True) + eps) * g


def _post_kernel(x_ref, oa_ref, ob_ref, p_ref, gb_ref, gm_ref, gf_ref, wout_ref, wup_ref,
                 wdown_ref, wgate_ref, wple_ref, y_ref):
    obn = _rms(ob_ref[...], gb_ref[...], EPS).astype(BF16)
    h = x_ref[...] + (
        jnp.dot(oa_ref[...], wout_ref[:WIDTH_A, :], preferred_element_type=F32)
        + jnp.dot(obn, wout_ref[WIDTH_A:, :], preferred_element_type=F32))
    c = _rms(h, gm_ref[...], EPS).astype(BF16)
    d_ff = wup_ref.shape[1]
    step = wup_ref.shape[0]
    for j in range(d_ff // step):
        u = jnp.dot(c, wup_ref[:, j * step:(j + 1) * step], preferred_element_type=F32)
        u = jnp.square(jnp.maximum(u, 0.0)).astype(BF16)
        h = h + jnp.dot(u, wdown_ref[j * step:(j + 1) * step, :], preferred_element_type=F32)
    gate = jax.nn.sigmoid(jnp.dot(h.astype(BF16), wgate_ref[...], preferred_element_type=F32))
    h = h + gate * jnp.dot(p_ref[...].astype(BF16), wple_ref[...], preferred_element_type=F32)
    y_ref[...] = _rms(h, gf_ref[...], EPS)


def _post(x, oa, ob, p, gb, gm, gf, wout, wup, wdown, wgate, wple):
    n, d = x.shape
    tm = min(ROW_TILE, n)
    assert n % tm == 0
    row = lambda i: (i, 0)
    fixed = lambda a: pl.BlockSpec(a.shape, lambda i: (0,) * a.ndim, pipeline_mode=pl.Buffered(1))
    vec = lambda a: a.reshape(1, -1).astype(F32)
    gb, gm, gf = vec(gb), vec(gm), vec(gf)
    return pl.pallas_call(
        _post_kernel,
        grid=(n // tm,),
        in_specs=[pl.BlockSpec((tm, d), row), pl.BlockSpec((tm, oa.shape[1]), row),
                  pl.BlockSpec((tm, ob.shape[1]), row), pl.BlockSpec((tm, p.shape[1]), row),
                  fixed(gb), fixed(gm), fixed(gf), fixed(wout), fixed(wup), fixed(wdown),
                  fixed(wgate), fixed(wple)],
        out_specs=pl.BlockSpec((tm, d), row),
        out_shape=jax.ShapeDtypeStruct((n, d), F32),
        compiler_params=pltpu.CompilerParams(dimension_semantics=("arbitrary",),
                                             vmem_limit_bytes=VMEM_LIMIT),
        name="post",
    )(x, oa, ob, p, gb, gm, gf, wout, wup, wdown, wgate, wple)


def _cast_kernel(x_ref, o_ref):
    o_ref[...] = x_ref[...].astype(o_ref.dtype)


def _to_bf16(w):
    rows, cols = w.shape
    tr = min(rows, CAST_ROWS)
    assert rows % tr == 0
    block = pl.BlockSpec((tr, cols), lambda i: (i, 0))
    return pl.pallas_call(
        _cast_kernel, grid=(rows // tr,), in_specs=[block], out_specs=block,
        out_shape=jax.ShapeDtypeStruct(w.shape, BF16),
        compiler_params=pltpu.CompilerParams(dimension_semantics=("arbitrary",),
                                             vmem_limit_bytes=VMEM_LIMIT),
        name="to_bf16",
    )(w)


def _pad_rows(a, rows):
    return jnp.pad(a, ((0, 0), (0, rows - a.shape[1]), (0, 0)))


def _sample_bias(stream_vec_fn, n_keys, tk, tq, first_kpos, past, band):
    def visible(t, a, b):
        kpos = first_kpos + a
        kc = kpos // CHUNK
        qc = (past + b) // CHUNK
        vis = (kc <= qc) & (a < n_keys)
        if band:
            vis = vis & (kc >= qc - BAND_CHUNKS) & (kpos >= 0)
        return vis

    return _bias_tiles(stream_vec_fn, tk, tq, (first_kpos - past,), visible)


def kernel(x_prompt, x_sample, cache_a_k, cache_a_v, cache_b_k, cache_b_v, p_prompt, p_sample,
           t5_table, g_attn, w_in, lambda_q1, lambda_k1, lambda_q2, lambda_k2, subln_g,
           band_table, out_norm_b, w_out, g_mlp, w_up, w_down, w_ple_gate, w_ple_proj, g_final):
    depth = w_in.shape[0]
    assert depth == 1
    i = 0
    B, S, D = x_prompt.shape
    Bs, Ts, _ = x_sample.shape
    past = cache_a_k.shape[2]
    cache_b_len = cache_b_k.shape[2]
    keep = min(WINDOW_B, S)
    assert keep == ROW_TILE and S % ATTN_BLOCK_Q == 0 and Bs * Ts == ROW_TILE

    lam_init = 0.8 - 0.6 * math.exp(-0.3 * i)
    lam = (jnp.exp(jnp.sum(lambda_q1[i].astype(F32) * lambda_k1[i].astype(F32)))
           - jnp.exp(jnp.sum(lambda_q2[i].astype(F32) * lambda_k2[i].astype(F32)))
           + lam_init)

    w_in_bf = _to_bf16(w_in[i])
    post_w = (out_norm_b[i], g_mlp[i], g_final, _to_bf16(w_out[i]), _to_bf16(w_up[i]),
              _to_bf16(w_down[i]), _to_bf16(w_ple_gate[i]), _to_bf16(w_ple_proj[i]))
    tk_p, tq_p = ATTN_BLOCK_K, ATTN_BLOCK_Q

    xp = x_prompt.reshape(B * S, D)
    (qa, ka32, ka16, va32, vta, qb, kb16, vtb, kb32, vb32) = _project(
        xp, g_attn[i], w_in_bf, tail_period=S // ROW_TILE, time_major_k=True)
    to3 = lambda a: a.reshape(B, S, a.shape[-1])
    oa = _attention(to3(qa), to3(ka16), vta, _diff_prompt_bias(t5_table), tq=tq_p, tk=tk_p,
                    causal_walk=True, combine="diff", out_dtype=BF16, lam=lam, g=subln_g[i],
                    lam_init=lam_init)
    ob = _attention(to3(qb), to3(kb16), vtb, _band_prompt_bias(band_table[i]), tq=tq_p, tk=tk_p,
                    causal_walk=True, combine="band", out_dtype=F32)
    y_prompt = _post(xp, oa.reshape(B * S, WIDTH_A), ob.reshape(B * S, WIDTH_B),
                     p_prompt[i].reshape(B * S, -1), *post_w).reshape(B, S, D)

    xs = x_sample.reshape(Bs * Ts, D)
    (qa_s, ka32_s, ka16_s, va32_s, _, qb_s, kb16_s, _, kb32_s, vb32_s) = _project(
        xs, g_attn[i], w_in_bf, tail_period=1)
    to3s = lambda a: a.reshape(Bs, Ts, -1)
    tq_s = LANES
    assert Ts <= tq_s

    def sample_keys(cache, new):
        c = cache.reshape(Bs, cache.shape[1], -1).astype(BF16)
        allk = jnp.concatenate([c, to3s(new).astype(BF16)], axis=1)
        n_keys = allk.shape[1]
        tk = -(-n_keys // LANES) * LANES
        return _pad_rows(allk, tk), n_keys, tk

    ka_all, n_a, tk_a = sample_keys(cache_a_k[i], ka16_s)
    va_all, _, _ = sample_keys(cache_a_v[i], va32_s)
    kb_all, n_b, tk_b = sample_keys(cache_b_k[i], kb16_s)
    vb_all, _, _ = sample_keys(cache_b_v[i], vb32_s)

    bias_a = _sample_bias(_diff_streams(t5_table), n_a, tk_a, tq_s, 0, past, band=False)
    bias_b = _sample_bias(_band_streams(band_table[i]), n_b, tk_b, tq_s,
                          past - cache_b_len, past, band=True)

    oa_s = _attention(_pad_rows(to3s(qa_s), tq_s), ka_all, _transpose_values(va_all, tk_a),
                      bias_a, tq=tq_s, tk=tk_a, causal_walk=False, combine="diff",
                      out_dtype=BF16, lam=lam, g=subln_g[i], lam_init=lam_init)[:, :Ts]
    ob_s = _attention(_pad_rows(to3s(qb_s), tq_s), kb_all, _transpose_values(vb_all, tk_b),
                      bias_b, tq=tq_s, tk=tk_b, causal_walk=False, combine="band",
                      out_dtype=F32)[:, :Ts]
    y_sample = _post(xs, oa_s.reshape(Bs * Ts, WIDTH_A), ob_s.reshape(Bs * Ts, WIDTH_B),
                     p_sample[i].reshape(Bs * Ts, -1), *post_w).reshape(Bs, Ts, D)

    return (y_prompt, y_sample,
            ka32.transpose(0, 4, 1, 2, 3)[None],
            va32.reshape(1, B, S, N_HEADS_A, 2 * HEAD_DIM),
            kb32.reshape(1, B, keep, N_HEADS_B, HEAD_DIM),
            vb32.reshape(1, B, keep, N_HEADS_B, HEAD_DIM),
            ka32_s.reshape(1, Bs, Ts, N_HEADS_A, 2, HEAD_DIM),
            va32_s.reshape(1, Bs, Ts, N_HEADS_A, 2 * HEAD_DIM),
            kb32_s.reshape(1, Bs, Ts, N_HEADS_B, HEAD_DIM),
            vb32_s.reshape(1, Bs, Ts, N_HEADS_B, HEAD_DIM))
```

```python
import functools
import math

import jax
import jax.numpy as jnp
from jax import lax
from jax.experimental import pallas as pl
from jax.experimental.pallas import tpu as pltpu

CHUNK = 64
HEAD_DIM = 64
N_HEADS_A = 4
N_HEADS_B = 8
WIDTH_A = N_HEADS_A * 2 * HEAD_DIM
WIDTH_B = N_HEADS_B * HEAD_DIM
N_BUCKETS = 32
MAX_DISTANCE = 128
BAND_CHUNKS = 8
WINDOW_B = BAND_CHUNKS * CHUNK
REL_CLIP = 128
EPS = 1e-6
SUBLN_EPS = 1e-5
MASKED = -1e30
LOG2E = math.log2(math.e)
Q_SCALE = LOG2E * HEAD_DIM ** -0.5

LANES = 128
VT_ROWS = LANES + 16
ATTN_BLOCK_K = 256
ATTN_BLOCK_Q = 512
GROUP_TILES = 4
GROUP_HALVES = 1
ROW_TILE = 512
VMEM_LIMIT = 56 * 1024 * 1024

F32 = jnp.float32
BF16 = jnp.bfloat16


def _t5_bucket(rel):
    half = N_BUCKETS // 2
    n = -rel
    ret = jnp.where(n < 0, half, 0)
    n = jnp.abs(n)
    max_exact = half // 2
    nf = jnp.maximum(n, 1).astype(F32)
    large = max_exact + (jnp.log(nf / max_exact) / math.log(MAX_DISTANCE / max_exact)
                         * (half - max_exact)).astype(jnp.int32)
    large = jnp.minimum(large, half - 1)
    return ret + jnp.where(n < max_exact, n, large)


def _bias_kernel(v_ref, o_ref, *, tk, tq, lane_starts, visible):
    a = lax.broadcasted_iota(jnp.int32, (tk, tq), 0)
    b = lax.broadcasted_iota(jnp.int32, (tk, tq), 1)
    for s in range(2):
        x = jnp.broadcast_to(v_ref[s], (tk, v_ref.shape[-1]))
        x = pltpu.roll(x, 0, 1, stride=1, stride_axis=0)
        for t, start in enumerate(lane_starts):
            tile = x[:, start:start + tq]
            vis = visible(t, a, b)
            o_ref[t, :, s * tq:(s + 1) * tq] = tile if vis is None else jnp.where(vis, tile, MASKED)


def _bias_tiles(stream_vec_fn, tk, tq, offsets, visible):
    first_start = -(-tk // LANES) * LANES
    r0 = first_start + max(offsets)
    lane_starts = tuple(r0 - off for off in offsets)
    assert all(s % LANES == 0 for s in lane_starts) and tq % LANES == 0
    n_lanes = max(lane_starts) + tq
    v = stream_vec_fn(r0 - jnp.arange(n_lanes)).astype(F32) * LOG2E
    groups = v.shape[0]
    return pl.pallas_call(
        functools.partial(_bias_kernel, tk=tk, tq=tq, lane_starts=lane_starts, visible=visible),
        grid=(groups,),
        in_specs=[pl.BlockSpec((None, 2, 1, n_lanes), lambda g: (g, 0, 0, 0))],
        out_specs=pl.BlockSpec((len(offsets), None, tk, 2 * tq), lambda g: (0, g, 0, 0)),
        out_shape=jax.ShapeDtypeStruct((len(offsets), groups, tk, 2 * tq), F32),
        compiler_params=pltpu.CompilerParams(dimension_semantics=("arbitrary",),
                                             vmem_limit_bytes=VMEM_LIMIT),
        name="bias_tiles",
    )(v.reshape(groups, 2, 1, n_lanes))


def _diff_streams(table, shift=None):
    def fn(rel):
        v = jnp.transpose(table[_t5_bucket(rel)].astype(F32))
        if shift is not None:
            v = v - shift[:, None]
        return jnp.stack([v, v], axis=1)
    return fn


def _band_streams(table):
    def fn(rel):
        v = jnp.transpose(table[jnp.clip(rel, -REL_CLIP, REL_CLIP) + REL_CLIP].astype(F32))
        return v.reshape(N_HEADS_B // 2, 2, -1)
    return fn


def _diff_prompt_bias(t5_table):
    tk, tq = ATTN_BLOCK_K, ATTN_BLOCK_Q
    assert tk >= MAX_DISTANCE
    far = t5_table[_t5_bucket(jnp.array(-2 * MAX_DISTANCE))].astype(F32)
    offsets = _near_offsets(tk, tq, reach=tk)
    visible = lambda t, a, b: None if offsets[t] < 0 else (a + offsets[t]) // CHUNK <= b // CHUNK
    return _bias_tiles(_diff_streams(t5_table, far), tk, tq, offsets, visible)


def _band_prompt_bias(band_table):
    tk, tq = ATTN_BLOCK_K, ATTN_BLOCK_Q
    offsets = _near_offsets(tk, tq, reach=WINDOW_B)

    def visible(t, a, b):
        kc = (a + offsets[t]) // CHUNK
        qc = b // CHUNK
        return (kc <= qc) & (kc >= qc - BAND_CHUNKS)

    return _bias_tiles(_band_streams(band_table), tk, tq, offsets, visible)


def _near_offsets(tk, tq, reach):
    assert tq % tk == 0 and reach % tk == 0
    return tuple(range(tq - tk, -reach - 1, -tk))


def _store_transposed(vt_ref, v):
    blk = vt_ref.shape[-1]
    ones_rows = _ones_rows(blk)
    for h in range(vt_ref.shape[0]):
        for c in range(vt_ref.shape[1]):
            vt_ref[h, c, :LANES] = v[c * blk:(c + 1) * blk, h * LANES:(h + 1) * LANES].T.astype(BF16)
            vt_ref[h, c, LANES:] = ones_rows


def _ones_rows(n):
    row = lax.broadcasted_iota(jnp.int32, (VT_ROWS - LANES, n), 0)
    return jnp.where(row == 0, 1.0, 0.0).astype(BF16)


def _proj_kernel(x_ref, g_ref, w_ref, qa_ref, ka32_ref, ka16_ref, va32_ref, vta_ref,
                 qb_ref, kb16_ref, vtb_ref, kb32_ref, vb32_ref, *, tail_period):
    x = x_ref[...]
    xn = x * lax.rsqrt(jnp.mean(x * x, axis=-1, keepdims=True) + EPS) * g_ref[...]
    xn = xn.astype(BF16)

    def cols(c):
        return jnp.dot(xn, w_ref[:, c * WIDTH_A:(c + 1) * WIDTH_A], preferred_element_type=F32)

    qa_ref[...] = (cols(0) * Q_SCALE).astype(BF16)
    ka = cols(1)
    if len(ka32_ref.shape) == 2:
        ka32_ref[...] = ka
    else:
        for h in range(N_HEADS_A):
            t = ka[:, h * LANES:(h + 1) * LANES].T
            ka32_ref[h, 0] = t[:HEAD_DIM]
            ka32_ref[h, 1] = t[HEAD_DIM:]
    ka16_ref[...] = ka.astype(BF16)
    va = cols(2)
    for h in range(N_HEADS_A):
        va32_ref[:, h, :] = va[:, h * LANES:(h + 1) * LANES]
    _store_transposed(vta_ref, va)
    qb_ref[...] = (cols(3) * Q_SCALE).astype(BF16)
    kb = cols(4)
    kb16_ref[...] = kb.astype(BF16)
    vb = cols(5)
    _store_transposed(vtb_ref, vb)

    @pl.when(pl.program_id(0) % tail_period == tail_period - 1)
    def _():
        kb32_ref[...] = kb
        vb32_ref[...] = vb


def _project(x, g, w_bf, tail_period, time_major_k=False):
    n, d = x.shape
    tm = ROW_TILE
    blk = ATTN_BLOCK_K
    assert n % (tm * tail_period) == 0 and tm % blk == 0
    groups = WIDTH_A // LANES
    row = lambda i: (i, 0)
    tail = lambda i: (i // tail_period, 0)
    rows = lambda dt, imap=row: (jax.ShapeDtypeStruct((n, WIDTH_A), dt),
                                 pl.BlockSpec((tm, WIDTH_A), imap))
    vt = (jax.ShapeDtypeStruct((groups, n // blk, VT_ROWS, blk), BF16),
          pl.BlockSpec((groups, tm // blk, VT_ROWS, blk), lambda i: (0, i, 0, 0)))
    n_tail = n // tail_period
    tails = (jax.ShapeDtypeStruct((n_tail, WIDTH_B), F32), pl.BlockSpec((tm, WIDTH_B), tail))
    if time_major_k:
        per_batch = tail_period * tm
        ka32 = (jax.ShapeDtypeStruct((n // per_batch, N_HEADS_A, 2, HEAD_DIM, per_batch), F32),
                pl.BlockSpec((None, N_HEADS_A, 2, HEAD_DIM, tm),
                             lambda i: (i // tail_period, 0, 0, 0, i % tail_period)))
    else:
        ka32 = rows(F32)
    va32 = (jax.ShapeDtypeStruct((n, N_HEADS_A, LANES), F32),
            pl.BlockSpec((tm, N_HEADS_A, LANES), lambda i: (i, 0, 0)))
    outs = [rows(BF16), ka32, rows(BF16), va32, vt,
            rows(BF16), rows(BF16), vt, tails, tails]
    return pl.pallas_call(
        functools.partial(_proj_kernel, tail_period=tail_period),
        grid=(n // tm,),
        in_specs=[pl.BlockSpec((tm, d), row),
                  pl.BlockSpec((1, d), lambda i: (0, 0)),
                  pl.BlockSpec(w_bf.shape, lambda i: (0, 0), pipeline_mode=pl.Buffered(1))],
        out_specs=[s for _, s in outs],
        out_shape=[s for s, _ in outs],
        compiler_params=pltpu.CompilerParams(dimension_semantics=("arbitrary",),
                                             vmem_limit_bytes=VMEM_LIMIT),
        name="in_proj",
    )(x, g.reshape(1, d), w_bf)


def _attn_kernel(*refs, tq, tk, n_near, causal_walk, combine, lam_init):
    far_walk = causal_walk and combine == "diff"
    refs = list(refs)
    lam_ref = refs.pop(0) if combine == "diff" else None
    q_ref = refs.pop(0)
    q_next_ref = refs.pop(0) if causal_walk else None
    k_ref, vt_ref, bias_ref = refs[:3]
    del refs[:3]
    g_ref = refs.pop(0) if combine == "diff" else None
    o_ref, m_ref, acc_ref = refs[:3]
    qi = pl.program_id(2)

    def two_streams(ref):
        q = ref[...]
        lane = lax.broadcasted_iota(jnp.int32, q.shape, 1)
        zero = jnp.zeros_like(q)
        return jnp.concatenate([jnp.where(lane < HEAD_DIM, q, zero),
                                jnp.where(lane >= HEAD_DIM, q, zero)], axis=0)

    def scores(qz, kb):
        start = kb * tk if isinstance(kb, int) else pl.multiple_of(kb * tk, tk)
        k = k_ref[pl.ds(start, tk), :]
        return lax.dot_general(k, qz, (((1,), (1,)), ((), ())),
                               preferred_element_type=F32)

    if not causal_walk:
        qz = two_streams(q_ref)
        m = jnp.full(m_ref.shape, MASKED, F32)
        acc = jnp.zeros(acc_ref.shape, F32)
        for t in range(n_near):
            s = scores(qz, t) + bias_ref[t]
            m_new = jnp.maximum(m, jnp.max(s, axis=0, keepdims=True))
            alpha = jnp.exp2(m - m_new)
            p = jnp.exp2(s - m_new)
            acc = alpha * acc + jnp.dot(vt_ref[t], p.astype(BF16), preferred_element_type=F32)
            m = m_new
        m_ref[...] = m
        acc_ref[...] = acc
    else:
        sbuf_ref, gmax_ref = refs[-2:]
        group = GROUP_TILES
        half = group // GROUP_HALVES

        class Walk:
            def __init__(self, ref, tile):
                self.qz = two_streams(ref)
                self.n_visible = (tq // tk) * (tile + 1)
                self.n_tiles = (self.n_visible if far_walk
                                else jnp.minimum(self.n_visible, n_near))

            def key_tile(self, n):
                return jnp.maximum(self.n_visible - 1 - n, 0)

        this = Walk(q_ref, qi)
        following = Walk(q_next_ref, jnp.minimum(qi + 1, pl.num_programs(2) - 1))

        def store_scores(walk, g, t, hmax, masked):
            n = g * group + t
            s = scores(walk.qz, walk.key_tile(n))
            if isinstance(g, int) and n < n_near:
                s = s + bias_ref[n]
            if masked:
                s = jnp.where(n < walk.n_tiles, s, MASKED)
            sbuf_ref[t * tk:(t + 1) * tk, :] = s
            mt = jnp.max(s, axis=0, keepdims=True)
            return mt if hmax is None else jnp.maximum(hmax, mt)

        assert 2 * (tq // tk) >= group

        def first_group(walk, t, hmax):
            return store_scores(walk, 0, t, hmax, walk is this)

        def group_step(g, produce):
            m = m_ref[...]
            acc = acc_ref[...]
            for h in range(GROUP_HALVES):
                m_new = jnp.maximum(m, gmax_ref[h])
                alpha = jnp.exp2(m - m_new)
                pv = None
                hmax = None
                for t in range(h * half, (h + 1) * half):
                    s = sbuf_ref[t * tk:(t + 1) * tk, :]
                    if produce == "following":
                        hmax = first_group(following, t, hmax)
                    else:
                        hmax = store_scores(this, g + 1, t, hmax, produce == "masked")
                    p = jnp.exp2(s - m_new)
                    d = jnp.dot(vt_ref[this.key_tile(g * group + t)], p.astype(BF16),
                                preferred_element_type=F32)
                    pv = d if pv is None else pv + d
                gmax_ref[h] = hmax
                acc = alpha * acc + pv
                m = m_new
            acc_ref[...] = acc
            m_ref[...] = m

        m_ref[...] = jnp.full(m_ref.shape, MASKED, F32)
        acc_ref[...] = jnp.zeros(acc_ref.shape, F32)

        @pl.when(qi == 0)
        def _():
            for h in range(GROUP_HALVES):
                hmax = None
                for t in range(h * half, (h + 1) * half):
                    hmax = first_group(this, t, hmax)
                gmax_ref[h] = hmax

        if far_walk:
            n_groups = (this.n_tiles + group - 1) // group
            n_plain = jnp.maximum(n_groups - 2, 0)

            def plain_steps(first, count):
                for j in range(count):
                    group_step(first + j, "plain")

            def quad(i, carry):
                plain_steps(4 * i, 4)
                return carry

            lax.fori_loop(0, n_plain // 4, quad, 0)
            done = (n_plain // 4) * 4

            @pl.when(n_plain - done >= 2)
            def _():
                plain_steps(done, 2)

            @pl.when(n_plain % 2 == 1)
            def _():
                plain_steps(n_plain - 1, 1)

            @pl.when(n_groups >= 2)
            def _():
                group_step(n_groups - 2, "masked")
                group_step(n_groups - 1, "following")

            @pl.when(n_groups < 2)
            def _():
                group_step(n_groups - 1, "following")
        else:
            assert n_near <= group
            group_step(0, "following")

    o = acc_ref[:LANES, :] * (1.0 / acc_ref[LANES:LANES + 1, :])
    if combine == "diff":
        d = o[:, :tq] - lam_ref[0] * o[:, tq:]
        y = d * lax.rsqrt(jnp.mean(d * d, axis=0, keepdims=True) + SUBLN_EPS)
        o_ref[...] = ((y.T * g_ref[...]) * (1.0 - lam_init)).astype(o_ref.dtype)
    else:
        row = lax.broadcasted_iota(jnp.int32, (LANES, tq), 0)
        o_ref[...] = jnp.where(row < HEAD_DIM, o[:, :tq], o[:, tq:]).T.astype(o_ref.dtype)


def _attention(q, k, vt, bias, *, tq, tk, causal_walk, combine, out_dtype,
               lam=None, g=None, lam_init=0.0):
    b, tq_total, width = q.shape
    tk_total = k.shape[1]
    groups = width // LANES
    n_near = bias.shape[0]
    nkb = tk_total // tk
    assert tq_total % tq == 0 and vt.shape == (groups, b * nkb, VT_ROWS, tk)
    nq = tq_total // tq
    if causal_walk:
        grid = (b, groups, nq)
        at = lambda f: f
    else:
        grid = (groups, b, nq)
        at = lambda f: lambda h, bi, qi: f(bi, h, qi)
    in_specs = [
        pl.BlockSpec((None, tq, LANES), at(lambda bi, h, qi: (bi, qi, h))),
        pl.BlockSpec((None, tk_total, LANES), at(lambda bi, h, qi: (bi, 0, h))),
        pl.BlockSpec((None, nkb, VT_ROWS, tk), at(lambda bi, h, qi: (h, bi, 0, 0))),
        pl.BlockSpec((n_near, None, tk, 2 * tq), at(lambda bi, h, qi: (0, h, 0, 0))),
    ]
    args = [q, k, vt, bias]
    if causal_walk:
        in_specs.insert(1, pl.BlockSpec((None, tq, LANES),
                                        lambda bi, h, qi: (bi, jnp.minimum(qi + 1, nq - 1), h)))
        args.insert(1, q)
    if combine == "diff":
        in_specs = [pl.BlockSpec(memory_space=pltpu.SMEM)] + in_specs
        in_specs.append(pl.BlockSpec((1, LANES), lambda *_: (0, 0)))
        args = [lam.reshape(1).astype(F32)] + args + [g.reshape(1, LANES).astype(F32)]
    scratch = [pltpu.VMEM((1, 2 * tq), F32), pltpu.VMEM((VT_ROWS, 2 * tq), F32)]
    if causal_walk:
        scratch.append(pltpu.VMEM((GROUP_TILES * tk, 2 * tq), F32))
        scratch.append(pltpu.VMEM((GROUP_HALVES, 1, 2 * tq), F32))
    return pl.pallas_call(
        functools.partial(_attn_kernel, tq=tq, tk=tk, n_near=n_near, causal_walk=causal_walk,
                          combine=combine, lam_init=lam_init),
        grid=grid,
        in_specs=in_specs,
        out_specs=pl.BlockSpec((None, tq, LANES), at(lambda bi, h, qi: (bi, qi, h))),
        out_shape=jax.ShapeDtypeStruct((b, tq_total, width), out_dtype),
        scratch_shapes=scratch,
        compiler_params=pltpu.CompilerParams(
            dimension_semantics=("arbitrary", "arbitrary", "arbitrary"),
            vmem_limit_bytes=VMEM_LIMIT),
        name="attn_" + combine + ("_walk" if causal_walk else "_flat"),
    )(*args)


def _transpose_values(v, tk):
    b, t, width = v.shape
    vt = v.reshape(b * (t // tk), tk, width // LANES, LANES).transpose(2, 0, 3, 1)
    ones = jnp.broadcast_to(_ones_rows(tk), vt.shape[:2] + (VT_ROWS - LANES, tk))
    return jnp.concatenate([vt, ones], axis=2)


def _rms(x, g, eps):
    return x * lax.rsqrt(jnp.mean(x * x, axis=-1, keepdims=True) + eps) * g


def _post_kernel(x_ref, oa_ref, ob_ref, p_ref, gb_ref, gm_ref, gf_ref, wout_ref, wup_ref,
                 wdown_ref, wgate_ref, wple_ref, y_ref):
    obn = _rms(ob_ref[...], gb_ref[...], EPS).astype(BF16)
    h = x_ref[...] + (
        jnp.dot(oa_ref[...], wout_ref[:WIDTH_A, :], preferred_element_type=F32)
        + jnp.dot(obn, wout_ref[WIDTH_A:, :], preferred_element_type=F32))
    c = _rms(h, gm_ref[...], EPS).astype(BF16)
    d_ff = wup_ref.shape[1]
    step = wup_ref.shape[0]
    for j in range(d_ff // step):
        u = jnp.dot(c, wup_ref[:, j * step:(j + 1) * step], preferred_element_type=F32)
        u = jnp.square(jnp.maximum(u, 0.0)).astype(BF16)
        h = h + jnp.dot(u, wdown_ref[j * step:(j + 1) * step, :], preferred_element_type=F32)
    gate = jax.nn.sigmoid(jnp.dot(h.astype(BF16), wgate_ref[...], preferred_element_type=F32))
    h = h + gate * jnp.dot(p_ref[...].astype(BF16), wple_ref[...], preferred_element_type=F32)
    y_ref[...] = _rms(h, gf_ref[...], EPS)


def _post(x, oa, ob, p, gb, gm, gf, wout, wup, wdown, wgate, wple):
    n, d = x.shape
    tm = min(ROW_TILE, n)
    assert n % tm == 0
    row = lambda i: (i, 0)
    fixed = lambda a: pl.BlockSpec(a.shape, lambda i: (0,) * a.ndim, pipeline_mode=pl.Buffered(1))
    vec = lambda a: a.reshape(1, -1).astype(F32)
    gb, gm, gf = vec(gb), vec(gm), vec(gf)
    return pl.pallas_call(
        _post_kernel,
        grid=(n // tm,),
        in_specs=[pl.BlockSpec((tm, d), row), pl.BlockSpec((tm, oa.shape[1]), row),
                  pl.BlockSpec((tm, ob.shape[1]), row), pl.BlockSpec((tm, p.shape[1]), row),
                  fixed(gb), fixed(gm), fixed(gf), fixed(wout), fixed(wup), fixed(wdown),
                  fixed(wgate), fixed(wple)],
        out_specs=pl.BlockSpec((tm, d), row),
        out_shape=jax.ShapeDtypeStruct((n, d), F32),
        compiler_params=pltpu.CompilerParams(dimension_semantics=("arbitrary",),
                                             vmem_limit_bytes=VMEM_LIMIT),
        name="post",
    )(x, oa, ob, p, gb, gm, gf, wout, wup, wdown, wgate, wple)


def _pad_rows(a, rows):
    return jnp.pad(a, ((0, 0), (0, rows - a.shape[1]), (0, 0)))


def _sample_bias(stream_vec_fn, n_keys, tk, tq, first_kpos, past, band):
    def visible(t, a, b):
        kpos = first_kpos + a
        kc = kpos // CHUNK
        qc = (past + b) // CHUNK
        vis = (kc <= qc) & (a < n_keys)
        if band:
            vis = vis & (kc >= qc - BAND_CHUNKS) & (kpos >= 0)
        return vis

    return _bias_tiles(stream_vec_fn, tk, tq, (first_kpos - past,), visible)


def kernel(x_prompt, x_sample, cache_a_k, cache_a_v, cache_b_k, cache_b_v, p_prompt, p_sample,
           t5_table, g_attn, w_in, lambda_q1, lambda_k1, lambda_q2, lambda_k2, subln_g,
           band_table, out_norm_b, w_out, g_mlp, w_up, w_down, w_ple_gate, w_ple_proj, g_final):
    depth = w_in.shape[0]
    assert depth == 1
    i = 0
    B, S, D = x_prompt.shape
    Bs, Ts, _ = x_sample.shape
    past = cache_a_k.shape[2]
    cache_b_len = cache_b_k.shape[2]
    keep = min(WINDOW_B, S)
    assert keep == ROW_TILE and S % ATTN_BLOCK_Q == 0 and Bs * Ts == ROW_TILE

    lam_init = 0.8 - 0.6 * math.exp(-0.3 * i)
    lam = (jnp.exp(jnp.sum(lambda_q1[i].astype(F32) * lambda_k1[i].astype(F32)))
           - jnp.exp(jnp.sum(lambda_q2[i].astype(F32) * lambda_k2[i].astype(F32)))
           + lam_init)

    w_in_bf = w_in[i].astype(BF16)
    post_w = (out_norm_b[i], g_mlp[i], g_final, w_out[i].astype(BF16), w_up[i].astype(BF16),
              w_down[i].astype(BF16), w_ple_gate[i].astype(BF16), w_ple_proj[i].astype(BF16))
    tk_p, tq_p = ATTN_BLOCK_K, ATTN_BLOCK_Q

    xp = x_prompt.reshape(B * S, D)
    (qa, ka32, ka16, va32, vta, qb, kb16, vtb, kb32, vb32) = _project(
        xp, g_attn[i], w_in_bf, tail_period=S // ROW_TILE, time_major_k=True)
    to3 = lambda a: a.reshape(B, S, a.shape[-1])
    oa = _attention(to3(qa), to3(ka16), vta, _diff_prompt_bias(t5_table), tq=tq_p, tk=tk_p,
                    causal_walk=True, combine="diff", out_dtype=BF16, lam=lam, g=subln_g[i],
                    lam_init=lam_init)
    ob = _attention(to3(qb), to3(kb16), vtb, _band_prompt_bias(band_table[i]), tq=tq_p, tk=tk_p,
                    causal_walk=True, combine="band", out_dtype=F32)
    y_prompt = _post(xp, oa.reshape(B * S, WIDTH_A), ob.reshape(B * S, WIDTH_B),
                     p_prompt[i].reshape(B * S, -1), *post_w).reshape(B, S, D)

    xs = x_sample.reshape(Bs * Ts, D)
    (qa_s, ka32_s, ka16_s, va32_s, _, qb_s, kb16_s, _, kb32_s, vb32_s) = _project(
        xs, g_attn[i], w_in_bf, tail_period=1)
    to3s = lambda a: a.reshape(Bs, Ts, -1)
    tq_s = LANES
    assert Ts <= tq_s

    def sample_keys(cache, new):
        c = cache.reshape(Bs, cache.shape[1], -1).astype(BF16)
        allk = jnp.concatenate([c, to3s(new).astype(BF16)], axis=1)
        n_keys = allk.shape[1]
        tk = -(-n_keys // LANES) * LANES
        return _pad_rows(allk, tk), n_keys, tk

    ka_all, n_a, tk_a = sample_keys(cache_a_k[i], ka16_s)
    va_all, _, _ = sample_keys(cache_a_v[i], va32_s)
    kb_all, n_b, tk_b = sample_keys(cache_b_k[i], kb16_s)
    vb_all, _, _ = sample_keys(cache_b_v[i], vb32_s)

    bias_a = _sample_bias(_diff_streams(t5_table), n_a, tk_a, tq_s, 0, past, band=False)
    bias_b = _sample_bias(_band_streams(band_table[i]), n_b, tk_b, tq_s,
                          past - cache_b_len, past, band=True)

    oa_s = _attention(_pad_rows(to3s(qa_s), tq_s), ka_all, _transpose_values(va_all, tk_a),
                      bias_a, tq=tq_s, tk=tk_a, causal_walk=False, combine="diff",
                      out_dtype=BF16, lam=lam, g=subln_g[i], lam_init=lam_init)[:, :Ts]
    ob_s = _attention(_pad_rows(to3s(qb_s), tq_s), kb_all, _transpose_values(vb_all, tk_b),
                      bias_b, tq=tq_s, tk=tk_b, causal_walk=False, combine="band",
                      out_dtype=F32)[:, :Ts]
    y_sample = _post(xs, oa_s.reshape(Bs * Ts, WIDTH_A), ob_s.reshape(Bs * Ts, WIDTH_B),
                     p_sample[i].reshape(Bs * Ts, -1), *post_w).reshape(Bs, Ts, D)

    return (y_prompt, y_sample,
            ka32.transpose(0, 4, 1, 2, 3)[None],
            va32.reshape(1, B, S, N_HEADS_A, 2 * HEAD_DIM),
            kb32.reshape(1, B, keep, N_HEADS_B, HEAD_DIM),
            vb32.reshape(1, B, keep, N_HEADS_B, HEAD_DIM),
            ka32_s.reshape(1, Bs, Ts, N_HEADS_A, 2, HEAD_DIM),
            va32_s.reshape(1, Bs, Ts, N_HEADS_A, 2 * HEAD_DIM),
            kb32_s.reshape(1, Bs, Ts, N_HEADS_B, HEAD_DIM),
            vb32_s.reshape(1, Bs, Ts, N_HEADS_B, HEAD_DIM))
```

```python
import functools
import math

import jax
import jax.numpy as jnp
from jax import lax
from jax.experimental import pallas as pl
from jax.experimental.pallas import tpu as pltpu

CHUNK = 64
HEAD_DIM = 64
N_HEADS_A = 4
N_HEADS_B = 8
WIDTH_A = N_HEADS_A * 2 * HEAD_DIM
WIDTH_B = N_HEADS_B * HEAD_DIM
N_BUCKETS = 32
MAX_DISTANCE = 128
BAND_CHUNKS = 8
WINDOW_B = BAND_CHUNKS * CHUNK
REL_CLIP = 128
EPS = 1e-6
SUBLN_EPS = 1e-5
MASKED = -1e30
LOG2E = math.log2(math.e)
Q_SCALE = LOG2E * HEAD_DIM ** -0.5

LANES = 128
VT_ROWS = LANES + 16
ATTN_BLOCK_K = 256
ATTN_BLOCK_Q = 512
GROUP_TILES = 4
GROUP_HALVES = 1
ROW_TILE = 512
VMEM_LIMIT = 56 * 1024 * 1024

F32 = jnp.float32
BF16 = jnp.bfloat16


def _t5_bucket(rel):
    half = N_BUCKETS // 2
    n = -rel
    ret = jnp.where(n < 0, half, 0)
    n = jnp.abs(n)
    max_exact = half // 2
    nf = jnp.maximum(n, 1).astype(F32)
    large = max_exact + (jnp.log(nf / max_exact) / math.log(MAX_DISTANCE / max_exact)
                         * (half - max_exact)).astype(jnp.int32)
    large = jnp.minimum(large, half - 1)
    return ret + jnp.where(n < max_exact, n, large)


def _bias_kernel(v_ref, o_ref, *, tk, tq, lane_starts, visible):
    a = lax.broadcasted_iota(jnp.int32, (tk, tq), 0)
    b = lax.broadcasted_iota(jnp.int32, (tk, tq), 1)
    for s in range(2):
        x = jnp.broadcast_to(v_ref[s], (tk, v_ref.shape[-1]))
        x = pltpu.roll(x, 0, 1, stride=1, stride_axis=0)
        for t, start in enumerate(lane_starts):
            tile = x[:, start:start + tq]
            vis = visible(t, a, b)
            o_ref[t, :, s * tq:(s + 1) * tq] = tile if vis is None else jnp.where(vis, tile, MASKED)


def _bias_tiles(stream_vec_fn, tk, tq, offsets, visible):
    first_start = -(-tk // LANES) * LANES
    r0 = first_start + max(offsets)
    lane_starts = tuple(r0 - off for off in offsets)
    assert all(s % LANES == 0 for s in lane_starts) and tq % LANES == 0
    n_lanes = max(lane_starts) + tq
    v = stream_vec_fn(r0 - jnp.arange(n_lanes)).astype(F32) * LOG2E
    groups = v.shape[0]
    return pl.pallas_call(
        functools.partial(_bias_kernel, tk=tk, tq=tq, lane_starts=lane_starts, visible=visible),
        grid=(groups,),
        in_specs=[pl.BlockSpec((None, 2, 1, n_lanes), lambda g: (g, 0, 0, 0))],
        out_specs=pl.BlockSpec((len(offsets), None, tk, 2 * tq), lambda g: (0, g, 0, 0)),
        out_shape=jax.ShapeDtypeStruct((len(offsets), groups, tk, 2 * tq), F32),
        compiler_params=pltpu.CompilerParams(dimension_semantics=("arbitrary",),
                                             vmem_limit_bytes=VMEM_LIMIT),
        name="bias_tiles",
    )(v.reshape(groups, 2, 1, n_lanes))


def _diff_streams(table, shift=None):
    def fn(rel):
        v = jnp.transpose(table[_t5_bucket(rel)].astype(F32))
        if shift is not None:
            v = v - shift[:, None]
        return jnp.stack([v, v], axis=1)
    return fn


def _band_streams(table):
    def fn(rel):
        v = jnp.transpose(table[jnp.clip(rel, -REL_CLIP, REL_CLIP) + REL_CLIP].astype(F32))
        return v.reshape(N_HEADS_B // 2, 2, -1)
    return fn


def _diff_prompt_bias(t5_table):
    tk, tq = ATTN_BLOCK_K, ATTN_BLOCK_Q
    assert tk >= MAX_DISTANCE
    far = t5_table[_t5_bucket(jnp.array(-2 * MAX_DISTANCE))].astype(F32)
    offsets = _near_offsets(tk, tq, reach=tk)
    visible = lambda t, a, b: None if offsets[t] < 0 else (a + offsets[t]) // CHUNK <= b // CHUNK
    return _bias_tiles(_diff_streams(t5_table, far), tk, tq, offsets, visible)


def _band_prompt_bias(band_table):
    tk, tq = ATTN_BLOCK_K, ATTN_BLOCK_Q
    offsets = _near_offsets(tk, tq, reach=WINDOW_B)

    def visible(t, a, b):
        kc = (a + offsets[t]) // CHUNK
        qc = b // CHUNK
        return (kc <= qc) & (kc >= qc - BAND_CHUNKS)

    return _bias_tiles(_band_streams(band_table), tk, tq, offsets, visible)


def _near_offsets(tk, tq, reach):
    assert tq % tk == 0 and reach % tk == 0
    return tuple(range(tq - tk, -reach - 1, -tk))


def _store_transposed(vt_ref, v):
    blk = vt_ref.shape[-1]
    ones_rows = _ones_rows(blk)
    for h in range(vt_ref.shape[0]):
        for c in range(vt_ref.shape[1]):
            vt_ref[h, c, :LANES] = v[c * blk:(c + 1) * blk, h * LANES:(h + 1) * LANES].T.astype(BF16)
            vt_ref[h, c, LANES:] = ones_rows


def _ones_rows(n):
    row = lax.broadcasted_iota(jnp.int32, (VT_ROWS - LANES, n), 0)
    return jnp.where(row == 0, 1.0, 0.0).astype(BF16)


def _proj_kernel(x_ref, g_ref, w_ref, qa_ref, ka32_ref, ka16_ref, va32_ref, vta_ref,
                 qb_ref, kb16_ref, vtb_ref, kb32_ref, vb32_ref, *, tail_period):
    x = x_ref[...]
    xn = x * lax.rsqrt(jnp.mean(x * x, axis=-1, keepdims=True) + EPS) * g_ref[...]
    xn = xn.astype(BF16)

    def cols(c):
        return jnp.dot(xn, w_ref[:, c * WIDTH_A:(c + 1) * WIDTH_A], preferred_element_type=F32)

    qa_ref[...] = (cols(0) * Q_SCALE).astype(BF16)
    ka = cols(1)
    if len(ka32_ref.shape) == 2:
        ka32_ref[...] = ka
    else:
        for h in range(N_HEADS_A):
            t = ka[:, h * LANES:(h + 1) * LANES].T
            ka32_ref[h, 0] = t[:HEAD_DIM]
            ka32_ref[h, 1] = t[HEAD_DIM:]
    ka16_ref[...] = ka.astype(BF16)
    va = cols(2)
    va32_ref[...] = va.reshape(va32_ref.shape)
    _store_transposed(vta_ref, va)
    qb_ref[...] = (cols(3) * Q_SCALE).astype(BF16)
    kb = cols(4)
    kb16_ref[...] = kb.astype(BF16)
    vb = cols(5)
    _store_transposed(vtb_ref, vb)

    @pl.when(pl.program_id(0) % tail_period == tail_period - 1)
    def _():
        kb32_ref[...] = kb
        vb32_ref[...] = vb


def _project(x, g, w_bf, tail_period, time_major_k=False):
    n, d = x.shape
    tm = ROW_TILE
    blk = ATTN_BLOCK_K
    assert n % (tm * tail_period) == 0 and tm % blk == 0
    groups = WIDTH_A // LANES
    row = lambda i: (i, 0)
    tail = lambda i: (i // tail_period, 0)
    rows = lambda dt, imap=row: (jax.ShapeDtypeStruct((n, WIDTH_A), dt),
                                 pl.BlockSpec((tm, WIDTH_A), imap))
    vt = (jax.ShapeDtypeStruct((groups, n // blk, VT_ROWS, blk), BF16),
          pl.BlockSpec((groups, tm // blk, VT_ROWS, blk), lambda i: (0, i, 0, 0)))
    n_tail = n // tail_period
    tails = (jax.ShapeDtypeStruct((n_tail, WIDTH_B), F32), pl.BlockSpec((tm, WIDTH_B), tail))
    if time_major_k:
        per_batch = tail_period * tm
        ka32 = (jax.ShapeDtypeStruct((n // per_batch, N_HEADS_A, 2, HEAD_DIM, per_batch), F32),
                pl.BlockSpec((None, N_HEADS_A, 2, HEAD_DIM, tm),
                             lambda i: (i // tail_period, 0, 0, 0, i % tail_period)))
    else:
        ka32 = rows(F32)
    va32 = (jax.ShapeDtypeStruct((n, N_HEADS_A, LANES), F32),
            pl.BlockSpec((tm, N_HEADS_A, LANES), lambda i: (i, 0, 0)))
    outs = [rows(BF16), ka32, rows(BF16), va32, vt,
            rows(BF16), rows(BF16), vt, tails, tails]
    return pl.pallas_call(
        functools.partial(_proj_kernel, tail_period=tail_period),
        grid=(n // tm,),
        in_specs=[pl.BlockSpec((tm, d), row),
                  pl.BlockSpec((1, d), lambda i: (0, 0)),
                  pl.BlockSpec(w_bf.shape, lambda i: (0, 0), pipeline_mode=pl.Buffered(1))],
        out_specs=[s for _, s in outs],
        out_shape=[s for s, _ in outs],
        compiler_params=pltpu.CompilerParams(dimension_semantics=("arbitrary",),
                                             vmem_limit_bytes=VMEM_LIMIT),
        name="in_proj",
    )(x, g.reshape(1, d), w_bf)


def _attn_kernel(*refs, tq, tk, n_near, causal_walk, combine, lam_init):
    far_walk = causal_walk and combine == "diff"
    refs = list(refs)
    lam_ref = refs.pop(0) if combine == "diff" else None
    q_ref = refs.pop(0)
    q_next_ref = refs.pop(0) if causal_walk else None
    k_ref, vt_ref, bias_ref = refs[:3]
    del refs[:3]
    g_ref = refs.pop(0) if combine == "diff" else None
    o_ref, m_ref, acc_ref = refs[:3]
    qi = pl.program_id(2)

    def two_streams(ref):
        q = ref[...]
        lane = lax.broadcasted_iota(jnp.int32, q.shape, 1)
        zero = jnp.zeros_like(q)
        return jnp.concatenate([jnp.where(lane < HEAD_DIM, q, zero),
                                jnp.where(lane >= HEAD_DIM, q, zero)], axis=0)

    def scores(qz, kb):
        start = kb * tk if isinstance(kb, int) else pl.multiple_of(kb * tk, tk)
        k = k_ref[pl.ds(start, tk), :]
        return lax.dot_general(k, qz, (((1,), (1,)), ((), ())),
                               preferred_element_type=F32)

    if not causal_walk:
        qz = two_streams(q_ref)
        m = jnp.full(m_ref.shape, MASKED, F32)
        acc = jnp.zeros(acc_ref.shape, F32)
        for t in range(n_near):
            s = scores(qz, t) + bias_ref[t]
            m_new = jnp.maximum(m, jnp.max(s, axis=0, keepdims=True))
            alpha = jnp.exp2(m - m_new)
            p = jnp.exp2(s - m_new)
            acc = alpha * acc + jnp.dot(vt_ref[t], p.astype(BF16), preferred_element_type=F32)
            m = m_new
        m_ref[...] = m
        acc_ref[...] = acc
    else:
        sbuf_ref, gmax_ref = refs[-2:]
        group = GROUP_TILES
        half = group // GROUP_HALVES

        class Walk:
            def __init__(self, ref, tile):
                self.qz = two_streams(ref)
                self.n_visible = (tq // tk) * (tile + 1)
                self.n_tiles = (self.n_visible if far_walk
                                else jnp.minimum(self.n_visible, n_near))

            def key_tile(self, n):
                return jnp.maximum(self.n_visible - 1 - n, 0)

        this = Walk(q_ref, qi)
        following = Walk(q_next_ref, jnp.minimum(qi + 1, pl.num_programs(2) - 1))

        def store_scores(walk, g, t, hmax, masked):
            n = g * group + t
            s = scores(walk.qz, walk.key_tile(n))
            if isinstance(g, int) and n < n_near:
                s = s + bias_ref[n]
            if masked:
                s = jnp.where(n < walk.n_tiles, s, MASKED)
            sbuf_ref[t * tk:(t + 1) * tk, :] = s
            mt = jnp.max(s, axis=0, keepdims=True)
            return mt if hmax is None else jnp.maximum(hmax, mt)

        assert 2 * (tq // tk) >= group

        def first_group(walk, t, hmax):
            return store_scores(walk, 0, t, hmax, walk is this)

        def group_step(g, produce):
            m = m_ref[...]
            acc = acc_ref[...]
            for h in range(GROUP_HALVES):
                m_new = jnp.maximum(m, gmax_ref[h])
                alpha = jnp.exp2(m - m_new)
                pv = None
                hmax = None
                for t in range(h * half, (h + 1) * half):
                    s = sbuf_ref[t * tk:(t + 1) * tk, :]
                    if produce == "following":
                        hmax = first_group(following, t, hmax)
                    else:
                        hmax = store_scores(this, g + 1, t, hmax, produce == "masked")
                    p = jnp.exp2(s - m_new)
                    d = jnp.dot(vt_ref[this.key_tile(g * group + t)], p.astype(BF16),
                                preferred_element_type=F32)
                    pv = d if pv is None else pv + d
                gmax_ref[h] = hmax
                acc = alpha * acc + pv
                m = m_new
            acc_ref[...] = acc
            m_ref[...] = m

        m_ref[...] = jnp.full(m_ref.shape, MASKED, F32)
        acc_ref[...] = jnp.zeros(acc_ref.shape, F32)

        @pl.when(qi == 0)
        def _():
            for h in range(GROUP_HALVES):
                hmax = None
                for t in range(h * half, (h + 1) * half):
                    hmax = first_group(this, t, hmax)
                gmax_ref[h] = hmax

        if far_walk:
            n_groups = (this.n_tiles + group - 1) // group
            n_plain = jnp.maximum(n_groups - 2, 0)

            def plain_steps(first, count):
                for j in range(count):
                    group_step(first + j, "plain")

            def quad(i, carry):
                plain_steps(4 * i, 4)
                return carry

            lax.fori_loop(0, n_plain // 4, quad, 0)
            done = (n_plain // 4) * 4

            @pl.when(n_plain - done >= 2)
            def _():
                plain_steps(done, 2)

            @pl.when(n_plain % 2 == 1)
            def _():
                plain_steps(n_plain - 1, 1)

            @pl.when(n_groups >= 2)
            def _():
                group_step(n_groups - 2, "masked")
                group_step(n_groups - 1, "following")

            @pl.when(n_groups < 2)
            def _():
                group_step(n_groups - 1, "following")
        else:
            assert n_near <= group
            group_step(0, "following")

    o = acc_ref[:LANES, :] * (1.0 / acc_ref[LANES:LANES + 1, :])
    if combine == "diff":
        d = o[:, :tq] - lam_ref[0] * o[:, tq:]
        y = d * lax.rsqrt(jnp.mean(d * d, axis=0, keepdims=True) + SUBLN_EPS)
        o_ref[...] = ((y.T * g_ref[...]) * (1.0 - lam_init)).astype(o_ref.dtype)
    else:
        row = lax.broadcasted_iota(jnp.int32, (LANES, tq), 0)
        o_ref[...] = jnp.where(row < HEAD_DIM, o[:, :tq], o[:, tq:]).T.astype(o_ref.dtype)


def _attention(q, k, vt, bias, *, tq, tk, causal_walk, combine, out_dtype,
               lam=None, g=None, lam_init=0.0):
    b, tq_total, width = q.shape
    tk_total = k.shape[1]
    groups = width // LANES
    n_near = bias.shape[0]
    nkb = tk_total // tk
    assert tq_total % tq == 0 and vt.shape == (groups, b * nkb, VT_ROWS, tk)
    nq = tq_total // tq
    if causal_walk:
        grid = (b, groups, nq)
        at = lambda f: f
    else:
        grid = (groups, b, nq)
        at = lambda f: lambda h, bi, qi: f(bi, h, qi)
    in_specs = [
        pl.BlockSpec((None, tq, LANES), at(lambda bi, h, qi: (bi, qi, h))),
        pl.BlockSpec((None, tk_total, LANES), at(lambda bi, h, qi: (bi, 0, h))),
        pl.BlockSpec((None, nkb, VT_ROWS, tk), at(lambda bi, h, qi: (h, bi, 0, 0))),
        pl.BlockSpec((n_near, None, tk, 2 * tq), at(lambda bi, h, qi: (0, h, 0, 0))),
    ]
    args = [q, k, vt, bias]
    if causal_walk:
        in_specs.insert(1, pl.BlockSpec((None, tq, LANES),
                                        lambda bi, h, qi: (bi, jnp.minimum(qi + 1, nq - 1), h)))
        args.insert(1, q)
    if combine == "diff":
        in_specs = [pl.BlockSpec(memory_space=pltpu.SMEM)] + in_specs
        in_specs.append(pl.BlockSpec((1, LANES), lambda *_: (0, 0)))
        args = [lam.reshape(1).astype(F32)] + args + [g.reshape(1, LANES).astype(F32)]
    scratch = [pltpu.VMEM((1, 2 * tq), F32), pltpu.VMEM((VT_ROWS, 2 * tq), F32)]
    if causal_walk:
        scratch.append(pltpu.VMEM((GROUP_TILES * tk, 2 * tq), F32))
        scratch.append(pltpu.VMEM((GROUP_HALVES, 1, 2 * tq), F32))
    return pl.pallas_call(
        functools.partial(_attn_kernel, tq=tq, tk=tk, n_near=n_near, causal_walk=causal_walk,
                          combine=combine, lam_init=lam_init),
        grid=grid,
        in_specs=in_specs,
        out_specs=pl.BlockSpec((None, tq, LANES), at(lambda bi, h, qi: (bi, qi, h))),
        out_shape=jax.ShapeDtypeStruct((b, tq_total, width), out_dtype),
        scratch_shapes=scratch,
        compiler_params=pltpu.CompilerParams(
            dimension_semantics=("arbitrary", "arbitrary", "arbitrary"),
            vmem_limit_bytes=VMEM_LIMIT),
        name="attn_" + combine + ("_walk" if causal_walk else "_flat"),
    )(*args)


def _transpose_values(v, tk):
    b, t, width = v.shape
    vt = v.reshape(b * (t // tk), tk, width // LANES, LANES).transpose(2, 0, 3, 1)
    ones = jnp.broadcast_to(_ones_rows(tk), vt.shape[:2] + (VT_ROWS - LANES, tk))
    return jnp.concatenate([vt, ones], axis=2)


def _rms(x, g, eps):
    return x * lax.rsqrt(jnp.mean(x * x, axis=-1, keepdims=True) + eps) * g


def _post_kernel(x_ref, oa_ref, ob_ref, p_ref, gb_ref, gm_ref, gf_ref, wout_ref, wup_ref,
                 wdown_ref, wgate_ref, wple_ref, y_ref):
    obn = _rms(ob_ref[...], gb_ref[...], EPS).astype(BF16)
    h = x_ref[...] + (
        jnp.dot(oa_ref[...], wout_ref[:WIDTH_A, :], preferred_element_type=F32)
        + jnp.dot(obn, wout_ref[WIDTH_A:, :], preferred_element_type=F32))
    c = _rms(h, gm_ref[...], EPS).astype(BF16)
    d_ff = wup_ref.shape[1]
    step = wup_ref.shape[0]
    for j in range(d_ff // step):
        u = jnp.dot(c, wup_ref[:, j * step:(j + 1) * step], preferred_element_type=F32)
        u = jnp.square(jnp.maximum(u, 0.0)).astype(BF16)
        h = h + jnp.dot(u, wdown_ref[j * step:(j + 1) * step, :], preferred_element_type=F32)
    gate = jax.nn.sigmoid(jnp.dot(h.astype(BF16), wgate_ref[...], preferred_element_type=F32))
    h = h + gate * jnp.dot(p_ref[...].astype(BF16), wple_ref[...], preferred_element_type=F32)
    y_ref[...] = _rms(h, gf_ref[...], EPS)


def _post(x, oa, ob, p, gb, gm, gf, wout, wup, wdown, wgate, wple):
    n, d = x.shape
    tm = min(ROW_TILE, n)
    assert n % tm == 0
    row = lambda i: (i, 0)
    fixed = lambda a: pl.BlockSpec(a.shape, lambda i: (0,) * a.ndim, pipeline_mode=pl.Buffered(1))
    vec = lambda a: a.reshape(1, -1).astype(F32)
    gb, gm, gf = vec(gb), vec(gm), vec(gf)
    return pl.pallas_call(
        _post_kernel,
        grid=(n // tm,),
        in_specs=[pl.BlockSpec((tm, d), row), pl.BlockSpec((tm, oa.shape[1]), row),
                  pl.BlockSpec((tm, ob.shape[1]), row), pl.BlockSpec((tm, p.shape[1]), row),
                  fixed(gb), fixed(gm), fixed(gf), fixed(wout), fixed(wup), fixed(wdown),
                  fixed(wgate), fixed(wple)],
        out_specs=pl.BlockSpec((tm, d), row),
        out_shape=jax.ShapeDtypeStruct((n, d), F32),
        compiler_params=pltpu.CompilerParams(dimension_semantics=("arbitrary",),
                                             vmem_limit_bytes=VMEM_LIMIT),
        name="post",
    )(x, oa, ob, p, gb, gm, gf, wout, wup, wdown, wgate, wple)


def _pad_rows(a, rows):
    return jnp.pad(a, ((0, 0), (0, rows - a.shape[1]), (0, 0)))


def _sample_bias(stream_vec_fn, n_keys, tk, tq, first_kpos, past, band):
    def visible(t, a, b):
        kpos = first_kpos + a
        kc = kpos // CHUNK
        qc = (past + b) // CHUNK
        vis = (kc <= qc) & (a < n_keys)
        if band:
            vis = vis & (kc >= qc - BAND_CHUNKS) & (kpos >= 0)
        return vis

    return _bias_tiles(stream_vec_fn, tk, tq, (first_kpos - past,), visible)


def kernel(x_prompt, x_sample, cache_a_k, cache_a_v, cache_b_k, cache_b_v, p_prompt, p_sample,
           t5_table, g_attn, w_in, lambda_q1, lambda_k1, lambda_q2, lambda_k2, subln_g,
           band_table, out_norm_b, w_out, g_mlp, w_up, w_down, w_ple_gate, w_ple_proj, g_final):
    depth = w_in.shape[0]
    assert depth == 1
    i = 0
    B, S, D = x_prompt.shape
    Bs, Ts, _ = x_sample.shape
    past = cache_a_k.shape[2]
    cache_b_len = cache_b_k.shape[2]
    keep = min(WINDOW_B, S)
    assert keep == ROW_TILE and S % ATTN_BLOCK_Q == 0 and Bs * Ts == ROW_TILE

    lam_init = 0.8 - 0.6 * math.exp(-0.3 * i)
    lam = (jnp.exp(jnp.sum(lambda_q1[i].astype(F32) * lambda_k1[i].astype(F32)))
           - jnp.exp(jnp.sum(lambda_q2[i].astype(F32) * lambda_k2[i].astype(F32)))
           + lam_init)

    w_in_bf = w_in[i].astype(BF16)
    post_w = (out_norm_b[i], g_mlp[i], g_final, w_out[i].astype(BF16), w_up[i].astype(BF16),
              w_down[i].astype(BF16), w_ple_gate[i].astype(BF16), w_ple_proj[i].astype(BF16))
    tk_p, tq_p = ATTN_BLOCK_K, ATTN_BLOCK_Q

    xp = x_prompt.reshape(B * S, D)
    (qa, ka32, ka16, va32, vta, qb, kb16, vtb, kb32, vb32) = _project(
        xp, g_attn[i], w_in_bf, tail_period=S // ROW_TILE, time_major_k=True)
    to3 = lambda a: a.reshape(B, S, a.shape[-1])
    oa = _attention(to3(qa), to3(ka16), vta, _diff_prompt_bias(t5_table), tq=tq_p, tk=tk_p,
                    causal_walk=True, combine="diff", out_dtype=BF16, lam=lam, g=subln_g[i],
                    lam_init=lam_init)
    ob = _attention(to3(qb), to3(kb16), vtb, _band_prompt_bias(band_table[i]), tq=tq_p, tk=tk_p,
                    causal_walk=True, combine="band", out_dtype=F32)
    y_prompt = _post(xp, oa.reshape(B * S, WIDTH_A), ob.reshape(B * S, WIDTH_B),
                     p_prompt[i].reshape(B * S, -1), *post_w).reshape(B, S, D)

    xs = x_sample.reshape(Bs * Ts, D)
    (qa_s, ka32_s, ka16_s, va32_s, _, qb_s, kb16_s, _, kb32_s, vb32_s) = _project(
        xs, g_attn[i], w_in_bf, tail_period=1)
    to3s = lambda a: a.reshape(Bs, Ts, -1)
    tq_s = LANES
    assert Ts <= tq_s

    def sample_keys(cache, new):
        c = cache.reshape(Bs, cache.shape[1], -1).astype(BF16)
        allk = jnp.concatenate([c, to3s(new).astype(BF16)], axis=1)
        n_keys = allk.shape[1]
        tk = -(-n_keys // LANES) * LANES
        return _pad_rows(allk, tk), n_keys, tk

    ka_all, n_a, tk_a = sample_keys(cache_a_k[i], ka16_s)
    va_all, _, _ = sample_keys(cache_a_v[i], va32_s)
    kb_all, n_b, tk_b = sample_keys(cache_b_k[i], kb16_s)
    vb_all, _, _ = sample_keys(cache_b_v[i], vb32_s)

    bias_a = _sample_bias(_diff_streams(t5_table), n_a, tk_a, tq_s, 0, past, band=False)
    bias_b = _sample_bias(_band_streams(band_table[i]), n_b, tk_b, tq_s,
                          past - cache_b_len, past, band=True)

    oa_s = _attention(_pad_rows(to3s(qa_s), tq_s), ka_all, _transpose_values(va_all, tk_a),
                      bias_a, tq=tq_s, tk=tk_a, causal_walk=False, combine="diff",
                      out_dtype=BF16, lam=lam, g=subln_g[i], lam_init=lam_init)[:, :Ts]
    ob_s = _attention(_pad_rows(to3s(qb_s), tq_s), kb_all, _transpose_values(vb_all, tk_b),
                      bias_b, tq=tq_s, tk=tk_b, causal_walk=False, combine="band",
                      out_dtype=F32)[:, :Ts]
    y_sample = _post(xs, oa_s.reshape(Bs * Ts, WIDTH_A), ob_s.reshape(Bs * Ts, WIDTH_B),
                     p_sample[i].reshape(Bs * Ts, -1), *post_w).reshape(Bs, Ts, D)

    return (y_prompt, y_sample,
            ka32.transpose(0, 4, 1, 2, 3)[None],
            va32.reshape(1, B, S, N_HEADS_A, 2 * HEAD_DIM),
            kb32.reshape(1, B, keep, N_HEADS_B, HEAD_DIM),
            vb32.reshape(1, B, keep, N_HEADS_B, HEAD_DIM),
            ka32_s.reshape(1, Bs, Ts, N_HEADS_A, 2, HEAD_DIM),
            va32_s.reshape(1, Bs, Ts, N_HEADS_A, 2 * HEAD_DIM),
            kb32_s.reshape(1, Bs, Ts, N_HEADS_B, HEAD_DIM),
            vb32_s.reshape(1, Bs, Ts, N_HEADS_B, HEAD_DIM))
```

```python
import functools
import math

import jax
import jax.numpy as jnp
from jax import lax
from jax.experimental import pallas as pl
from jax.experimental.pallas import tpu as pltpu

CHUNK = 64
HEAD_DIM = 64
N_HEADS_A = 4
N_HEADS_B = 8
WIDTH_A = N_HEADS_A * 2 * HEAD_DIM
WIDTH_B = N_HEADS_B * HEAD_DIM
N_BUCKETS = 32
MAX_DISTANCE = 128
BAND_CHUNKS = 8
WINDOW_B = BAND_CHUNKS * CHUNK
REL_CLIP = 128
EPS = 1e-6
SUBLN_EPS = 1e-5
MASKED = -1e30
LOG2E = math.log2(math.e)
Q_SCALE = LOG2E * HEAD_DIM ** -0.5

LANES = 128
VT_ROWS = LANES + 16
ATTN_BLOCK_K = 256
ATTN_BLOCK_Q = 512
GROUP_TILES = 4
GROUP_HALVES = 1
FLAT_BATCH = 4
ROW_TILE = 512
VMEM_LIMIT = 56 * 1024 * 1024

F32 = jnp.float32
BF16 = jnp.bfloat16


def _t5_bucket(rel):
    half = N_BUCKETS // 2
    n = -rel
    ret = jnp.where(n < 0, half, 0)
    n = jnp.abs(n)
    max_exact = half // 2
    nf = jnp.maximum(n, 1).astype(F32)
    large = max_exact + (jnp.log(nf / max_exact) / math.log(MAX_DISTANCE / max_exact)
                         * (half - max_exact)).astype(jnp.int32)
    large = jnp.minimum(large, half - 1)
    return ret + jnp.where(n < max_exact, n, large)


def _bias_kernel(v_ref, o_ref, *, tk, tq, lane_starts, visible):
    a = lax.broadcasted_iota(jnp.int32, (tk, tq), 0)
    b = lax.broadcasted_iota(jnp.int32, (tk, tq), 1)
    for s in range(2):
        x = jnp.broadcast_to(v_ref[s], (tk, v_ref.shape[-1]))
        x = pltpu.roll(x, 0, 1, stride=1, stride_axis=0)
        for t, start in enumerate(lane_starts):
            tile = x[:, start:start + tq]
            vis = visible(t, a, b)
            o_ref[t, :, s * tq:(s + 1) * tq] = tile if vis is None else jnp.where(vis, tile, MASKED)


def _bias_tiles(stream_vec_fn, tk, tq, offsets, visible):
    first_start = -(-tk // LANES) * LANES
    r0 = first_start + max(offsets)
    lane_starts = tuple(r0 - off for off in offsets)
    assert all(s % LANES == 0 for s in lane_starts) and tq % LANES == 0
    n_lanes = max(lane_starts) + tq
    v = stream_vec_fn(r0 - jnp.arange(n_lanes)).astype(F32) * LOG2E
    groups = v.shape[0]
    return pl.pallas_call(
        functools.partial(_bias_kernel, tk=tk, tq=tq, lane_starts=lane_starts, visible=visible),
        grid=(groups,),
        in_specs=[pl.BlockSpec((None, 2, 1, n_lanes), lambda g: (g, 0, 0, 0))],
        out_specs=pl.BlockSpec((len(offsets), None, tk, 2 * tq), lambda g: (0, g, 0, 0)),
        out_shape=jax.ShapeDtypeStruct((len(offsets), groups, tk, 2 * tq), F32),
        compiler_params=pltpu.CompilerParams(dimension_semantics=("arbitrary",),
                                             vmem_limit_bytes=VMEM_LIMIT),
        name="bias_tiles",
    )(v.reshape(groups, 2, 1, n_lanes))


def _diff_streams(table, shift=None):
    def fn(rel):
        v = jnp.transpose(table[_t5_bucket(rel)].astype(F32))
        if shift is not None:
            v = v - shift[:, None]
        return jnp.stack([v, v], axis=1)
    return fn


def _band_streams(table):
    def fn(rel):
        v = jnp.transpose(table[jnp.clip(rel, -REL_CLIP, REL_CLIP) + REL_CLIP].astype(F32))
        return v.reshape(N_HEADS_B // 2, 2, -1)
    return fn


def _diff_prompt_bias(t5_table):
    tk, tq = ATTN_BLOCK_K, ATTN_BLOCK_Q
    assert tk >= MAX_DISTANCE
    far = t5_table[_t5_bucket(jnp.array(-2 * MAX_DISTANCE))].astype(F32)
    offsets = _near_offsets(tk, tq, reach=tk)
    visible = lambda t, a, b: None if offsets[t] < 0 else (a + offsets[t]) // CHUNK <= b // CHUNK
    return _bias_tiles(_diff_streams(t5_table, far), tk, tq, offsets, visible)


def _band_prompt_bias(band_table):
    tk, tq = ATTN_BLOCK_K, ATTN_BLOCK_Q
    offsets = _near_offsets(tk, tq, reach=WINDOW_B)

    def visible(t, a, b):
        kc = (a + offsets[t]) // CHUNK
        qc = b // CHUNK
        return (kc <= qc) & (kc >= qc - BAND_CHUNKS)

    return _bias_tiles(_band_streams(band_table), tk, tq, offsets, visible)


def _near_offsets(tk, tq, reach):
    assert tq % tk == 0 and reach % tk == 0
    return tuple(range(tq - tk, -reach - 1, -tk))


def _store_transposed(vt_ref, v):
    blk = vt_ref.shape[-1]
    ones_rows = _ones_rows(blk)
    for h in range(vt_ref.shape[0]):
        for c in range(vt_ref.shape[1]):
            vt_ref[h, c, :LANES] = v[c * blk:(c + 1) * blk, h * LANES:(h + 1) * LANES].T.astype(BF16)
            vt_ref[h, c, LANES:] = ones_rows


def _ones_rows(n):
    row = lax.broadcasted_iota(jnp.int32, (VT_ROWS - LANES, n), 0)
    return jnp.where(row == 0, 1.0, 0.0).astype(BF16)


def _proj_kernel(x_ref, g_ref, w_ref, qa_ref, ka32_ref, ka16_ref, va32_ref, vta_ref,
                 qb_ref, kb16_ref, vtb_ref, kb32_ref, vb32_ref, *, tail_period):
    x = x_ref[...]
    xn = x * lax.rsqrt(jnp.mean(x * x, axis=-1, keepdims=True) + EPS) * g_ref[...]
    xn = xn.astype(BF16)

    def cols(c):
        return jnp.dot(xn, w_ref[:, c * WIDTH_A:(c + 1) * WIDTH_A], preferred_element_type=F32)

    qa_ref[...] = (cols(0) * Q_SCALE).astype(BF16)
    ka = cols(1)
    if len(ka32_ref.shape) == 2:
        ka32_ref[...] = ka
    else:
        for h in range(N_HEADS_A):
            t = ka[:, h * LANES:(h + 1) * LANES].T
            ka32_ref[h, 0] = t[:HEAD_DIM]
            ka32_ref[h, 1] = t[HEAD_DIM:]
    ka16_ref[...] = ka.astype(BF16)
    va = cols(2)
    va32_ref[...] = va.reshape(va32_ref.shape)
    _store_transposed(vta_ref, va)
    qb_ref[...] = (cols(3) * Q_SCALE).astype(BF16)
    kb = cols(4)
    kb16_ref[...] = kb.astype(BF16)
    vb = cols(5)
    _store_transposed(vtb_ref, vb)

    @pl.when(pl.program_id(0) % tail_period == tail_period - 1)
    def _():
        kb32_ref[...] = kb
        vb32_ref[...] = vb


def _project(x, g, w_bf, tail_period, time_major_k=False):
    n, d = x.shape
    tm = ROW_TILE
    blk = ATTN_BLOCK_K
    assert n % (tm * tail_period) == 0 and tm % blk == 0
    groups = WIDTH_A // LANES
    row = lambda i: (i, 0)
    tail = lambda i: (i // tail_period, 0)
    rows = lambda dt, imap=row: (jax.ShapeDtypeStruct((n, WIDTH_A), dt),
                                 pl.BlockSpec((tm, WIDTH_A), imap))
    vt = (jax.ShapeDtypeStruct((groups, n // blk, VT_ROWS, blk), BF16),
          pl.BlockSpec((groups, tm // blk, VT_ROWS, blk), lambda i: (0, i, 0, 0)))
    n_tail = n // tail_period
    tails = (jax.ShapeDtypeStruct((n_tail, WIDTH_B), F32), pl.BlockSpec((tm, WIDTH_B), tail))
    if time_major_k:
        per_batch = tail_period * tm
        ka32 = (jax.ShapeDtypeStruct((n // per_batch, N_HEADS_A, 2, HEAD_DIM, per_batch), F32),
                pl.BlockSpec((None, N_HEADS_A, 2, HEAD_DIM, tm),
                             lambda i: (i // tail_period, 0, 0, 0, i % tail_period)))
    else:
        ka32 = rows(F32)
    va32 = (jax.ShapeDtypeStruct((n, N_HEADS_A, LANES), F32),
            pl.BlockSpec((tm, N_HEADS_A, LANES), lambda i: (i, 0, 0)))
    outs = [rows(BF16), ka32, rows(BF16), va32, vt,
            rows(BF16), rows(BF16), vt, tails, tails]
    return pl.pallas_call(
        functools.partial(_proj_kernel, tail_period=tail_period),
        grid=(n // tm,),
        in_specs=[pl.BlockSpec((tm, d), row),
                  pl.BlockSpec((1, d), lambda i: (0, 0)),
                  pl.BlockSpec(w_bf.shape, lambda i: (0, 0), pipeline_mode=pl.Buffered(1))],
        out_specs=[s for _, s in outs],
        out_shape=[s for s, _ in outs],
        compiler_params=pltpu.CompilerParams(dimension_semantics=("arbitrary",),
                                             vmem_limit_bytes=VMEM_LIMIT),
        name="in_proj",
    )(x, g.reshape(1, d), w_bf)


def _attn_kernel(*refs, tq, tk, n_near, causal_walk, combine, lam_init):
    far_walk = causal_walk and combine == "diff"
    refs = list(refs)
    lam_ref = refs.pop(0) if combine == "diff" else None
    q_ref = refs.pop(0)
    q_next_ref = refs.pop(0) if causal_walk else None
    k_ref, vt_ref, bias_ref = refs[:3]
    del refs[:3]
    g_ref = refs.pop(0) if combine == "diff" else None
    o_ref = refs.pop(0)
    qi = pl.program_id(2)

    def two_streams(q):
        lane = lax.broadcasted_iota(jnp.int32, q.shape, 1)
        zero = jnp.zeros_like(q)
        return jnp.concatenate([jnp.where(lane < HEAD_DIM, q, zero),
                                jnp.where(lane >= HEAD_DIM, q, zero)], axis=0)

    def nt_dot(k, qz):
        return lax.dot_general(k, qz, (((1,), (1,)), ((), ())),
                               preferred_element_type=F32)

    def scores(qz, kb):
        return nt_dot(k_ref[pl.ds(pl.multiple_of(kb * tk, tk), tk), :], qz)

    def finish(acc):
        o = acc[:LANES, :] * (1.0 / acc[LANES:LANES + 1, :])
        if combine == "diff":
            d = o[:, :tq] - lam_ref[0] * o[:, tq:]
            y = d * lax.rsqrt(jnp.mean(d * d, axis=0, keepdims=True) + SUBLN_EPS)
            return ((y.T * g_ref[...]) * (1.0 - lam_init)).astype(o_ref.dtype)
        row = lax.broadcasted_iota(jnp.int32, (LANES, tq), 0)
        return jnp.where(row < HEAD_DIM, o[:, :tq], o[:, tq:]).T.astype(o_ref.dtype)

    if not causal_walk:
        assert n_near == 1
        entries = range(q_ref.shape[0])
        s = [nt_dot(k_ref[e], two_streams(q_ref[e])) + bias_ref[0] for e in entries]
        p = [jnp.exp2(s[e] - jnp.max(s[e], axis=0, keepdims=True)) for e in entries]
        acc = [jnp.dot(vt_ref[e], p[e].astype(BF16), preferred_element_type=F32)
               for e in entries]
        for e in entries:
            o_ref[e] = finish(acc[e])
    else:
        m_ref, acc_ref = refs[:2]
        sbuf_ref, gmax_ref = refs[-2:]
        group = GROUP_TILES
        half = group // GROUP_HALVES

        class Walk:
            def __init__(self, ref, tile):
                self.qz = two_streams(ref[...])
                self.n_visible = (tq // tk) * (tile + 1)
                self.n_tiles = (self.n_visible if far_walk
                                else jnp.minimum(self.n_visible, n_near))

            def key_tile(self, n):
                return jnp.maximum(self.n_visible - 1 - n, 0)

        this = Walk(q_ref, qi)
        following = Walk(q_next_ref, jnp.minimum(qi + 1, pl.num_programs(2) - 1))

        def store_scores(walk, g, t, hmax, masked):
            n = g * group + t
            s = scores(walk.qz, walk.key_tile(n))
            if isinstance(g, int) and n < n_near:
                s = s + bias_ref[n]
            if masked:
                s = jnp.where(n < walk.n_tiles, s, MASKED)
            sbuf_ref[t * tk:(t + 1) * tk, :] = s
            mt = jnp.max(s, axis=0, keepdims=True)
            return mt if hmax is None else jnp.maximum(hmax, mt)

        assert 2 * (tq // tk) >= group

        def first_group(walk, t, hmax):
            return store_scores(walk, 0, t, hmax, walk is this)

        def group_step(g, produce):
            m = m_ref[...]
            acc = acc_ref[...]
            for h in range(GROUP_HALVES):
                m_new = jnp.maximum(m, gmax_ref[h])
                alpha = jnp.exp2(m - m_new)
                pv = None
                hmax = None
                for t in range(h * half, (h + 1) * half):
                    s = sbuf_ref[t * tk:(t + 1) * tk, :]
                    if produce == "following":
                        hmax = first_group(following, t, hmax)
                    else:
                        hmax = store_scores(this, g + 1, t, hmax, produce == "masked")
                    p = jnp.exp2(s - m_new)
                    d = jnp.dot(vt_ref[this.key_tile(g * group + t)], p.astype(BF16),
                                preferred_element_type=F32)
                    pv = d if pv is None else pv + d
                gmax_ref[h] = hmax
                acc = alpha * acc + pv
                m = m_new
            acc_ref[...] = acc
            m_ref[...] = m

        m_ref[...] = jnp.full(m_ref.shape, MASKED, F32)
        acc_ref[...] = jnp.zeros(acc_ref.shape, F32)

        @pl.when(qi == 0)
        def _():
            for h in range(GROUP_HALVES):
                hmax = None
                for t in range(h * half, (h + 1) * half):
                    hmax = first_group(this, t, hmax)
                gmax_ref[h] = hmax

        if far_walk:
            n_groups = (this.n_tiles + group - 1) // group
            n_plain = jnp.maximum(n_groups - 2, 0)

            def plain_steps(first, count):
                for j in range(count):
                    group_step(first + j, "plain")

            def quad(i, carry):
                plain_steps(4 * i, 4)
                return carry

            lax.fori_loop(0, n_plain // 4, quad, 0)
            done = (n_plain // 4) * 4

            @pl.when(n_plain - done >= 2)
            def _():
                plain_steps(done, 2)

            @pl.when(n_plain % 2 == 1)
            def _():
                plain_steps(n_plain - 1, 1)

            @pl.when(n_groups >= 2)
            def _():
                group_step(n_groups - 2, "masked")
                group_step(n_groups - 1, "following")

            @pl.when(n_groups < 2)
            def _():
                group_step(n_groups - 1, "following")
        else:
            assert n_near <= group
            group_step(0, "following")

        o_ref[...] = finish(acc_ref[...])


def _attention(q, k, vt, bias, *, tq, tk, causal_walk, combine, out_dtype,
               lam=None, g=None, lam_init=0.0):
    b, tq_total, width = q.shape
    tk_total = k.shape[1]
    groups = width // LANES
    n_near = bias.shape[0]
    nkb = tk_total // tk
    assert tq_total % tq == 0 and vt.shape == (groups, b * nkb, VT_ROWS, tk)
    nq = tq_total // tq
    if causal_walk:
        grid = (b, groups, nq)
        at = lambda f: f
        nb = None
        vt_block = (None, nkb, VT_ROWS, tk)
    else:
        nb = FLAT_BATCH
        assert b % nb == 0 and nkb == 1
        grid = (groups, b // nb, nq)
        at = lambda f: lambda h, bi, qi: f(bi, h, qi)
        vt_block = (None, nb, VT_ROWS, tk)
    in_specs = [
        pl.BlockSpec((nb, tq, LANES), at(lambda bi, h, qi: (bi, qi, h))),
        pl.BlockSpec((nb, tk_total, LANES), at(lambda bi, h, qi: (bi, 0, h))),
        pl.BlockSpec(vt_block, at(lambda bi, h, qi: (h, bi, 0, 0))),
        pl.BlockSpec((n_near, None, tk, 2 * tq), at(lambda bi, h, qi: (0, h, 0, 0))),
    ]
    args = [q, k, vt, bias]
    if causal_walk:
        in_specs.insert(1, pl.BlockSpec((None, tq, LANES),
                                        lambda bi, h, qi: (bi, jnp.minimum(qi + 1, nq - 1), h)))
        args.insert(1, q)
    if combine == "diff":
        in_specs = [pl.BlockSpec(memory_space=pltpu.SMEM)] + in_specs
        in_specs.append(pl.BlockSpec((1, LANES), lambda *_: (0, 0)))
        args = [lam.reshape(1).astype(F32)] + args + [g.reshape(1, LANES).astype(F32)]
    scratch = []
    if causal_walk:
        scratch += [pltpu.VMEM((1, 2 * tq), F32), pltpu.VMEM((VT_ROWS, 2 * tq), F32)]
        scratch.append(pltpu.VMEM((GROUP_TILES * tk, 2 * tq), F32))
        scratch.append(pltpu.VMEM((GROUP_HALVES, 1, 2 * tq), F32))
    return pl.pallas_call(
        functools.partial(_attn_kernel, tq=tq, tk=tk, n_near=n_near, causal_walk=causal_walk,
                          combine=combine, lam_init=lam_init),
        grid=grid,
        in_specs=in_specs,
        out_specs=pl.BlockSpec((nb, tq, LANES), at(lambda bi, h, qi: (bi, qi, h))),
        out_shape=jax.ShapeDtypeStruct((b, tq_total, width), out_dtype),
        scratch_shapes=scratch,
        compiler_params=pltpu.CompilerParams(
            dimension_semantics=("arbitrary", "arbitrary", "arbitrary"),
            vmem_limit_bytes=VMEM_LIMIT),
        name="attn_" + combine + ("_walk" if causal_walk else "_flat"),
    )(*args)


def _transpose_values(v, tk):
    b, t, width = v.shape
    vt = v.reshape(b * (t // tk), tk, width // LANES, LANES).transpose(2, 0, 3, 1)
    ones = jnp.broadcast_to(_ones_rows(tk), vt.shape[:2] + (VT_ROWS - LANES, tk))
    return jnp.concatenate([vt, ones], axis=2)


def _rms(x, g, eps):
    return x * lax.rsqrt(jnp.mean(x * x, axis=-1, keepdims=True) + eps) * g


def _post_kernel(x_ref, oa_ref, ob_ref, p_ref, gb_ref, gm_ref, gf_ref, wout_ref, wup_ref,
                 wdown_ref, wgate_ref, wple_ref, y_ref):
    obn = _rms(ob_ref[...], gb_ref[...], EPS).astype(BF16)
    h = x_ref[...] + (
        jnp.dot(oa_ref[...], wout_ref[:WIDTH_A, :], preferred_element_type=F32)
        + jnp.dot(obn, wout_ref[WIDTH_A:, :], preferred_element_type=F32))
    c = _rms(h, gm_ref[...], EPS).astype(BF16)
    d_ff = wup_ref.shape[1]
    step = wup_ref.shape[0]
    for j in range(d_ff // step):
        u = jnp.dot(c, wup_ref[:, j * step:(j + 1) * step], preferred_element_type=F32)
        u = jnp.square(jnp.maximum(u, 0.0)).astype(BF16)
        h = h + jnp.dot(u, wdown_ref[j * step:(j + 1) * step, :], preferred_element_type=F32)
    gate = jax.nn.sigmoid(jnp.dot(h.astype(BF16), wgate_ref[...], preferred_element_type=F32))
    h = h + gate * jnp.dot(p_ref[...].astype(BF16), wple_ref[...], preferred_element_type=F32)
    y_ref[...] = _rms(h, gf_ref[...], EPS)


def _post(x, oa, ob, p, gb, gm, gf, wout, wup, wdown, wgate, wple):
    n, d = x.shape
    tm = min(ROW_TILE, n)
    assert n % tm == 0
    row = lambda i: (i, 0)
    fixed = lambda a: pl.BlockSpec(a.shape, lambda i: (0,) * a.ndim, pipeline_mode=pl.Buffered(1))
    vec = lambda a: a.reshape(1, -1).astype(F32)
    gb, gm, gf = vec(gb), vec(gm), vec(gf)
    return pl.pallas_call(
        _post_kernel,
        grid=(n // tm,),
        in_specs=[pl.BlockSpec((tm, d), row), pl.BlockSpec((tm, oa.shape[1]), row),
                  pl.BlockSpec((tm, ob.shape[1]), row), pl.BlockSpec((tm, p.shape[1]), row),
                  fixed(gb), fixed(gm), fixed(gf), fixed(wout), fixed(wup), fixed(wdown),
                  fixed(wgate), fixed(wple)],
        out_specs=pl.BlockSpec((tm, d), row),
        out_shape=jax.ShapeDtypeStruct((n, d), F32),
        compiler_params=pltpu.CompilerParams(dimension_semantics=("arbitrary",),
                                             vmem_limit_bytes=VMEM_LIMIT),
        name="post",
    )(x, oa, ob, p, gb, gm, gf, wout, wup, wdown, wgate, wple)


def _pad_rows(a, rows):
    return jnp.pad(a, ((0, 0), (0, rows - a.shape[1]), (0, 0)))


def _sample_bias(stream_vec_fn, n_keys, tk, tq, first_kpos, past, band):
    def visible(t, a, b):
        kpos = first_kpos + a
        kc = kpos // CHUNK
        qc = (past + b) // CHUNK
        vis = (kc <= qc) & (a < n_keys)
        if band:
            vis = vis & (kc >= qc - BAND_CHUNKS) & (kpos >= 0)
        return vis

    return _bias_tiles(stream_vec_fn, tk, tq, (first_kpos - past,), visible)


def kernel(x_prompt, x_sample, cache_a_k, cache_a_v, cache_b_k, cache_b_v, p_prompt, p_sample,
           t5_table, g_attn, w_in, lambda_q1, lambda_k1, lambda_q2, lambda_k2, subln_g,
           band_table, out_norm_b, w_out, g_mlp, w_up, w_down, w_ple_gate, w_ple_proj, g_final):
    depth = w_in.shape[0]
    assert depth == 1
    i = 0
    B, S, D = x_prompt.shape
    Bs, Ts, _ = x_sample.shape
    past = cache_a_k.shape[2]
    cache_b_len = cache_b_k.shape[2]
    keep = min(WINDOW_B, S)
    assert keep == ROW_TILE and S % ATTN_BLOCK_Q == 0 and Bs * Ts == ROW_TILE

    lam_init = 0.8 - 0.6 * math.exp(-0.3 * i)
    lam = (jnp.exp(jnp.sum(lambda_q1[i].astype(F32) * lambda_k1[i].astype(F32)))
           - jnp.exp(jnp.sum(lambda_q2[i].astype(F32) * lambda_k2[i].astype(F32)))
           + lam_init)

    w_in_bf = w_in[i].astype(BF16)
    post_w = (out_norm_b[i], g_mlp[i], g_final, w_out[i].astype(BF16), w_up[i].astype(BF16),
              w_down[i].astype(BF16), w_ple_gate[i].astype(BF16), w_ple_proj[i].astype(BF16))
    tk_p, tq_p = ATTN_BLOCK_K, ATTN_BLOCK_Q

    xp = x_prompt.reshape(B * S, D)
    (qa, ka32, ka16, va32, vta, qb, kb16, vtb, kb32, vb32) = _project(
        xp, g_attn[i], w_in_bf, tail_period=S // ROW_TILE, time_major_k=True)
    to3 = lambda a: a.reshape(B, S, a.shape[-1])
    oa = _attention(to3(qa), to3(ka16), vta, _diff_prompt_bias(t5_table), tq=tq_p, tk=tk_p,
                    causal_walk=True, combine="diff", out_dtype=BF16, lam=lam, g=subln_g[i],
                    lam_init=lam_init)
    ob = _attention(to3(qb), to3(kb16), vtb, _band_prompt_bias(band_table[i]), tq=tq_p, tk=tk_p,
                    causal_walk=True, combine="band", out_dtype=F32)
    y_prompt = _post(xp, oa.reshape(B * S, WIDTH_A), ob.reshape(B * S, WIDTH_B),
                     p_prompt[i].reshape(B * S, -1), *post_w).reshape(B, S, D)

    xs = x_sample.reshape(Bs * Ts, D)
    (qa_s, ka32_s, ka16_s, va32_s, _, qb_s, kb16_s, _, kb32_s, vb32_s) = _project(
        xs, g_attn[i], w_in_bf, tail_period=1)
    to3s = lambda a: a.reshape(Bs, Ts, -1)
    tq_s = LANES
    assert Ts <= tq_s

    def sample_keys(cache, new):
        c = cache.reshape(Bs, cache.shape[1], -1).astype(BF16)
        allk = jnp.concatenate([c, to3s(new).astype(BF16)], axis=1)
        n_keys = allk.shape[1]
        tk = -(-n_keys // LANES) * LANES
        return _pad_rows(allk, tk), n_keys, tk

    ka_all, n_a, tk_a = sample_keys(cache_a_k[i], ka16_s)
    va_all, _, _ = sample_keys(cache_a_v[i], va32_s)
    kb_all, n_b, tk_b = sample_keys(cache_b_k[i], kb16_s)
    vb_all, _, _ = sample_keys(cache_b_v[i], vb32_s)

    bias_a = _sample_bias(_diff_streams(t5_table), n_a, tk_a, tq_s, 0, past, band=False)
    bias_b = _sample_bias(_band_streams(band_table[i]), n_b, tk_b, tq_s,
                          past - cache_b_len, past, band=True)

    oa_s = _attention(_pad_rows(to3s(qa_s), tq_s), ka_all, _transpose_values(va_all, tk_a),
                      bias_a, tq=tq_s, tk=tk_a, causal_walk=False, combine="diff",
                      out_dtype=BF16, lam=lam, g=subln_g[i], lam_init=lam_init)[:, :Ts]
    ob_s = _attention(_pad_rows(to3s(qb_s), tq_s), kb_all, _transpose_values(vb_all, tk_b),
                      bias_b, tq=tq_s, tk=tk_b, causal_walk=False, combine="band",
                      out_dtype=F32)[:, :Ts]
    y_sample = _post(xs, oa_s.reshape(Bs * Ts, WIDTH_A), ob_s.reshape(Bs * Ts, WIDTH_B),
                     p_sample[i].reshape(Bs * Ts, -1), *post_w).reshape(Bs, Ts, D)

    return (y_prompt, y_sample,
            ka32.transpose(0, 4, 1, 2, 3)[None],
            va32.reshape(1, B, S, N_HEADS_A, 2 * HEAD_DIM),
            kb32.reshape(1, B, keep, N_HEADS_B, HEAD_DIM),
            vb32.reshape(1, B, keep, N_HEADS_B, HEAD_DIM),
            ka32_s.reshape(1, Bs, Ts, N_HEADS_A, 2, HEAD_DIM),
            va32_s.reshape(1, Bs, Ts, N_HEADS_A, 2 * HEAD_DIM),
            kb32_s.reshape(1, Bs, Ts, N_HEADS_B, HEAD_DIM),
            vb32_s.reshape(1, Bs, Ts, N_HEADS_B, HEAD_DIM))
```

```python
import functools
import math

import jax
import jax.numpy as jnp
from jax import lax
from jax.experimental import pallas as pl
from jax.experimental.pallas import tpu as pltpu

CHUNK = 64
HEAD_DIM = 64
N_HEADS_A = 4
N_HEADS_B = 8
WIDTH_A = N_HEADS_A * 2 * HEAD_DIM
WIDTH_B = N_HEADS_B * HEAD_DIM
N_BUCKETS = 32
MAX_DISTANCE = 128
BAND_CHUNKS = 8
WINDOW_B = BAND_CHUNKS * CHUNK
REL_CLIP = 128
EPS = 1e-6
SUBLN_EPS = 1e-5
MASKED = -1e30
LOG2E = math.log2(math.e)
Q_SCALE = LOG2E * HEAD_DIM ** -0.5

LANES = 128
VT_ROWS = LANES + 16
ATTN_BLOCK_K = 256
ATTN_BLOCK_Q = 512
GROUP_TILES = 4
GROUP_HALVES = 1
FLAT_BATCH = 8
ROW_TILE = 512
VMEM_LIMIT = 56 * 1024 * 1024

F32 = jnp.float32
BF16 = jnp.bfloat16


def _t5_bucket(rel):
    half = N_BUCKETS // 2
    n = -rel
    ret = jnp.where(n < 0, half, 0)
    n = jnp.abs(n)
    max_exact = half // 2
    nf = jnp.maximum(n, 1).astype(F32)
    large = max_exact + (jnp.log(nf / max_exact) / math.log(MAX_DISTANCE / max_exact)
                         * (half - max_exact)).astype(jnp.int32)
    large = jnp.minimum(large, half - 1)
    return ret + jnp.where(n < max_exact, n, large)


def _bias_kernel(v_ref, o_ref, *, tk, tq, lane_starts, visible):
    a = lax.broadcasted_iota(jnp.int32, (tk, tq), 0)
    b = lax.broadcasted_iota(jnp.int32, (tk, tq), 1)
    for s in range(2):
        x = jnp.broadcast_to(v_ref[s], (tk, v_ref.shape[-1]))
        x = pltpu.roll(x, 0, 1, stride=1, stride_axis=0)
        for t, start in enumerate(lane_starts):
            tile = x[:, start:start + tq]
            vis = visible(t, a, b)
            o_ref[t, :, s * tq:(s + 1) * tq] = tile if vis is None else jnp.where(vis, tile, MASKED)


def _bias_tiles(stream_vec_fn, tk, tq, offsets, visible):
    first_start = -(-tk // LANES) * LANES
    r0 = first_start + max(offsets)
    lane_starts = tuple(r0 - off for off in offsets)
    assert all(s % LANES == 0 for s in lane_starts) and tq % LANES == 0
    n_lanes = max(lane_starts) + tq
    v = stream_vec_fn(r0 - jnp.arange(n_lanes)).astype(F32) * LOG2E
    groups = v.shape[0]
    return pl.pallas_call(
        functools.partial(_bias_kernel, tk=tk, tq=tq, lane_starts=lane_starts, visible=visible),
        grid=(groups,),
        in_specs=[pl.BlockSpec((None, 2, 1, n_lanes), lambda g: (g, 0, 0, 0))],
        out_specs=pl.BlockSpec((len(offsets), None, tk, 2 * tq), lambda g: (0, g, 0, 0)),
        out_shape=jax.ShapeDtypeStruct((len(offsets), groups, tk, 2 * tq), F32),
        compiler_params=pltpu.CompilerParams(dimension_semantics=("arbitrary",),
                                             vmem_limit_bytes=VMEM_LIMIT),
        name="bias_tiles",
    )(v.reshape(groups, 2, 1, n_lanes))


def _diff_streams(table, shift=None):
    def fn(rel):
        v = jnp.transpose(table[_t5_bucket(rel)].astype(F32))
        if shift is not None:
            v = v - shift[:, None]
        return jnp.stack([v, v], axis=1)
    return fn


def _band_streams(table):
    def fn(rel):
        v = jnp.transpose(table[jnp.clip(rel, -REL_CLIP, REL_CLIP) + REL_CLIP].astype(F32))
        return v.reshape(N_HEADS_B // 2, 2, -1)
    return fn


def _diff_prompt_bias(t5_table):
    tk, tq = ATTN_BLOCK_K, ATTN_BLOCK_Q
    assert tk >= MAX_DISTANCE
    far = t5_table[_t5_bucket(jnp.array(-2 * MAX_DISTANCE))].astype(F32)
    offsets = _near_offsets(tk, tq, reach=tk)
    visible = lambda t, a, b: None if offsets[t] < 0 else (a + offsets[t]) // CHUNK <= b // CHUNK
    return _bias_tiles(_diff_streams(t5_table, far), tk, tq, offsets, visible)


def _band_prompt_bias(band_table):
    tk, tq = ATTN_BLOCK_K, ATTN_BLOCK_Q
    offsets = _near_offsets(tk, tq, reach=WINDOW_B)

    def visible(t, a, b):
        kc = (a + offsets[t]) // CHUNK
        qc = b // CHUNK
        return (kc <= qc) & (kc >= qc - BAND_CHUNKS)

    return _bias_tiles(_band_streams(band_table), tk, tq, offsets, visible)


def _near_offsets(tk, tq, reach):
    assert tq % tk == 0 and reach % tk == 0
    return tuple(range(tq - tk, -reach - 1, -tk))


def _store_transposed(vt_ref, v):
    blk = vt_ref.shape[-1]
    ones_rows = _ones_rows(blk)
    for h in range(vt_ref.shape[0]):
        for c in range(vt_ref.shape[1]):
            vt_ref[h, c, :LANES] = v[c * blk:(c + 1) * blk, h * LANES:(h + 1) * LANES].T.astype(BF16)
            vt_ref[h, c, LANES:] = ones_rows


def _ones_rows(n):
    row = lax.broadcasted_iota(jnp.int32, (VT_ROWS - LANES, n), 0)
    return jnp.where(row == 0, 1.0, 0.0).astype(BF16)


def _proj_kernel(x_ref, g_ref, w_ref, qa_ref, ka32_ref, ka16_ref, va32_ref, vta_ref,
                 qb_ref, kb16_ref, vtb_ref, kb32_ref, vb32_ref, *, tail_period):
    x = x_ref[...]
    xn = x * lax.rsqrt(jnp.mean(x * x, axis=-1, keepdims=True) + EPS) * g_ref[...]
    xn = xn.astype(BF16)

    def cols(c):
        return jnp.dot(xn, w_ref[:, c * WIDTH_A:(c + 1) * WIDTH_A], preferred_element_type=F32)

    qa_ref[...] = (cols(0) * Q_SCALE).astype(BF16)
    ka = cols(1)
    if len(ka32_ref.shape) == 2:
        ka32_ref[...] = ka
    else:
        for h in range(N_HEADS_A):
            t = ka[:, h * LANES:(h + 1) * LANES].T
            ka32_ref[h, 0] = t[:HEAD_DIM]
            ka32_ref[h, 1] = t[HEAD_DIM:]
    ka16_ref[...] = ka.astype(BF16)
    va = cols(2)
    va32_ref[...] = va.reshape(va32_ref.shape)
    _store_transposed(vta_ref, va)
    qb_ref[...] = (cols(3) * Q_SCALE).astype(BF16)
    kb = cols(4)
    kb16_ref[...] = kb.astype(BF16)
    vb = cols(5)
    _store_transposed(vtb_ref, vb)

    @pl.when(pl.program_id(0) % tail_period == tail_period - 1)
    def _():
        kb32_ref[...] = kb
        vb32_ref[...] = vb


def _project(x, g, w_bf, tail_period, time_major_k=False):
    n, d = x.shape
    tm = ROW_TILE
    blk = ATTN_BLOCK_K
    assert n % (tm * tail_period) == 0 and tm % blk == 0
    groups = WIDTH_A // LANES
    row = lambda i: (i, 0)
    tail = lambda i: (i // tail_period, 0)
    rows = lambda dt, imap=row: (jax.ShapeDtypeStruct((n, WIDTH_A), dt),
                                 pl.BlockSpec((tm, WIDTH_A), imap))
    vt = (jax.ShapeDtypeStruct((groups, n // blk, VT_ROWS, blk), BF16),
          pl.BlockSpec((groups, tm // blk, VT_ROWS, blk), lambda i: (0, i, 0, 0)))
    n_tail = n // tail_period
    tails = (jax.ShapeDtypeStruct((n_tail, WIDTH_B), F32), pl.BlockSpec((tm, WIDTH_B), tail))
    if time_major_k:
        per_batch = tail_period * tm
        ka32 = (jax.ShapeDtypeStruct((n // per_batch, N_HEADS_A, 2, HEAD_DIM, per_batch), F32),
                pl.BlockSpec((None, N_HEADS_A, 2, HEAD_DIM, tm),
                             lambda i: (i // tail_period, 0, 0, 0, i % tail_period)))
    else:
        ka32 = rows(F32)
    va32 = (jax.ShapeDtypeStruct((n, N_HEADS_A, LANES), F32),
            pl.BlockSpec((tm, N_HEADS_A, LANES), lambda i: (i, 0, 0)))
    outs = [rows(BF16), ka32, rows(BF16), va32, vt,
            rows(BF16), rows(BF16), vt, tails, tails]
    return pl.pallas_call(
        functools.partial(_proj_kernel, tail_period=tail_period),
        grid=(n // tm,),
        in_specs=[pl.BlockSpec((tm, d), row),
                  pl.BlockSpec((1, d), lambda i: (0, 0)),
                  pl.BlockSpec(w_bf.shape, lambda i: (0, 0), pipeline_mode=pl.Buffered(1))],
        out_specs=[s for _, s in outs],
        out_shape=[s for s, _ in outs],
        compiler_params=pltpu.CompilerParams(dimension_semantics=("arbitrary",),
                                             vmem_limit_bytes=VMEM_LIMIT),
        name="in_proj",
    )(x, g.reshape(1, d), w_bf)


def _attn_kernel(*refs, tq, tk, n_near, causal_walk, combine, lam_init):
    far_walk = causal_walk and combine == "diff"
    refs = list(refs)
    lam_ref = refs.pop(0) if combine == "diff" else None
    q_ref = refs.pop(0)
    q_next_ref = refs.pop(0) if causal_walk else None
    k_ref, vt_ref, bias_ref = refs[:3]
    del refs[:3]
    g_ref = refs.pop(0) if combine == "diff" else None
    o_ref = refs.pop(0)
    qi = pl.program_id(2)

    def two_streams(q):
        lane = lax.broadcasted_iota(jnp.int32, q.shape, 1)
        zero = jnp.zeros_like(q)
        return jnp.concatenate([jnp.where(lane < HEAD_DIM, q, zero),
                                jnp.where(lane >= HEAD_DIM, q, zero)], axis=0)

    def nt_dot(k, qz):
        return lax.dot_general(k, qz, (((1,), (1,)), ((), ())),
                               preferred_element_type=F32)

    def scores(qz, kb):
        return nt_dot(k_ref[pl.ds(pl.multiple_of(kb * tk, tk), tk), :], qz)

    def finish(acc):
        o = acc[:LANES, :] * (1.0 / acc[LANES:LANES + 1, :])
        if combine == "diff":
            d = o[:, :tq] - lam_ref[0] * o[:, tq:]
            y = d * lax.rsqrt(jnp.mean(d * d, axis=0, keepdims=True) + SUBLN_EPS)
            return ((y.T * g_ref[...]) * (1.0 - lam_init)).astype(o_ref.dtype)
        row = lax.broadcasted_iota(jnp.int32, (LANES, tq), 0)
        return jnp.where(row < HEAD_DIM, o[:, :tq], o[:, tq:]).T.astype(o_ref.dtype)

    if not causal_walk:
        assert n_near == 1
        entries = range(q_ref.shape[0])
        s = [nt_dot(k_ref[e], two_streams(q_ref[e])) + bias_ref[0] for e in entries]
        p = [jnp.exp2(s[e] - jnp.max(s[e], axis=0, keepdims=True)) for e in entries]
        acc = [jnp.dot(vt_ref[e], p[e].astype(BF16), preferred_element_type=F32)
               for e in entries]
        for e in entries:
            o_ref[e] = finish(acc[e])
    else:
        m_ref, acc_ref = refs[:2]
        sbuf_ref, gmax_ref = refs[-2:]
        group = GROUP_TILES
        half = group // GROUP_HALVES

        class Walk:
            def __init__(self, ref, tile):
                self.qz = two_streams(ref[...])
                self.n_visible = (tq // tk) * (tile + 1)
                self.n_tiles = (self.n_visible if far_walk
                                else jnp.minimum(self.n_visible, n_near))

            def key_tile(self, n):
                return jnp.maximum(self.n_visible - 1 - n, 0)

        this = Walk(q_ref, qi)
        following = Walk(q_next_ref, jnp.minimum(qi + 1, pl.num_programs(2) - 1))

        def store_scores(walk, g, t, hmax, masked):
            n = g * group + t
            s = scores(walk.qz, walk.key_tile(n))
            if isinstance(g, int) and n < n_near:
                s = s + bias_ref[n]
            if masked:
                s = jnp.where(n < walk.n_tiles, s, MASKED)
            sbuf_ref[t * tk:(t + 1) * tk, :] = s
            mt = jnp.max(s, axis=0, keepdims=True)
            return mt if hmax is None else jnp.maximum(hmax, mt)

        assert 2 * (tq // tk) >= group

        def first_group(walk, t, hmax):
            return store_scores(walk, 0, t, hmax, walk is this)

        def group_step(g, produce):
            m = m_ref[...]
            acc = acc_ref[...]
            for h in range(GROUP_HALVES):
                m_new = jnp.maximum(m, gmax_ref[h])
                alpha = jnp.exp2(m - m_new)
                pv = None
                hmax = None
                for t in range(h * half, (h + 1) * half):
                    s = sbuf_ref[t * tk:(t + 1) * tk, :]
                    if produce == "following":
                        hmax = first_group(following, t, hmax)
                    else:
                        hmax = store_scores(this, g + 1, t, hmax, produce == "masked")
                    p = jnp.exp2(s - m_new)
                    d = jnp.dot(vt_ref[this.key_tile(g * group + t)], p.astype(BF16),
                                preferred_element_type=F32)
                    pv = d if pv is None else pv + d
                gmax_ref[h] = hmax
                acc = alpha * acc + pv
                m = m_new
            acc_ref[...] = acc
            m_ref[...] = m

        m_ref[...] = jnp.full(m_ref.shape, MASKED, F32)
        acc_ref[...] = jnp.zeros(acc_ref.shape, F32)

        @pl.when(qi == 0)
        def _():
            for h in range(GROUP_HALVES):
                hmax = None
                for t in range(h * half, (h + 1) * half):
                    hmax = first_group(this, t, hmax)
                gmax_ref[h] = hmax

        if far_walk:
            n_groups = (this.n_tiles + group - 1) // group
            n_plain = jnp.maximum(n_groups - 2, 0)

            def plain_steps(first, count):
                for j in range(count):
                    group_step(first + j, "plain")

            def quad(i, carry):
                plain_steps(4 * i, 4)
                return carry

            lax.fori_loop(0, n_plain // 4, quad, 0)
            done = (n_plain // 4) * 4

            @pl.when(n_plain - done >= 2)
            def _():
                plain_steps(done, 2)

            @pl.when(n_plain % 2 == 1)
            def _():
                plain_steps(n_plain - 1, 1)

            @pl.when(n_groups >= 2)
            def _():
                group_step(n_groups - 2, "masked")
                group_step(n_groups - 1, "following")

            @pl.when(n_groups < 2)
            def _():
                group_step(n_groups - 1, "following")
        else:
            assert n_near <= group
            group_step(0, "following")

        o_ref[...] = finish(acc_ref[...])


def _attention(q, k, vt, bias, *, tq, tk, causal_walk, combine, out_dtype,
               lam=None, g=None, lam_init=0.0):
    b, tq_total, width = q.shape
    tk_total = k.shape[1]
    groups = width // LANES
    n_near = bias.shape[0]
    nkb = tk_total // tk
    assert tq_total % tq == 0 and vt.shape == (groups, b * nkb, VT_ROWS, tk)
    nq = tq_total // tq
    if causal_walk:
        grid = (b, groups, nq)
        at = lambda f: f
        nb = None
        vt_block = (None, nkb, VT_ROWS, tk)
    else:
        nb = FLAT_BATCH
        assert b % nb == 0 and nkb == 1
        grid = (groups, b // nb, nq)
        at = lambda f: lambda h, bi, qi: f(bi, h, qi)
        vt_block = (None, nb, VT_ROWS, tk)
    in_specs = [
        pl.BlockSpec((nb, tq, LANES), at(lambda bi, h, qi: (bi, qi, h))),
        pl.BlockSpec((nb, tk_total, LANES), at(lambda bi, h, qi: (bi, 0, h))),
        pl.BlockSpec(vt_block, at(lambda bi, h, qi: (h, bi, 0, 0))),
        pl.BlockSpec((n_near, None, tk, 2 * tq), at(lambda bi, h, qi: (0, h, 0, 0))),
    ]
    args = [q, k, vt, bias]
    if causal_walk:
        in_specs.insert(1, pl.BlockSpec((None, tq, LANES),
                                        lambda bi, h, qi: (bi, jnp.minimum(qi + 1, nq - 1), h)))
        args.insert(1, q)
    if combine == "diff":
        in_specs = [pl.BlockSpec(memory_space=pltpu.SMEM)] + in_specs
        in_specs.append(pl.BlockSpec((1, LANES), lambda *_: (0, 0)))
        args = [lam.reshape(1).astype(F32)] + args + [g.reshape(1, LANES).astype(F32)]
    scratch = []
    if causal_walk:
        scratch += [pltpu.VMEM((1, 2 * tq), F32), pltpu.VMEM((VT_ROWS, 2 * tq), F32)]
        scratch.append(pltpu.VMEM((GROUP_TILES * tk, 2 * tq), F32))
        scratch.append(pltpu.VMEM((GROUP_HALVES, 1, 2 * tq), F32))
    return pl.pallas_call(
        functools.partial(_attn_kernel, tq=tq, tk=tk, n_near=n_near, causal_walk=causal_walk,
                          combine=combine, lam_init=lam_init),
        grid=grid,
        in_specs=in_specs,
        out_specs=pl.BlockSpec((nb, tq, LANES), at(lambda bi, h, qi: (bi, qi, h))),
        out_shape=jax.ShapeDtypeStruct((b, tq_total, width), out_dtype),
        scratch_shapes=scratch,
        compiler_params=pltpu.CompilerParams(
            dimension_semantics=("arbitrary", "arbitrary", "arbitrary"),
            vmem_limit_bytes=VMEM_LIMIT),
        name="attn_" + combine + ("_walk" if causal_walk else "_flat"),
    )(*args)


def _transpose_values(v, tk):
    b, t, width = v.shape
    vt = v.reshape(b * (t // tk), tk, width // LANES, LANES).transpose(2, 0, 3, 1)
    ones = jnp.broadcast_to(_ones_rows(tk), vt.shape[:2] + (VT_ROWS - LANES, tk))
    return jnp.concatenate([vt, ones], axis=2)


def _rms(x, g, eps):
    return x * lax.rsqrt(jnp.mean(x * x, axis=-1, keepdims=True) + eps) * g


def _post_kernel(x_ref, oa_ref, ob_ref, p_ref, gb_ref, gm_ref, gf_ref, wout_ref, wup_ref,
                 wdown_ref, wgate_ref, wple_ref, y_ref):
    obn = _rms(ob_ref[...], gb_ref[...], EPS).astype(BF16)
    h = x_ref[...] + (
        jnp.dot(oa_ref[...], wout_ref[:WIDTH_A, :], preferred_element_type=F32)
        + jnp.dot(obn, wout_ref[WIDTH_A:, :], preferred_element_type=F32))
    c = _rms(h, gm_ref[...], EPS).astype(BF16)
    d_ff = wup_ref.shape[1]
    step = wup_ref.shape[0]
    for j in range(d_ff // step):
        u = jnp.dot(c, wup_ref[:, j * step:(j + 1) * step], preferred_element_type=F32)
        u = jnp.square(jnp.maximum(u, 0.0)).astype(BF16)
        h = h + jnp.dot(u, wdown_ref[j * step:(j + 1) * step, :], preferred_element_type=F32)
    gate = jax.nn.sigmoid(jnp.dot(h.astype(BF16), wgate_ref[...], preferred_element_type=F32))
    h = h + gate * jnp.dot(p_ref[...].astype(BF16), wple_ref[...], preferred_element_type=F32)
    y_ref[...] = _rms(h, gf_ref[...], EPS)


def _post(x, oa, ob, p, gb, gm, gf, wout, wup, wdown, wgate, wple):
    n, d = x.shape
    tm = min(ROW_TILE, n)
    assert n % tm == 0
    row = lambda i: (i, 0)
    fixed = lambda a: pl.BlockSpec(a.shape, lambda i: (0,) * a.ndim, pipeline_mode=pl.Buffered(1))
    vec = lambda a: a.reshape(1, -1).astype(F32)
    gb, gm, gf = vec(gb), vec(gm), vec(gf)
    return pl.pallas_call(
        _post_kernel,
        grid=(n // tm,),
        in_specs=[pl.BlockSpec((tm, d), row), pl.BlockSpec((tm, oa.shape[1]), row),
                  pl.BlockSpec((tm, ob.shape[1]), row), pl.BlockSpec((tm, p.shape[1]), row),
                  fixed(gb), fixed(gm), fixed(gf), fixed(wout), fixed(wup), fixed(wdown),
                  fixed(wgate), fixed(wple)],
        out_specs=pl.BlockSpec((tm, d), row),
        out_shape=jax.ShapeDtypeStruct((n, d), F32),
        compiler_params=pltpu.CompilerParams(dimension_semantics=("arbitrary",),
                                             vmem_limit_bytes=VMEM_LIMIT),
        name="post",
    )(x, oa, ob, p, gb, gm, gf, wout, wup, wdown, wgate, wple)


def _pad_rows(a, rows):
    return jnp.pad(a, ((0, 0), (0, rows - a.shape[1]), (0, 0)))


def _sample_bias(stream_vec_fn, n_keys, tk, tq, first_kpos, past, band):
    def visible(t, a, b):
        kpos = first_kpos + a
        kc = kpos // CHUNK
        qc = (past + b) // CHUNK
        vis = (kc <= qc) & (a < n_keys)
        if band:
            vis = vis & (kc >= qc - BAND_CHUNKS) & (kpos >= 0)
        return vis

    return _bias_tiles(stream_vec_fn, tk, tq, (first_kpos - past,), visible)


def kernel(x_prompt, x_sample, cache_a_k, cache_a_v, cache_b_k, cache_b_v, p_prompt, p_sample,
           t5_table, g_attn, w_in, lambda_q1, lambda_k1, lambda_q2, lambda_k2, subln_g,
           band_table, out_norm_b, w_out, g_mlp, w_up, w_down, w_ple_gate, w_ple_proj, g_final):
    depth = w_in.shape[0]
    assert depth == 1
    i = 0
    B, S, D = x_prompt.shape
    Bs, Ts, _ = x_sample.shape
    past = cache_a_k.shape[2]
    cache_b_len = cache_b_k.shape[2]
    keep = min(WINDOW_B, S)
    assert keep == ROW_TILE and S % ATTN_BLOCK_Q == 0 and Bs * Ts == ROW_TILE

    lam_init = 0.8 - 0.6 * math.exp(-0.3 * i)
    lam = (jnp.exp(jnp.sum(lambda_q1[i].astype(F32) * lambda_k1[i].astype(F32)))
           - jnp.exp(jnp.sum(lambda_q2[i].astype(F32) * lambda_k2[i].astype(F32)))
           + lam_init)

    w_in_bf = w_in[i].astype(BF16)
    post_w = (out_norm_b[i], g_mlp[i], g_final, w_out[i].astype(BF16), w_up[i].astype(BF16),
              w_down[i].astype(BF16), w_ple_gate[i].astype(BF16), w_ple_proj[i].astype(BF16))
    tk_p, tq_p = ATTN_BLOCK_K, ATTN_BLOCK_Q

    xp = x_prompt.reshape(B * S, D)
    (qa, ka32, ka16, va32, vta, qb, kb16, vtb, kb32, vb32) = _project(
        xp, g_attn[i], w_in_bf, tail_period=S // ROW_TILE, time_major_k=True)
    to3 = lambda a: a.reshape(B, S, a.shape[-1])
    oa = _attention(to3(qa), to3(ka16), vta, _diff_prompt_bias(t5_table), tq=tq_p, tk=tk_p,
                    causal_walk=True, combine="diff", out_dtype=BF16, lam=lam, g=subln_g[i],
                    lam_init=lam_init)
    ob = _attention(to3(qb), to3(kb16), vtb, _band_prompt_bias(band_table[i]), tq=tq_p, tk=tk_p,
                    causal_walk=True, combine="band", out_dtype=F32)
    y_prompt = _post(xp, oa.reshape(B * S, WIDTH_A), ob.reshape(B * S, WIDTH_B),
                     p_prompt[i].reshape(B * S, -1), *post_w).reshape(B, S, D)

    xs = x_sample.reshape(Bs * Ts, D)
    (qa_s, ka32_s, ka16_s, va32_s, _, qb_s, kb16_s, _, kb32_s, vb32_s) = _project(
        xs, g_attn[i], w_in_bf, tail_period=1)
    to3s = lambda a: a.reshape(Bs, Ts, -1)
    tq_s = LANES
    assert Ts <= tq_s

    def sample_keys(cache, new):
        c = cache.reshape(Bs, cache.shape[1], -1).astype(BF16)
        allk = jnp.concatenate([c, to3s(new).astype(BF16)], axis=1)
        n_keys = allk.shape[1]
        tk = -(-n_keys // LANES) * LANES
        return _pad_rows(allk, tk), n_keys, tk

    ka_all, n_a, tk_a = sample_keys(cache_a_k[i], ka16_s)
    va_all, _, _ = sample_keys(cache_a_v[i], va32_s)
    kb_all, n_b, tk_b = sample_keys(cache_b_k[i], kb16_s)
    vb_all, _, _ = sample_keys(cache_b_v[i], vb32_s)

    bias_a = _sample_bias(_diff_streams(t5_table), n_a, tk_a, tq_s, 0, past, band=False)
    bias_b = _sample_bias(_band_streams(band_table[i]), n_b, tk_b, tq_s,
                          past - cache_b_len, past, band=True)

    oa_s = _attention(_pad_rows(to3s(qa_s), tq_s), ka_all, _transpose_values(va_all, tk_a),
                      bias_a, tq=tq_s, tk=tk_a, causal_walk=False, combine="diff",
                      out_dtype=BF16, lam=lam, g=subln_g[i], lam_init=lam_init)[:, :Ts]
    ob_s = _attention(_pad_rows(to3s(qb_s), tq_s), kb_all, _transpose_values(vb_all, tk_b),
                      bias_b, tq=tq_s, tk=tk_b, causal_walk=False, combine="band",
                      out_dtype=F32)[:, :Ts]
    y_sample = _post(xs, oa_s.reshape(Bs * Ts, WIDTH_A), ob_s.reshape(Bs * Ts, WIDTH_B),
                     p_sample[i].reshape(Bs * Ts, -1), *post_w).reshape(Bs, Ts, D)

    return (y_prompt, y_sample,
            ka32.transpose(0, 4, 1, 2, 3)[None],
            va32.reshape(1, B, S, N_HEADS_A, 2 * HEAD_DIM),
            kb32.reshape(1, B, keep, N_HEADS_B, HEAD_DIM),
            vb32.reshape(1, B, keep, N_HEADS_B, HEAD_DIM),
            ka32_s.reshape(1, Bs, Ts, N_HEADS_A, 2, HEAD_DIM),
            va32_s.reshape(1, Bs, Ts, N_HEADS_A, 2 * HEAD_DIM),
            kb32_s.reshape(1, Bs, Ts, N_HEADS_B, HEAD_DIM),
            vb32_s.reshape(1, Bs, Ts, N_HEADS_B, HEAD_DIM))
```
